```python
import math
import jax, jax.numpy as jnp
from jax import lax
import numpy as np

D_MODEL = 1024
BATCH = 8
SEQ = 4096
DEPTH = 4

N_MIXERS = 3
N_LAYERS_A = (DEPTH + 2) // 3
N_LAYERS_B = (DEPTH + 1) // 3
N_LAYERS_C = DEPTH // 3
EPS = 1e-6
ROPE_THETA = 10000.0
MASK_VALUE = -1e30
TINY = 1e-30

HG_HEADS = 8
HG_DK = 128
HG_DV = D_MODEL // HG_HEADS
HG_CHUNK = 64

AT_GROUPS = ((128, 1), (512, 4), (2048, 16))
AT_NGROUP = 3
AT_HEADS = 8
AT_HEAD_DIM = 64
AT_BLOCK = 128

RT_HEADS = 4
RT_DK = D_MODEL // RT_HEADS
RT_DV = 2 * RT_DK
RT_CHUNK = 128

PK_HEADS = 8
PK_NKEYS = 128
PK_EXPERTS = PK_NKEYS * PK_NKEYS
PK_DQ = 256
PK_TOPK = 16
PK_TOKEN_BLOCK = 128

kernel_name = 'hybrid_hgrn2_dilattn_retnet_peer'

F32 = jnp.float32


def _rmsnorm(x, gain):
    x32 = x.astype(F32)
    y = x32 * lax.rsqrt(jnp.mean(x32 * x32, axis=-1, keepdims=True) + EPS)
    return (y * gain.astype(F32)).astype(x.dtype)


def _rope(x, positions):
    half = x.shape[-1] // 2
    inv_freq = ROPE_THETA ** (-jnp.arange(half, dtype=F32) / half)
    ang = positions.astype(F32)[..., None] * inv_freq
    cos = jnp.cos(ang)[:, :, None, :]
    sin = jnp.sin(ang)[:, :, None, :]
    x32 = x.astype(F32)
    x1, x2 = x32[..., :half], x32[..., half:]
    return jnp.concatenate([x1 * cos - x2 * sin, x2 * cos + x1 * sin], axis=-1).astype(x.dtype)


def _gla_chunk_scan(q, k, v, log_f):
    B, S, H, DK = q.shape
    DV = v.shape[-1]
    C = HG_CHUNK
    n = S // C

    def chunks(t):
        return t.astype(F32).reshape(B, n, C, H, t.shape[-1]).transpose(1, 0, 3, 2, 4)

    qc, kc, vc = chunks(q), chunks(k), chunks(v)
    bc = jnp.cumsum(chunks(log_f), axis=3)
    causal = jnp.tril(jnp.ones((C, C), dtype=bool))[:, :, None]

    def step(state, inp):
        q_, k_, v_, b_ = inp
        diff = b_[:, :, :, None, :] - b_[:, :, None, :, :]
        dec = jnp.where(causal, jnp.exp(jnp.minimum(diff, 0.0)), 0.0)
        att = jnp.einsum('bhtk,bhsk,bhtsk->bhts', q_, k_, dec)
        o = jnp.einsum('bhts,bhsv->bhtv', att, v_)
        o = o + jnp.einsum('bhtk,bhkv->bhtv', q_ * jnp.exp(b_), state)
        b_last = b_[:, :, -1:, :]
        k_dec = k_ * jnp.exp(b_last - b_)
        state = jnp.exp(b_last[:, :, 0, :])[..., None] * state + jnp.einsum('bhsk,bhsv->bhkv', k_dec, v_)
        return state, o

    state0 = jnp.zeros((B, H, DK, DV), F32)
    _, oc = lax.scan(step, state0, (qc, kc, vc, bc))
    return oc.transpose(1, 0, 3, 2, 4).reshape(B, S, H, DV)


def _hgrn2_mixer(h, w_in, o_gain, w_out, lower_bound):
    B, S, _ = h.shape
    fd = HG_HEADS * HG_DK
    vd = HG_HEADS * HG_DV
    q, f, i, g = jnp.split(h @ w_in, [fd, 2 * fd, 2 * fd + vd], axis=-1)
    q = jax.nn.silu(q)
    f32 = f.astype(F32)
    lb = lower_bound.astype(F32)
    forget = lb + (1.0 - lb) * jax.nn.sigmoid(f32)
    log_f = jnp.log(jnp.maximum(forget, TINY))
    k = (1.0 - lb) * jax.nn.sigmoid(-f32)
    o = _gla_chunk_scan(q.reshape(B, S, HG_HEADS, HG_DK), k.reshape(B, S, HG_HEADS, HG_DK),
                        i.reshape(B, S, HG_HEADS, HG_DV), log_f.reshape(B, S, HG_HEADS, HG_DK))
    o = _rmsnorm(o, o_gain.reshape(HG_HEADS, HG_DV)).astype(h.dtype).reshape(B, S, vd)
    return (o * jax.nn.silu(g)) @ w_out


def _dilated_window_attention(q, k, v, window, dilation):
    B, S, H, Dh = q.shape
    span = window // dilation
    L = -(-S // dilation)
    Lp = -(-L // AT_BLOCK) * AT_BLOCK
    Sp = Lp * dilation
    nb = Lp // AT_BLOCK

    def strided_blocks(t):
        t = jnp.pad(t, ((0, 0), (0, Sp - S), (0, 0), (0, 0)))
        return t.reshape(B, nb, AT_BLOCK, dilation, H, Dh).transpose(0, 3, 4, 1, 2, 5)

    def with_prev(t):
        prev = jnp.pad(t[:, :, :, :-1], ((0, 0), (0, 0), (0, 0), (1, 0), (0, 0), (0, 0)))
        return jnp.concatenate([prev, t], axis=4)

    qb = strided_blocks(q)
    kk = with_prev(strided_blocks(k))
    vv = with_prev(strided_blocks(v))
    s = jnp.einsum('brhnqe,brhnke->brhnqk', qb, kk, preferred_element_type=F32) * (Dh ** -0.5)
    qi = jnp.arange(AT_BLOCK)[:, None] + AT_BLOCK
    ki = jnp.arange(2 * AT_BLOCK)[None, :]
    dist = qi - ki
    band = (dist >= 0) & (dist <= span)
    first = (jnp.arange(nb) == 0)[:, None, None]
    mask = band[None] & ~(first & (ki < AT_BLOCK)[None])
    s = jnp.where(mask, s, MASK_VALUE)
    m = jnp.max(s, axis=-1, keepdims=True)
    p = jnp.exp(s - m)
    l = jnp.sum(p, axis=-1, keepdims=True)
    o = jnp.einsum('brhnqk,brhnke->brhnqe', p, vv.astype(F32)) / l
    lse = (m + jnp.log(l))[..., 0]
    o = o.transpose(0, 3, 4, 1, 2, 5).reshape(B, Sp, H, Dh)[:, :S]
    lse = lse.transpose(0, 3, 4, 1, 2).reshape(B, Sp, H)[:, :S]
    return o, lse


def _dilated_attention_mixer(h, positions, w_in, q_gain, k_gain, w_out):
    B, S, _ = h.shape
    proj = (h @ w_in).reshape(B, S, 3, AT_NGROUP, AT_HEADS, AT_HEAD_DIM)
    outs = []
    lses = []
    for gi, (window, dilation) in enumerate(AT_GROUPS):
        q = _rope(_rmsnorm(proj[:, :, 0, gi], q_gain[gi]), positions)
        k = _rope(_rmsnorm(proj[:, :, 1, gi], k_gain[gi]), positions)
        o, lse = _dilated_window_attention(q, k, proj[:, :, 2, gi], window, dilation)
        outs.append(o)
        lses.append(lse)
    wts = jax.nn.softmax(jnp.stack(lses, axis=0), axis=0)
    o = jnp.sum(wts[..., None] * jnp.stack(outs, axis=0), axis=0)
    return o.astype(h.dtype).reshape(B, S, AT_HEADS * AT_HEAD_DIM) @ w_out


def _retention_chunk_scan(q, k, v):
    B, S, H, DK = q.shape
    DV = v.shape[-1]
    C = RT_CHUNK
    n = S // C
    log_gamma = jnp.log1p(-jnp.exp2(-5.0 - jnp.arange(H, dtype=F32)))
    idx = jnp.arange(C, dtype=F32)
    rel = idx[:, None] - idx[None, :]
    decay = jnp.where(rel >= 0, jnp.exp(log_gamma[:, None, None] * jnp.maximum(rel, 0.0)), 0.0)
    q_dec = jnp.exp(log_gamma[:, None] * (idx + 1.0))
    k_dec = jnp.exp(log_gamma[:, None] * (C - 1.0 - idx))
    chunk_dec = jnp.exp(log_gamma * C)

    def chunks(t):
        return t.astype(F32).reshape(B, n, C, H, t.shape[-1]).transpose(1, 0, 3, 2, 4)

    def step(state, inp):
        q_, k_, v_ = inp
        att = jnp.einsum('bhtk,bhsk->bhts', q_, k_) * decay
        o = jnp.einsum('bhts,bhsv->bhtv', att, v_)
        o = o + jnp.einsum('bhtk,bhkv->bhtv', q_ * q_dec[None, :, :, None], state)
        state = chunk_dec[None, :, None, None] * state + jnp.einsum('bhsk,bhsv->bhkv', k_ * k_dec[None, :, :, None], v_)
        return state, o

    state0 = jnp.zeros((B, H, DK, DV), F32)
    _, oc = lax.scan(step, state0, (chunks(q), chunks(k), chunks(v)))
    return oc.transpose(1, 0, 3, 2, 4).reshape(B, S, H, DV)


def _retention_mixer(h, positions, w_in, o_gain, w_out):
    B, S, _ = h.shape
    kd = RT_HEADS * RT_DK
    vd = RT_HEADS * RT_DV
    q, k, v, g = jnp.split(h @ w_in, [kd, 2 * kd, 2 * kd + vd], axis=-1)
    q = _rope(q.reshape(B, S, RT_HEADS, RT_DK), positions)
    k = _rope(k.reshape(B, S, RT_HEADS, RT_DK), positions) * (RT_DK ** -0.5)
    o = _retention_chunk_scan(q, k, v.reshape(B, S, RT_HEADS, RT_DV))
    o = _rmsnorm(o, o_gain.reshape(RT_HEADS, RT_DV)).astype(h.dtype).reshape(B, S, vd)
    return (jax.nn.silu(g) * o) @ w_out


def _peer_ffn(h, w_q, sub_keys, u, v):
    B, S, D = h.shape
    T = B * S
    ht = h.reshape(T, D)
    qh = (ht @ w_q).reshape(T, PK_HEADS, 2, PK_DQ // 2)
    s = jnp.einsum('thcd,hcnd->thcn', qh, sub_keys, preferred_element_type=F32)
    s_a, i_a = lax.top_k(s[:, :, 0], PK_TOPK)
    s_b, i_b = lax.top_k(s[:, :, 1], PK_TOPK)
    cand_s = (s_a[..., :, None] + s_b[..., None, :]).reshape(T, PK_HEADS, PK_TOPK * PK_TOPK)
    cand_i = (i_a[..., :, None] * PK_NKEYS + i_b[..., None, :]).reshape(T, PK_HEADS, PK_TOPK * PK_TOPK)
    top_s, sel = lax.top_k(cand_s, PK_TOPK)
    idx = jnp.take_along_axis(cand_i, sel, axis=-1)
    gate = jax.nn.softmax(top_s, axis=-1).astype(h.dtype)
    nblk = T // PK_TOKEN_BLOCK

    def expert_block(args):
        hb, ib, gb = args
        act = jax.nn.gelu(jnp.einsum('td,thkd->thk', hb, u[ib]), approximate=False)
        return jnp.einsum('thk,thkd->td', gb * act, v[ib])

    out = lax.map(expert_block, (ht.reshape(nblk, PK_TOKEN_BLOCK, D),
                                 idx.reshape(nblk, PK_TOKEN_BLOCK, PK_HEADS, PK_TOPK),
                                 gate.reshape(nblk, PK_TOKEN_BLOCK, PK_HEADS, PK_TOPK)))
    return out.reshape(B, S, D)


def setup_inputs(seed: int = 0) -> dict:
    key = jax.random.key(seed)
    ks = jax.random.split(key, 20)
    D = D_MODEL

    def nrm(k, shape, scale):
        return jax.random.normal(k, shape, F32) * scale

    hg_in = 2 * HG_HEADS * HG_DK + 2 * HG_HEADS * HG_DV
    at_in = 3 * AT_NGROUP * AT_HEADS * AT_HEAD_DIM
    rt_in = 2 * RT_HEADS * RT_DK + 2 * RT_HEADS * RT_DV
    hg_vd = HG_HEADS * HG_DV
    at_od = AT_HEADS * AT_HEAD_DIM
    rt_vd = RT_HEADS * RT_DV
    x = nrm(ks[0], (BATCH, SEQ, D), 1.0)
    positions = jnp.arange(SEQ, dtype=jnp.int32)[None, :] + jax.random.randint(ks[1], (BATCH, 1), 0, 1024, dtype=jnp.int32)
    return {
        'x': x,
        'positions': positions,
        'mix_norm': 1.0 + nrm(ks[2], (DEPTH, D), 0.02),
        'ffn_norm': 1.0 + nrm(ks[3], (DEPTH, D), 0.02),
        'hg_lb': nrm(ks[4], (DEPTH, HG_HEADS * HG_DK), 0.5),
        'hg_w_in': nrm(ks[5], (N_LAYERS_A, D, hg_in), D ** -0.5),
        'hg_onorm': 1.0 + nrm(ks[6], (N_LAYERS_A, hg_vd), 0.02),
        'hg_w_out': nrm(ks[7], (N_LAYERS_A, hg_vd, D), 0.5 * hg_vd ** -0.5),
        'at_w_in': nrm(ks[8], (N_LAYERS_B, D, at_in), D ** -0.5),
        'at_qnorm': 1.0 + nrm(ks[9], (N_LAYERS_B, AT_NGROUP, AT_HEAD_DIM), 0.02),
        'at_knorm': 1.0 + nrm(ks[10], (N_LAYERS_B, AT_NGROUP, AT_HEAD_DIM), 0.02),
        'at_w_out': nrm(ks[11], (N_LAYERS_B, at_od, D), 0.5 * at_od ** -0.5),
        'rt_w_in': nrm(ks[12], (N_LAYERS_C, D, rt_in), D ** -0.5),
        'rt_onorm': 1.0 + nrm(ks[13], (N_LAYERS_C, rt_vd), 0.02),
        'rt_w_out': nrm(ks[14], (N_LAYERS_C, rt_vd, D), 0.5 * rt_vd ** -0.5),
        'pk_w_q': nrm(ks[15], (DEPTH, D, PK_HEADS * PK_DQ), D ** -0.5),
        'pk_keys': nrm(ks[16], (DEPTH, PK_HEADS, 2, PK_NKEYS, PK_DQ // 2), (PK_DQ // 2) ** -0.5),
        'pk_u': nrm(ks[17], (DEPTH, PK_EXPERTS, D), D ** -0.5),
        'pk_v': nrm(ks[18], (DEPTH, PK_EXPERTS, D), 0.5 * PK_HEADS ** -0.5),
    }


def reference(x, positions, mix_norm, ffn_norm, hg_lb, hg_w_in, hg_onorm, hg_w_out,
              at_w_in, at_qnorm, at_knorm, at_w_out, rt_w_in, rt_onorm, rt_w_out,
              pk_w_q, pk_keys, pk_u, pk_v):
    lb_soft = jax.nn.softmax(hg_lb.astype(F32), axis=0)
    lb_all = jnp.cumsum(lb_soft, axis=0) - lb_soft[0:1]
    ia = 0
    ib = 0
    ic = 0
    for layer in range(DEPTH):
        h = _rmsnorm(x, mix_norm[layer])
        kind = layer % N_MIXERS
        if kind == 0:
            y = _hgrn2_mixer(h, hg_w_in[ia], hg_onorm[ia], hg_w_out[ia], lb_all[layer])
            ia += 1
        elif kind == 1:
            y = _dilated_attention_mixer(h, positions, at_w_in[ib], at_qnorm[ib], at_knorm[ib], at_w_out[ib])
            ib += 1
        else:
            y = _retention_mixer(h, positions, rt_w_in[ic], rt_onorm[ic], rt_w_out[ic])
            ic += 1
        x = x + y.astype(x.dtype)
        h = _rmsnorm(x, ffn_norm[layer])
        x = x + _peer_ffn(h, pk_w_q[layer], pk_keys[layer], pk_u[layer], pk_v[layer]).astype(x.dtype)
    return x
```

```python
import functools
import math

import jax
import jax.numpy as jnp
from jax import lax
from jax.experimental import pallas as pl
from jax.experimental.pallas import tpu as pltpu

F32 = jnp.float32
BF16 = jnp.bfloat16
_MXU_DTYPE = BF16
_EPS = 1e-6
_LANES = 128
_SUBLANES = 8
_VMEM_LIMIT = 56 * 1024 * 1024

_NT = (((1,), (1,)), ((), ()))


def _params(sem):
    return pltpu.CompilerParams(dimension_semantics=sem, vmem_limit_bytes=_VMEM_LIMIT)


def _rms(x, gain):
    return x * lax.rsqrt(jnp.mean(x * x, axis=-1, keepdims=True) + _EPS) * gain


def _merge_desc(c):
    c = list(c)
    n = len(c)
    j = n // 2
    while j >= 1:
        for i in range(n):
            l = i ^ j
            if l > i:
                hi = jnp.maximum(c[i], c[l])
                lo = jnp.minimum(c[i], c[l])
                c[i], c[l] = hi, lo
        j //= 2
    return c


def _sort_desc(xs):
    xs = list(xs)
    n = len(xs)
    k = 2
    while k <= n:
        j = k // 2
        while j >= 1:
            for i in range(n):
                l = i ^ j
                if l > i:
                    hi = jnp.maximum(xs[i], xs[l])
                    lo = jnp.minimum(xs[i], xs[l])
                    if (i & k) == 0:
                        xs[i], xs[l] = hi, lo
                    else:
                        xs[i], xs[l] = lo, hi
            j //= 2
        k *= 2
    return xs


def _top_of_two(a, b):
    n = len(a)
    return _merge_desc([jnp.maximum(a[i], b[n - 1 - i]) for i in range(n)])


_PK_TOPK = 16
_PK_PAIRS = [(i, j) for i in range(_PK_TOPK) for j in range(_PK_TOPK) if (i + 1) * (j + 1) <= _PK_TOPK]


def _top16_products(a_top, b_top):
    cand = [a_top[i] * b_top[j] for (i, j) in _PK_PAIRS]
    zero = jnp.zeros_like(cand[0])
    cand = cand + [zero] * (64 - len(cand))
    groups = [_sort_desc(cand[16 * g:16 * (g + 1)]) for g in range(4)]
    return _top_of_two(_top_of_two(groups[0], groups[1]), _top_of_two(groups[2], groups[3]))


def _peer_route_kernel(x_ref, g_ref, wq_ref, keys_ref, h_ref, ea_ref, eb_ref, th_ref, s_scr, top_scr,
                       *, heads, nkeys):
    tb = x_ref.shape[0]
    hb = _rms(x_ref[...], g_ref[...]).astype(_MXU_DTYPE)
    h_ref[...] = hb
    q = jnp.dot(hb, wq_ref[...], preferred_element_type=F32)
    for hc in range(2 * heads):
        qs = q[:, hc * nkeys:(hc + 1) * nkeys].astype(_MXU_DTYPE)
        s_scr[hc] = lax.dot_general(keys_ref[hc], qs, _NT, preferred_element_type=F32)

    nrow = nkeys // _SUBLANES

    def group(gi, carry):
        lanes = pl.ds(pl.multiple_of(gi * _LANES, _LANES), _LANES)
        for hc in range(2 * heads):
            hd, c = divmod(hc, 2)
            rows = [s_scr[hc, pl.ds(_SUBLANES * m, _SUBLANES), lanes] for m in range(nrow)]
            srt = _sort_desc(rows)
            for shift in (4, 2, 1):
                srt = _top_of_two(srt, [pltpu.roll(r, shift, 0) for r in srt])
            smax = srt[0]
            dst = eb_ref if c else ea_ref
            for m in range(nrow):
                dst[hd, pl.ds(_SUBLANES * m, _SUBLANES), lanes] = jnp.exp(rows[m] - smax)
            for i in range(_PK_TOPK):
                top_scr[c, i, pl.ds(hd, 1), :] = srt[i][0:1, :]
        a_s = [top_scr[0, i] for i in range(_PK_TOPK)]
        b_s = [top_scr[1, i] for i in range(_PK_TOPK)]
        a_top = [jnp.exp(a - a_s[0]) for a in a_s]
        b_top = [jnp.exp(b - b_s[0]) for b in b_s]
        top = _top16_products(a_top, b_top)
        z = top[0]
        for t in top[1:]:
            z = z + t
        rz = 1.0 / z
        a_top = [a * rz for a in a_top]
        th_ref[:, lanes] = _top16_products(a_top, b_top)[_PK_TOPK - 1]
        for hd in range(heads):
            ea_ref[hd, :, lanes] = ea_ref[hd, :, lanes] * rz[hd:hd + 1, :]
        return carry

    lax.fori_loop(0, tb // _LANES, group, 0)


def _peer_route(x2d, gain, wq, keys, *, tb):
    t, d = x2d.shape
    hc, nkeys, dk = keys.shape
    heads = hc // 2
    kern = functools.partial(_peer_route_kernel, heads=heads, nkeys=nkeys)
    return pl.pallas_call(
        kern,
        grid=(t // tb,),
        in_specs=[
            pl.BlockSpec((tb, d), lambda i: (i, 0)),
            pl.BlockSpec((1, d), lambda i: (0, 0)),
            pl.BlockSpec(wq.shape, lambda i: (0, 0)),
            pl.BlockSpec(keys.shape, lambda i: (0, 0, 0)),
        ],
        out_specs=[
            pl.BlockSpec((tb, d), lambda i: (i, 0)),
            pl.BlockSpec((heads, nkeys, tb), lambda i: (0, 0, i)),
            pl.BlockSpec((heads, nkeys, tb), lambda i: (0, 0, i)),
            pl.BlockSpec((heads, tb), lambda i: (0, i)),
        ],
        out_shape=[
            jax.ShapeDtypeStruct((t, d), _MXU_DTYPE),
            jax.ShapeDtypeStruct((heads, nkeys, t), F32),
            jax.ShapeDtypeStruct((heads, nkeys, t), F32),
            jax.ShapeDtypeStruct((heads, t), F32),
        ],
        scratch_shapes=[
            pltpu.VMEM((hc, nkeys, tb), F32),
            pltpu.VMEM((2, _PK_TOPK, heads, _LANES), F32),
        ],
        compiler_params=_params(("parallel",)),
        name="peer_route",
    )(x2d, gain, wq, keys)


def _gelu(x):
    return 0.5 * x * (1.0 + lax.erf(x * (1.0 / math.sqrt(2.0))))


def _peer_dense_kernel(h_ref, u_ref, vt_ref, ea_ref, eb_ref, th_ref, x_ref, o_ref, acc_ref, hu_ref, w_ref,
                       *, heads, nkeys):
    e = pl.program_id(1)
    tb = h_ref.shape[0]
    a_t = ea_ref.shape[1]

    @pl.when(e == 0)
    def _():
        acc_ref[...] = jnp.zeros_like(acc_ref)

    hu_ref[...] = lax.dot_general(u_ref[...], h_ref[...], _NT, preferred_element_type=F32)

    def body(tc, carry):
        lanes = pl.ds(pl.multiple_of(tc * _LANES, _LANES), _LANES)
        ths = [th_ref[pl.ds(hd, 1), lanes] for hd in range(heads)]
        for a in range(a_t):
            w = None
            for hd in range(heads):
                p = ea_ref[hd, pl.ds(a, 1), lanes] * eb_ref[hd, :, lanes]
                sel = jnp.where(p >= ths[hd], p, 0.0)
                w = sel if w is None else w + sel
            rows = pl.ds(a * nkeys, nkeys)
            w_ref[rows, lanes] = (w * _gelu(hu_ref[rows, lanes])).astype(w_ref.dtype)
        return carry

    lax.fori_loop(0, tb // _LANES, body, 0)
    acc_ref[...] += jnp.dot(vt_ref[...], w_ref[...], preferred_element_type=F32)

    @pl.when(e == pl.num_programs(1) - 1)
    def _():
        o_ref[...] = x_ref[...] + acc_ref[...].T


def _peer_dense(h, u, vt, ea, eb, th, x2d, *, tb, a_t):
    t, d = x2d.shape
    heads, nkeys, _ = ea.shape
    e_t = a_t * nkeys
    n_e = u.shape[0] // e_t
    kern = functools.partial(_peer_dense_kernel, heads=heads, nkeys=nkeys)
    return pl.pallas_call(
        kern,
        grid=(t // tb, n_e),
        in_specs=[
            pl.BlockSpec((tb, d), lambda i, e: (i, 0)),
            pl.BlockSpec((e_t, d), lambda i, e: (e, 0)),
            pl.BlockSpec((d, e_t), lambda i, e: (0, e)),
            pl.BlockSpec((heads, a_t, tb), lambda i, e: (0, e, i)),
            pl.BlockSpec((heads, nkeys, tb), lambda i, e: (0, 0, i)),
            pl.BlockSpec((heads, tb), lambda i, e: (0, i)),
            pl.BlockSpec((tb, d), lambda i, e: (i, 0)),
        ],
        out_specs=pl.BlockSpec((tb, d), lambda i, e: (i, 0)),
        out_shape=jax.ShapeDtypeStruct((t, d), F32),
        scratch_shapes=[
            pltpu.VMEM((d, tb), F32),
            pltpu.VMEM((e_t, tb), F32),
            pltpu.VMEM((e_t, tb), _MXU_DTYPE),
        ],
        compiler_params=_params(("parallel", "arbitrary")),
        name="peer_dense",
    )(h, u, vt, ea, eb, th, x2d)


def _peer_layer(x2d, gain, wq, keys, u, vt, *, tb_route, tb_dense, a_t):
    h, ea, eb, th = _peer_route(x2d, gain, wq, keys, tb=tb_route)
    return _peer_dense(h, u, vt, ea, eb, th, x2d, tb=tb_dense, a_t=a_t)


def _norm_proj_kernel(x_ref, g_ref, w_ref, o_ref, hb_ref):
    @pl.when(pl.program_id(1) == 0)
    def _():
        hb_ref[...] = _rms(x_ref[...], g_ref[...]).astype(hb_ref.dtype)

    o_ref[...] = jnp.dot(hb_ref[...], w_ref[...], preferred_element_type=F32)


def _norm_proj(x2d, gain, w, *, tm, tn):
    t, d = x2d.shape
    n = w.shape[1]
    return pl.pallas_call(
        _norm_proj_kernel,
        grid=(t // tm, n // tn),
        in_specs=[
            pl.BlockSpec((tm, d), lambda i, j: (i, 0)),
            pl.BlockSpec((1, d), lambda i, j: (0, 0)),
            pl.BlockSpec((d, tn), lambda i, j: (0, j)),
        ],
        out_specs=pl.BlockSpec((tm, tn), lambda i, j: (i, j)),
        out_shape=jax.ShapeDtypeStruct((t, n), F32),
        scratch_shapes=[pltpu.VMEM((tm, d), _MXU_DTYPE)],
        compiler_params=_params(("parallel", "arbitrary")),
        name="norm_proj",
    )(x2d, gain, w)


def _gated_out_kernel(o_ref, g_ref, gain_ref, w_ref, x_ref, y_ref, a_ref, *, dv):
    vd = o_ref.shape[1]
    for hd in range(vd // dv):
        cols = slice(hd * dv, (hd + 1) * dv)
        o = o_ref[:, cols]
        g = g_ref[:, cols]
        a_ref[:, cols] = (_rms(o, gain_ref[:, cols]) * (g * jax.nn.sigmoid(g))).astype(a_ref.dtype)
    y_ref[...] = x_ref[...] + jnp.dot(a_ref[...], w_ref[...], preferred_element_type=F32)


def _gated_out(o, p, g_block, gain, w, x2d, *, dv, tm):
    t, vd = o.shape
    d = x2d.shape[1]
    return pl.pallas_call(
        functools.partial(_gated_out_kernel, dv=dv),
        grid=(t // tm,),
        in_specs=[
            pl.BlockSpec((tm, vd), lambda i: (i, 0)),
            pl.BlockSpec((tm, vd), lambda i: (i, g_block)),
            pl.BlockSpec((1, vd), lambda i: (0, 0)),
            pl.BlockSpec((vd, d), lambda i: (0, 0)),
            pl.BlockSpec((tm, d), lambda i: (i, 0)),
        ],
        out_specs=pl.BlockSpec((tm, d), lambda i: (i, 0)),
        out_shape=jax.ShapeDtypeStruct((t, d), F32),
        scratch_shapes=[pltpu.VMEM((tm, vd), _MXU_DTYPE)],
        compiler_params=_params(("parallel",)),
        name="gated_out",
    )(o, p, gain, w, x2d)


_ROPE_THETA = 10000.0


def _rope_kernel(pos_ref, cr_ref, sr_ref, ca_ref, s1_ref, s2_ref, *, rt_half, at_half):
    pos = pos_ref[...]
    lane = lax.broadcasted_iota(jnp.int32, pos.shape, 1)
    inv = jnp.exp(lane.astype(F32) * (-math.log(_ROPE_THETA) / rt_half))
    ang = pos * inv
    cr_ref[...] = jnp.cos(ang)
    sr_ref[...] = jnp.sin(ang)
    j = lane % (2 * at_half)
    inv = jnp.exp((j % at_half).astype(F32) * (-math.log(_ROPE_THETA) / at_half))
    ang = pos * inv
    sn = jnp.sin(ang)
    ca_ref[...] = jnp.cos(ang)
    s1_ref[...] = jnp.where(j < at_half, -sn, 0.0)
    s2_ref[...] = jnp.where(j < at_half, 0.0, sn)


def _rope_tables(pos_b, *, rt_half, at_half, tm):
    t = pos_b.shape[0]
    spec = pl.BlockSpec((tm, _LANES), lambda i: (i, 0))
    return pl.pallas_call(
        functools.partial(_rope_kernel, rt_half=rt_half, at_half=at_half),
        grid=(t // tm,),
        in_specs=[spec],
        out_specs=[spec] * 5,
        out_shape=[jax.ShapeDtypeStruct((t, _LANES), F32)] * 5,
        compiler_params=_params(("parallel",)),
        name="rope_tables",
    )(pos_b)


_HG_CHUNK = 64
_HG_SUB = 16
_TINY = 1e-30
_TN = (((0,), (0,)), ((), ()))


def _split_dot(a, b_f32):
    hi = b_f32.astype(BF16)
    lo = (b_f32 - hi.astype(F32)).astype(BF16)
    return (jnp.dot(a, hi, preferred_element_type=F32) + jnp.dot(a, lo, preferred_element_type=F32))


def _hgrn_kernel(q_ref, f_ref, i_ref, lb_ref, o_ref, st_ref, *, layer):
    c, sc = _HG_CHUNK, _HG_SUB
    nchunk = q_ref.shape[0] // c

    @pl.when(pl.program_id(2) == 0)
    def _():
        st_ref[...] = jnp.zeros_like(st_ref)

    lbs = lb_ref[...]
    e = jnp.exp(lbs - jnp.max(lbs, axis=0, keepdims=True))
    soft = e / jnp.sum(e, axis=0, keepdims=True)
    lb = jnp.zeros_like(soft[0:1])
    for l in range(1, layer + 1):
        lb = lb + soft[l:l + 1]

    r_i = lax.broadcasted_iota(jnp.int32, (c, c), 0)
    c_i = lax.broadcasted_iota(jnp.int32, (c, c), 1)
    tri = (r_i >= c_i).astype(BF16)
    sub_t = lax.broadcasted_iota(jnp.int32, (sc, 1), 0)

    def chunk(ci, carry):
        rows = pl.ds(pl.multiple_of(ci * c, c), c)
        qr = q_ref[rows, :]
        fr = f_ref[rows, :]
        v = i_ref[rows, :]
        q = qr * jax.nn.sigmoid(qr)
        forget = lb + (1.0 - lb) * jax.nn.sigmoid(fr)
        logf = jnp.log(jnp.maximum(forget, _TINY))
        k = (1.0 - lb) * jax.nn.sigmoid(-fr)
        if _MXU_DTYPE == BF16:
            b = _split_dot(tri, logf)
        else:
            b = jnp.dot(tri.astype(F32), logf, preferred_element_type=F32)
        st = st_ref[...]
        o = lax.dot_general((q * jnp.exp(b)).astype(_MXU_DTYPE), st.astype(_MXU_DTYPE), _NT,
                            preferred_element_type=F32)
        vb = v.astype(_MXU_DTYPE)
        outs = []
        for i in range(c // sc):
            lo, hi = i * sc, (i + 1) * sc
            qi, bi = q[lo:hi], b[lo:hi]
            oi = o[lo:hi]
            if i > 0:
                ref_b = b[lo - 1:lo]
                qt = (qi * jnp.exp(bi - ref_b)).astype(_MXU_DTYPE)
                kt = (k[:lo] * jnp.exp(ref_b - b[:lo])).astype(_MXU_DTYPE)
                att = lax.dot_general(qt, kt, _NT, preferred_element_type=F32)
                oi = oi + jnp.dot(att.astype(_MXU_DTYPE), vb[:lo], preferred_element_type=F32)
            for s in range(sc):
                dec = jnp.exp(jnp.minimum(bi - bi[s:s + 1], 0.0))
                col = jnp.sum(qi * k[lo + s:lo + s + 1] * dec, axis=1, keepdims=True)
                col = jnp.where(sub_t >= s, col, 0.0)
                oi = oi + col * v[lo + s:lo + s + 1]
            outs.append(oi)
        o_ref[rows, :] = jnp.concatenate(outs, axis=0)
        b_last = b[c - 1:c]
        kd = (k * jnp.exp(b_last - b)).astype(_MXU_DTYPE)
        st_ref[...] = jnp.exp(b_last) * st + lax.dot_general(vb, kd, _TN, preferred_element_type=F32)
        return carry

    lax.fori_loop(0, nchunk, chunk, 0)


def _hgrn_scan(p, hg_lb, *, batch, seq, heads, dk, layer, blk):
    t = p.shape[0]
    nblk = seq // blk
    row = lambda b, h, s: b * nblk + s
    return pl.pallas_call(
        functools.partial(_hgrn_kernel, layer=layer),
        grid=(batch, heads, nblk),
        in_specs=[
            pl.BlockSpec((blk, dk), lambda b, h, s: (row(b, h, s), h)),
            pl.BlockSpec((blk, dk), lambda b, h, s: (row(b, h, s), heads + h)),
            pl.BlockSpec((blk, dk), lambda b, h, s: (row(b, h, s), 2 * heads + h)),
            pl.BlockSpec((hg_lb.shape[0], dk), lambda b, h, s: (0, h)),
        ],
        out_specs=pl.BlockSpec((blk, dk), lambda b, h, s: (row(b, h, s), h)),
        out_shape=jax.ShapeDtypeStruct((t, heads * dk), F32),
        scratch_shapes=[pltpu.VMEM((dk, dk), F32)],
        compiler_params=_params(("parallel", "parallel", "arbitrary")),
        name="hgrn_scan",
    )(p, p, p, hg_lb)


_RT_CHUNK = 128


def _retention_kernel(q_ref, k_ref, v_ref, cos_ref, sin_ref, o_ref, st_ref):
    c = _RT_CHUNK
    nchunk = q_ref.shape[0] // c
    dk = q_ref.shape[1]
    half = dk // 2

    @pl.when(pl.program_id(2) == 0)
    def _():
        st_ref[...] = jnp.zeros_like(st_ref)

    hv = jnp.full((1, _LANES), pl.program_id(1), jnp.int32).astype(F32)
    lg = jnp.log1p(-jnp.exp2(-5.0 - hv))
    t_i = lax.broadcasted_iota(jnp.int32, (c, c), 0)
    s_i = lax.broadcasted_iota(jnp.int32, (c, c), 1)
    rel = (t_i - s_i).astype(F32)
    decay = jnp.where(rel >= 0, jnp.exp(lg * jnp.maximum(rel, 0.0)), 0.0)
    idx = lax.broadcasted_iota(jnp.int32, (c, _LANES), 0).astype(F32)
    q_dec = jnp.exp(lg * (idx + 1.0))
    k_dec = jnp.exp(lg * (c - 1.0 - idx))
    chunk_dec = jnp.exp(lg * c)[:, :1]

    def rope(x, cos, sin):
        x1, x2 = x[:, :half], x[:, half:]
        return x1 * cos - x2 * sin, x2 * cos + x1 * sin

    def chunk(ci, carry):
        rows = pl.ds(pl.multiple_of(ci * c, c), c)
        cos, sin = cos_ref[rows, :], sin_ref[rows, :]
        q1, q2 = rope(q_ref[rows, :], cos, sin)
        k1, k2 = rope(k_ref[rows, :], cos, sin)
        scale = dk ** -0.5
        k1, k2 = k1 * scale, k2 * scale
        cat = lambda a, b: jnp.concatenate([a, b], axis=1).astype(_MXU_DTYPE)
        qb, kb = cat(q1, q2), cat(k1, k2)
        vb = v_ref[rows, :].astype(_MXU_DTYPE)
        att = lax.dot_general(qb, kb, _NT, preferred_element_type=F32) * decay
        o = jnp.dot(att.astype(_MXU_DTYPE), vb, preferred_element_type=F32)
        st = st_ref[...]
        o = o + jnp.dot(cat(q1 * q_dec, q2 * q_dec), st.astype(_MXU_DTYPE), preferred_element_type=F32)
        o_ref[rows, :] = o
        st_ref[...] = chunk_dec * st + lax.dot_general(cat(k1 * k_dec, k2 * k_dec), vb, _TN,
                                                       preferred_element_type=F32)
        return carry

    lax.fori_loop(0, nchunk, chunk, 0)


def _retention_scan(p, cos, sin, *, batch, seq, heads, dk, dv, blk):
    t = p.shape[0]
    nblk = seq // blk
    row = lambda b, h, s: b * nblk + s
    v_off = 2 * heads * dk // dv
    return pl.pallas_call(
        _retention_kernel,
        grid=(batch, heads, nblk),
        in_specs=[
            pl.BlockSpec((blk, dk), lambda b, h, s: (row(b, h, s), h)),
            pl.BlockSpec((blk, dk), lambda b, h, s: (row(b, h, s), heads + h)),
            pl.BlockSpec((blk, dv), lambda b, h, s: (row(b, h, s), v_off + h)),
            pl.BlockSpec((blk, _LANES), lambda b, h, s: (row(b, h, s), 0)),
            pl.BlockSpec((blk, _LANES), lambda b, h, s: (row(b, h, s), 0)),
        ],
        out_specs=pl.BlockSpec((blk, dv), lambda b, h, s: (row(b, h, s), h)),
        out_shape=jax.ShapeDtypeStruct((t, heads * dv), F32),
        scratch_shapes=[pltpu.VMEM((dk, dv), F32)],
        compiler_params=_params(("parallel", "parallel", "arbitrary")),
        name="retention_scan",
    )(p, p, p, cos, sin)


_AT_BLOCK = 128
_MASK_VALUE = -1e30


def _attn_prep_kernel(p_ref, gain_ref, cos_ref, s1_ref, s2_ref, o_ref, *, head_dim):
    x = p_ref[...]
    width = x.shape[1]
    r_i = lax.broadcasted_iota(jnp.int32, (_LANES, _LANES), 0) // head_dim
    c_i = lax.broadcasted_iota(jnp.int32, (_LANES, _LANES), 1) // head_dim
    seg = jnp.where(r_i == c_i, 1.0 / head_dim, 0.0)
    cos, s1, s2 = cos_ref[...], s1_ref[...], s2_ref[...]
    gain = gain_ref[0]
    for j in range(width // _LANES):
        cols = slice(j * _LANES, (j + 1) * _LANES)
        xj = x[:, cols]
        if _MXU_DTYPE == BF16:
            msq = _split_dot_lhs(xj * xj, seg.astype(BF16))
        else:
            msq = jnp.dot(xj * xj, seg, preferred_element_type=F32)
        y = xj * lax.rsqrt(msq + _EPS) * gain[:, cols]
        half = head_dim // 2
        y = y * cos + pltpu.roll(y, _LANES - half, 1) * s1 + pltpu.roll(y, half, 1) * s2
        o_ref[:, cols] = y.astype(o_ref.dtype)


def _split_dot_lhs(a_f32, b):
    hi = a_f32.astype(BF16)
    lo = (a_f32 - hi.astype(F32)).astype(BF16)
    return (jnp.dot(hi, b, preferred_element_type=F32) + jnp.dot(lo, b, preferred_element_type=F32))


def _attn_prep(p, gains, cos, s1, s2, *, head_dim, tm):
    t = p.shape[0]
    nblk, _, width = gains.shape
    tab = pl.BlockSpec((tm, _LANES), lambda i, j: (i, 0))
    return pl.pallas_call(
        functools.partial(_attn_prep_kernel, head_dim=head_dim),
        grid=(t // tm, nblk),
        in_specs=[
            pl.BlockSpec((tm, width), lambda i, j: (i, j)),
            pl.BlockSpec((1, 1, width), lambda i, j: (j, 0, 0)),
            tab, tab, tab,
        ],
        out_specs=pl.BlockSpec((tm, width), lambda i, j: (i, j)),
        out_shape=jax.ShapeDtypeStruct((t, nblk * width), _MXU_DTYPE),
        compiler_params=_params(("parallel", "parallel")),
        name="attn_prep",
    )(p, gains, cos, s1, s2)


def _attn_kernel(q_ref, kp_ref, k_ref, vp_ref, v_ref, o_ref, l_ref, *, head_dim, span):
    blk = _AT_BLOCK
    nsub = q_ref.shape[0] // blk
    lane = lax.broadcasted_iota(jnp.int32, (1, _LANES), 1)
    qi = lax.broadcasted_iota(jnp.int32, (blk, 2 * blk), 0) + blk
    ki = lax.broadcasted_iota(jnp.int32, (blk, 2 * blk), 1)
    band = (ki <= qi) & (ki >= qi - span)
    first_lo = jnp.where(pl.program_id(3) == 0, blk, 0)
    band_first = (ki <= qi) & (ki >= jnp.maximum(qi - span, first_lo))
    scale = head_dim ** -0.5
    for j in range(nsub):
        rows = slice(j * blk, (j + 1) * blk)
        q = q_ref[rows, :]
        if j == 0:
            kk = jnp.concatenate([kp_ref[...], k_ref[rows, :]], axis=0)
            vv = jnp.concatenate([vp_ref[...], v_ref[rows, :]], axis=0)
            mask = band_first
        else:
            kk = k_ref[(j - 1) * blk:(j + 1) * blk, :]
            vv = v_ref[(j - 1) * blk:(j + 1) * blk, :]
            mask = band
        vv = vv.astype(_MXU_DTYPE)
        o_pair = None
        l_pair = None
        for hd in range(_LANES // head_dim):
            in_head = (lane // head_dim) == hd
            qh = jnp.where(in_head, q, jnp.zeros_like(q))
            s = lax.dot_general(qh, kk, _NT, preferred_element_type=F32) * scale
            s = jnp.where(mask, s, _MASK_VALUE)
            m = jnp.max(s, axis=-1, keepdims=True)
            pr = jnp.exp(s - m)
            l = jnp.sum(pr, axis=-1, keepdims=True)
            o = jnp.dot(pr.astype(_MXU_DTYPE), vv, preferred_element_type=F32) / l
            lse = m + jnp.log(l)
            if o_pair is None:
                o_pair, l_pair = o, jnp.broadcast_to(lse, o.shape)
            else:
                o_pair = jnp.where(in_head, o, o_pair)
                l_pair = jnp.where(in_head, lse, l_pair)
        o_ref[rows, :] = o_pair
        l_ref[rows, :] = l_pair


def _attn_group(qk, p, *, batch, seq, group, ngroup, heads, head_dim, window, dilation):
    t = p.shape[0]
    d = dilation
    width = heads * head_dim
    assert seq % (d * _AT_BLOCK) == 0 and window // d <= _AT_BLOCK
    ls = seq // d
    nb = ls // _AT_BLOCK
    r_sub = min(4, nb)
    nstep = nb // r_sub
    rows = r_sub * _AT_BLOCK
    qk_w = qk.shape[1] // _LANES
    p_w = p.shape[1] // _LANES
    pairs = width // _LANES
    qk_s = qk.reshape(t // d, d * qk.shape[1])
    p_s = p.reshape(t // d, d * p.shape[1])
    q_col = lambda r, hp: r * qk_w + group * pairs + hp
    k_col = lambda r, hp: r * qk_w + (ngroup + group) * pairs + hp
    v_col = lambda r, hp: r * p_w + (2 * ngroup + group) * pairs + hp
    o_col = lambda r, hp: r * pairs + hp
    main = lambda b, n: b * nstep + n
    prev = lambda b, n: jnp.maximum(b * nb + n * r_sub - 1, 0)
    out_spec = pl.BlockSpec((rows, _LANES), lambda b, r, hp, n: (main(b, n), o_col(r, hp)))
    o, lse = pl.pallas_call(
        functools.partial(_attn_kernel, head_dim=head_dim, span=window // d),
        grid=(batch, d, pairs, nstep),
        in_specs=[
            pl.BlockSpec((rows, _LANES), lambda b, r, hp, n: (main(b, n), q_col(r, hp))),
            pl.BlockSpec((_AT_BLOCK, _LANES), lambda b, r, hp, n: (prev(b, n), k_col(r, hp))),
            pl.BlockSpec((rows, _LANES), lambda b, r, hp, n: (main(b, n), k_col(r, hp))),
            pl.BlockSpec((_AT_BLOCK, _LANES), lambda b, r, hp, n: (prev(b, n), v_col(r, hp))),
            pl.BlockSpec((rows, _LANES), lambda b, r, hp, n: (main(b, n), v_col(r, hp))),
        ],
        out_specs=[out_spec, out_spec],
        out_shape=[jax.ShapeDtypeStruct((t // d, d * width), F32)] * 2,
        compiler_params=_params(("parallel", "parallel", "parallel", "arbitrary")),
        name=f"attn_group{group}",
    )(qk_s, qk_s, qk_s, p_s, p_s)
    return o.reshape(t, width), lse.reshape(t, width)


def _attn_out_kernel(*refs, ngroup):
    o_refs, l_refs = refs[:ngroup], refs[ngroup:2 * ngroup]
    w_ref, x_ref, y_ref = refs[2 * ngroup:]
    ls = [r[...] for r in l_refs]
    m = functools.reduce(jnp.maximum, ls)
    es = [jnp.exp(l - m) for l in ls]
    den = functools.reduce(lambda a, b: a + b, es)
    o = functools.reduce(lambda a, b: a + b, [e * r[...] for e, r in zip(es, o_refs)]) / den
    y_ref[...] = x_ref[...] + jnp.dot(o.astype(_MXU_DTYPE), w_ref[...], preferred_element_type=F32)


def _attn_out(os_, ls_, w, x2d, *, tm):
    t, width = os_[0].shape
    d = x2d.shape[1]
    ng = len(os_)
    blk = pl.BlockSpec((tm, width), lambda i: (i, 0))
    return pl.pallas_call(
        functools.partial(_attn_out_kernel, ngroup=ng),
        grid=(t // tm,),
        in_specs=[blk] * (2 * ng) + [pl.BlockSpec((width, d), lambda i: (0, 0)),
                                     pl.BlockSpec((tm, d), lambda i: (i, 0))],
        out_specs=pl.BlockSpec((tm, d), lambda i: (i, 0)),
        out_shape=jax.ShapeDtypeStruct((t, d), F32),
        compiler_params=_params(("parallel",)),
        name="attn_out",
    )(*os_, *ls_, w, x2d)


_HG_HEADS, _HG_DK = 8, 128
_AT_GROUPS = ((128, 1), (512, 4), (2048, 16))
_AT_HEADS, _AT_HEAD_DIM = 8, 64
_RT_HEADS = 4
_N_MIXERS = 3


def _pick(n, candidates):
    for c in candidates:
        if n % c == 0:
            return c
    return n


def _hgrn_mixer(x2d, gain, w_in, hg_lb, o_gain, w_out, *, batch, seq, layer, blk):
    t = x2d.shape[0]
    p = _norm_proj(x2d, gain, w_in, tm=_pick(t, (512, 256, 128)), tn=_pick(w_in.shape[1], (1024, 512)))
    o = _hgrn_scan(p, hg_lb, batch=batch, seq=seq, heads=_HG_HEADS, dk=_HG_DK, layer=layer, blk=min(blk, seq))
    return _gated_out(o, p, 3, o_gain, w_out, x2d, dv=_HG_DK, tm=_pick(t, (256, 128)))


def _retention_mixer(x2d, gain, w_in, o_gain, w_out, cos, sin, *, batch, seq, blk):
    t, d = x2d.shape
    dk = d // _RT_HEADS
    dv = 2 * dk
    assert dk == 2 * _LANES
    p = _norm_proj(x2d, gain, w_in, tm=_pick(t, (512, 256, 128)), tn=_pick(w_in.shape[1], (1536, 1024, 512)))
    o = _retention_scan(p, cos, sin, batch=batch, seq=seq, heads=_RT_HEADS, dk=dk, dv=dv, blk=min(blk, seq))
    return _gated_out(o, p, 2, o_gain, w_out, x2d, dv=dv, tm=_pick(t, (256, 128)))


def _attn_mixer(x2d, gain, w_in, q_gain, k_gain, w_out, cos, s1, s2, *, batch, seq):
    t = x2d.shape[0]
    ng = len(_AT_GROUPS)
    width = _AT_HEADS * _AT_HEAD_DIM
    p = _norm_proj(x2d, gain, w_in, tm=_pick(t, (512, 256, 128)), tn=_pick(w_in.shape[1], (1536, 512)))
    gains = jnp.concatenate([jnp.tile(q_gain, (1, _AT_HEADS)), jnp.tile(k_gain, (1, _AT_HEADS))], axis=0)
    qk = _attn_prep(p, gains[:, None, :], cos, s1, s2, head_dim=_AT_HEAD_DIM, tm=_pick(t, (512, 256, 128)))
    os_, ls_ = [], []
    for g, (window, dilation) in enumerate(_AT_GROUPS):
        o, lse = _attn_group(qk, p, batch=batch, seq=seq, group=g, ngroup=ng, heads=_AT_HEADS,
                             head_dim=_AT_HEAD_DIM, window=window, dilation=dilation)
        os_.append(o)
        ls_.append(lse)
    return _attn_out(os_, ls_, w_out, x2d, tm=_pick(t, (512, 256, 128)))


def kernel(x, positions, mix_norm, ffn_norm, hg_lb, hg_w_in, hg_onorm, hg_w_out, at_w_in, at_qnorm, at_knorm,
           at_w_out, rt_w_in, rt_onorm, rt_w_out, pk_w_q, pk_keys, pk_u, pk_v):
    b, s, d = x.shape
    depth = mix_norm.shape[0]
    x2d = x.reshape(b * s, d)
    pos_b = jnp.broadcast_to(positions.reshape(b * s, 1).astype(F32), (b * s, _LANES))
    rt_dk = d // _RT_HEADS
    cos_r, sin_r, cos_a, s1_a, s2_a = _rope_tables(pos_b, rt_half=rt_dk // 2, at_half=_AT_HEAD_DIM // 2, tm=512)
    ia = ib = ic = 0
    for layer in range(depth):
        kind = layer % _N_MIXERS
        gain = mix_norm[layer][None, :]
        if kind == 0:
            x2d = _hgrn_mixer(x2d, gain, hg_w_in[ia].astype(_MXU_DTYPE), hg_lb, hg_onorm[ia][None, :],
                              hg_w_out[ia].astype(_MXU_DTYPE), batch=b, seq=s, layer=layer, blk=256)
            ia += 1
        elif kind == 1:
            x2d = _attn_mixer(x2d, gain, at_w_in[ib].astype(_MXU_DTYPE), at_qnorm[ib], at_knorm[ib],
                              at_w_out[ib].astype(_MXU_DTYPE), cos_a, s1_a, s2_a, batch=b, seq=s)
            ib += 1
        else:
            x2d = _retention_mixer(x2d, gain, rt_w_in[ic].astype(_MXU_DTYPE), rt_onorm[ic][None, :],
                                   rt_w_out[ic].astype(_MXU_DTYPE), cos_r, sin_r, batch=b, seq=s, blk=256)
            ic += 1
        keys = pk_keys[layer].reshape(-1, pk_keys.shape[-2], pk_keys.shape[-1]).astype(_MXU_DTYPE)
        x2d = _peer_layer(x2d, ffn_norm[layer][None, :], pk_w_q[layer].astype(_MXU_DTYPE), keys,
                          pk_u[layer].astype(_MXU_DTYPE), pk_v[layer].T.astype(_MXU_DTYPE),
                          tb_route=512, tb_dense=512, a_t=8)
    return x2d.reshape(b, s, d)
```

```python
import functools
import math

import jax
import jax.numpy as jnp
from jax import lax
from jax.experimental import pallas as pl
from jax.experimental.pallas import tpu as pltpu

F32 = jnp.float32
BF16 = jnp.bfloat16
_MXU_DTYPE = BF16
_EPS = 1e-6
_LANES = 128
_SUBLANES = 8
_VMEM_LIMIT = 56 * 1024 * 1024

_NT = (((1,), (1,)), ((), ()))


def _params(sem, flags=None):
    return pltpu.CompilerParams(dimension_semantics=sem, vmem_limit_bytes=_VMEM_LIMIT, flags=flags)


def _rms(x, gain):
    return x * lax.rsqrt(jnp.mean(x * x, axis=-1, keepdims=True) + _EPS) * gain


def _merge_desc(c):
    c = list(c)
    n = len(c)
    j = n // 2
    while j >= 1:
        for i in range(n):
            l = i ^ j
            if l > i:
                hi = jnp.maximum(c[i], c[l])
                lo = jnp.minimum(c[i], c[l])
                c[i], c[l] = hi, lo
        j //= 2
    return c


def _sort_desc(xs):
    xs = list(xs)
    n = len(xs)
    k = 2
    while k <= n:
        j = k // 2
        while j >= 1:
            for i in range(n):
                l = i ^ j
                if l > i:
                    hi = jnp.maximum(xs[i], xs[l])
                    lo = jnp.minimum(xs[i], xs[l])
                    if (i & k) == 0:
                        xs[i], xs[l] = hi, lo
                    else:
                        xs[i], xs[l] = lo, hi
            j //= 2
        k *= 2
    return xs


def _top_of_two(a, b):
    n = len(a)
    return _merge_desc([jnp.maximum(a[i], b[n - 1 - i]) for i in range(n)])


_PK_TOPK = 16
_PK_PAIRS = [(i, j) for i in range(_PK_TOPK) for j in range(_PK_TOPK) if (i + 1) * (j + 1) <= _PK_TOPK]


_NEG = -3.0e38


def _top16_sums(a_top, b_top):
    cand = [a_top[i] + b_top[j] for (i, j) in _PK_PAIRS]
    pad = jnp.full_like(cand[0], _NEG)
    cand = cand + [pad] * (64 - len(cand))
    groups = [_sort_desc(cand[16 * g:16 * (g + 1)]) for g in range(4)]
    return _top_of_two(_top_of_two(groups[0], groups[1]), _top_of_two(groups[2], groups[3]))


def _gate_word(x, dtype):
    if dtype == F32:
        return x
    u = lax.bitcast_convert_type(x.astype(BF16).astype(F32), jnp.uint32)
    return u | (u >> 16)


def _gate_rows(ref, hd, a, lanes, wdt):
    row = jnp.broadcast_to(ref[hd, pl.ds(a, 1), lanes], (_SUBLANES, _LANES))
    return row if wdt == F32 else pltpu.bitcast(row, wdt)


def _peer_route_kernel(x_ref, g_ref, wq_ref, keys_ref, h_ref, ea_ref, ca_ref, eb_ref, rb_ref,
                       s_scr, top_scr, btop_scr, *, heads, nkeys):
    tb = x_ref.shape[0]
    hf = _rms(x_ref[...], g_ref[...])
    hb = hf.astype(_MXU_DTYPE)
    h_ref[...] = hf.T.astype(_MXU_DTYPE)
    q = jnp.dot(hb, wq_ref[...], preferred_element_type=F32)
    for hc in range(2 * heads):
        qs = q[:, hc * nkeys:(hc + 1) * nkeys].astype(_MXU_DTYPE)
        s_scr[hc] = lax.dot_general(keys_ref[hc], qs, _NT, preferred_element_type=F32)

    nrow = nkeys // _SUBLANES
    bcast = lambda row: jnp.broadcast_to(row, (_SUBLANES, _LANES))

    def group(gi, carry):
        lanes = pl.ds(pl.multiple_of(gi * _LANES, _LANES), _LANES)
        for hc in range(2 * heads):
            hd, c = divmod(hc, 2)
            rows = [s_scr[hc, pl.ds(_SUBLANES * m, _SUBLANES), lanes] for m in range(nrow)]
            srt = _sort_desc(rows)
            for shift in (4, 2, 1):
                srt = _top_of_two(srt, [pltpu.roll(r, shift, 0) for r in srt])
            for i in range(_PK_TOPK):
                top_scr[c, i, pl.ds(hd, 1), :] = srt[i][0:1, :]
            if c == 1:
                for i in range(_PK_TOPK):
                    btop_scr[hd, i] = srt[i]
                ebs, rbs = [], []
                for m in range(nrow):
                    ebs.append(jnp.exp(rows[m] - srt[0]))
                    rank = jnp.full_like(rows[m], float(_PK_TOPK))
                    for i in reversed(range(_PK_TOPK)):
                        rank = jnp.where(srt[i] <= rows[m], float(i), rank)
                    rbs.append(rank)
                for m in range(0, nrow, 2):
                    sl = pl.ds(_SUBLANES * m, 2 * _SUBLANES)
                    eb_ref[hd, sl, lanes] = jnp.concatenate(ebs[m:m + 2], axis=0).astype(eb_ref.dtype)
                    rb_ref[hd, sl, lanes] = jnp.concatenate(rbs[m:m + 2], axis=0).astype(rb_ref.dtype)
        a_s = [top_scr[0, i] for i in range(_PK_TOPK)]
        b_s = [top_scr[1, i] for i in range(_PK_TOPK)]
        top = _top16_sums(a_s, b_s)
        z = jnp.exp(top[0] - top[0])
        for t in top[1:]:
            z = z + jnp.exp(t - top[0])
        rz = 1.0 / z
        theta = top[_PK_TOPK - 1]
        for hd in range(heads):
            amax = bcast(a_s[0][hd:hd + 1, :])
            rz_h = bcast(rz[hd:hd + 1, :])
            th_h = bcast(theta[hd:hd + 1, :])
            for m in range(nrow):
                sl = pl.ds(_SUBLANES * m, _SUBLANES)
                sa = s_scr[2 * hd, sl, lanes]
                ea_ref[hd, sl, lanes] = _gate_word(jnp.exp(sa - amax) * rz_h, ea_ref.dtype)
                cnt = jnp.full_like(sa, float(_PK_TOPK))
                for j in reversed(range(_PK_TOPK)):
                    cnt = jnp.where(sa + btop_scr[hd, j] < th_h, float(j), cnt)
                ca_ref[hd, sl, lanes] = _gate_word(cnt, ca_ref.dtype)
        return carry

    lax.fori_loop(0, tb // _LANES, group, 0)


def _peer_route(x2d, gain, wq, keys, *, tb):
    t, d = x2d.shape
    hc, nkeys, dk = keys.shape
    heads = hc // 2
    kern = functools.partial(_peer_route_kernel, heads=heads, nkeys=nkeys)
    tab = pl.BlockSpec((heads, nkeys, tb), lambda i: (0, 0, i))
    word = F32 if _MXU_DTYPE == F32 else jnp.uint32
    return pl.pallas_call(
        kern,
        grid=(t // tb,),
        in_specs=[
            pl.BlockSpec((tb, d), lambda i: (i, 0)),
            pl.BlockSpec((1, d), lambda i: (0, 0)),
            pl.BlockSpec(wq.shape, lambda i: (0, 0)),
            pl.BlockSpec(keys.shape, lambda i: (0, 0, 0)),
        ],
        out_specs=[pl.BlockSpec((d, tb), lambda i: (0, i)), tab, tab, tab, tab],
        out_shape=[
            jax.ShapeDtypeStruct((d, t), _MXU_DTYPE),
            jax.ShapeDtypeStruct((heads, nkeys, t), word),
            jax.ShapeDtypeStruct((heads, nkeys, t), word),
            jax.ShapeDtypeStruct((heads, nkeys, t), _MXU_DTYPE),
            jax.ShapeDtypeStruct((heads, nkeys, t), _MXU_DTYPE),
        ],
        scratch_shapes=[
            pltpu.VMEM((hc, nkeys, tb), F32),
            pltpu.VMEM((2, _PK_TOPK, heads, _LANES), F32),
            pltpu.VMEM((heads, _PK_TOPK, _SUBLANES, _LANES), F32),
        ],
        compiler_params=_params(("parallel",)),
        name="peer_route",
    )(x2d, gain, wq, keys)


def _gelu(x):
    return 0.5 * x * (1.0 + lax.erf(x * (1.0 / math.sqrt(2.0))))


def _peer_dense_kernel(h_ref, u_ref, vt_ref, ea_ref, ca_ref, eb_ref, rb_ref, x_ref, o_ref, acc_ref,
                       *, heads, nkeys):
    e = pl.program_id(1)
    tb = h_ref.shape[1]
    a_t = ea_ref.shape[1]
    wdt = vt_ref.dtype
    pk = _SUBLANES * (4 // jnp.dtype(wdt).itemsize)
    zero = jnp.zeros((pk, _LANES), wdt)
    mc = _pick(tb, (256,))

    @pl.when(e == 0)
    def _():
        acc_ref[...] = jnp.zeros_like(acc_ref)

    for c in range(tb // mc):
        toks = slice(c * mc, (c + 1) * mc)
        mr = 2 * nkeys
        hu = jnp.concatenate(
            [jnp.dot(u_ref[r * mr:(r + 1) * mr, :], h_ref[:, toks], preferred_element_type=F32)
             for r in range(a_t * nkeys // mr)], axis=0)
        cols = [[] for _ in range(mc // _LANES)]
        for a in range(a_t):
            for tc in range(mc // _LANES):
                lanes = slice(c * mc + tc * _LANES, c * mc + (tc + 1) * _LANES)
                pieces = cols[tc]
                accs = [None] * (nkeys // pk)
                for hd in range(heads):
                    ea_b = _gate_rows(ea_ref, hd, a, lanes, wdt)
                    ca_b = _gate_rows(ca_ref, hd, a, lanes, wdt)
                    for bc in range(nkeys // pk):
                        rows = slice(bc * pk, (bc + 1) * pk)
                        sel = jnp.minimum(jnp.maximum(ca_b - rb_ref[hd, rows, lanes], zero),
                                          eb_ref[hd, rows, lanes])
                        term = ea_b * sel
                        accs[bc] = term if accs[bc] is None else accs[bc] + term
                for bc in range(nkeys // pk):
                    r0 = a * nkeys + bc * pk
                    act = _gelu(hu[r0:r0 + pk, tc * _LANES:(tc + 1) * _LANES])
                    pieces.append(accs[bc] * act.astype(wdt))
        w = jnp.concatenate([jnp.concatenate(p, axis=0) for p in cols], axis=1)
        acc_ref[:, toks] += jnp.dot(vt_ref[...], w, preferred_element_type=F32)

    @pl.when(e == pl.num_programs(1) - 1)
    def _():
        o_ref[...] = x_ref[...] + acc_ref[...].T


def _peer_dense(h, u, vt, ea, ca, eb, rb, x2d, *, tb, a_t):
    t, d = x2d.shape
    heads, nkeys, _ = ea.shape
    e_t = a_t * nkeys
    n_e = u.shape[0] // e_t
    kern = functools.partial(_peer_dense_kernel, heads=heads, nkeys=nkeys)
    gate_a = pl.BlockSpec((heads, a_t, tb), lambda i, e: (0, e, i))
    gate_b = pl.BlockSpec((heads, nkeys, tb), lambda i, e: (0, 0, i))
    return pl.pallas_call(
        kern,
        grid=(t // tb, n_e),
        in_specs=[
            pl.BlockSpec((d, tb), lambda i, e: (0, i)),
            pl.BlockSpec((e_t, d), lambda i, e: (e, 0)),
            pl.BlockSpec((d, e_t), lambda i, e: (0, e)),
            gate_a, gate_a, gate_b, gate_b,
            pl.BlockSpec((tb, d), lambda i, e: (i, 0)),
        ],
        out_specs=pl.BlockSpec((tb, d), lambda i, e: (i, 0)),
        out_shape=jax.ShapeDtypeStruct((t, d), F32),
        scratch_shapes=[pltpu.VMEM((d, tb), F32)],
        compiler_params=_params(("parallel", "arbitrary")),
        name="peer_dense",
    )(h, u, vt, ea, ca, eb, rb, x2d)


def _peer_layer(x2d, gain, wq, keys, u, vt, *, tb_route, tb_dense, a_t):
    h, ea, ca, eb, rb = _peer_route(x2d, gain, wq, keys, tb=tb_route)
    return _peer_dense(h, u, vt, ea, ca, eb, rb, x2d, tb=tb_dense, a_t=a_t)


def _norm_proj_kernel(x_ref, g_ref, w_ref, o_ref, hb_ref):
    @pl.when(pl.program_id(1) == 0)
    def _():
        hb_ref[...] = _rms(x_ref[...], g_ref[...]).astype(hb_ref.dtype)

    o_ref[...] = jnp.dot(hb_ref[...], w_ref[...], preferred_element_type=F32)


def _norm_proj(x2d, gain, w, *, tm, tn):
    t, d = x2d.shape
    n = w.shape[1]
    return pl.pallas_call(
        _norm_proj_kernel,
        grid=(t // tm, n // tn),
        in_specs=[
            pl.BlockSpec((tm, d), lambda i, j: (i, 0)),
            pl.BlockSpec((1, d), lambda i, j: (0, 0)),
            pl.BlockSpec((d, tn), lambda i, j: (0, j)),
        ],
        out_specs=pl.BlockSpec((tm, tn), lambda i, j: (i, j)),
        out_shape=jax.ShapeDtypeStruct((t, n), F32),
        scratch_shapes=[pltpu.VMEM((tm, d), _MXU_DTYPE)],
        compiler_params=_params(("parallel", "arbitrary")),
        name="norm_proj",
    )(x2d, gain, w)


def _gated_out_kernel(o_ref, g_ref, gain_ref, w_ref, x_ref, y_ref, a_ref, *, dv):
    vd = o_ref.shape[1]
    for hd in range(vd // dv):
        cols = slice(hd * dv, (hd + 1) * dv)
        o = o_ref[:, cols]
        g = g_ref[:, cols]
        a_ref[:, cols] = (_rms(o, gain_ref[:, cols]) * (g * jax.nn.sigmoid(g))).astype(a_ref.dtype)
    y_ref[...] = x_ref[...] + jnp.dot(a_ref[...], w_ref[...], preferred_element_type=F32)


def _gated_out(o, p, g_block, gain, w, x2d, *, dv, tm):
    t, vd = o.shape
    d = x2d.shape[1]
    return pl.pallas_call(
        functools.partial(_gated_out_kernel, dv=dv),
        grid=(t // tm,),
        in_specs=[
            pl.BlockSpec((tm, vd), lambda i: (i, 0)),
            pl.BlockSpec((tm, vd), lambda i: (i, g_block)),
            pl.BlockSpec((1, vd), lambda i: (0, 0)),
            pl.BlockSpec((vd, d), lambda i: (0, 0)),
            pl.BlockSpec((tm, d), lambda i: (i, 0)),
        ],
        out_specs=pl.BlockSpec((tm, d), lambda i: (i, 0)),
        out_shape=jax.ShapeDtypeStruct((t, d), F32),
        scratch_shapes=[pltpu.VMEM((tm, vd), _MXU_DTYPE)],
        compiler_params=_params(("parallel",)),
        name="gated_out",
    )(o, p, gain, w, x2d)


_ROPE_THETA = 10000.0


def _rope_kernel(pos_ref, cr_ref, sr_ref, ca_ref, s1_ref, s2_ref, *, rt_half, at_half):
    pos = pos_ref[...]
    lane = lax.broadcasted_iota(jnp.int32, pos.shape, 1)
    inv = jnp.exp(lane.astype(F32) * (-math.log(_ROPE_THETA) / rt_half))
    ang = pos * inv
    cr_ref[...] = jnp.cos(ang)
    sr_ref[...] = jnp.sin(ang)
    j = lane % (2 * at_half)
    inv = jnp.exp((j % at_half).astype(F32) * (-math.log(_ROPE_THETA) / at_half))
    ang = pos * inv
    sn = jnp.sin(ang)
    ca_ref[...] = jnp.cos(ang)
    s1_ref[...] = jnp.where(j < at_half, -sn, 0.0)
    s2_ref[...] = jnp.where(j < at_half, 0.0, sn)


def _rope_tables(pos_b, *, rt_half, at_half, tm):
    t = pos_b.shape[0]
    spec = pl.BlockSpec((tm, _LANES), lambda i: (i, 0))
    return pl.pallas_call(
        functools.partial(_rope_kernel, rt_half=rt_half, at_half=at_half),
        grid=(t // tm,),
        in_specs=[spec],
        out_specs=[spec] * 5,
        out_shape=[jax.ShapeDtypeStruct((t, _LANES), F32)] * 5,
        compiler_params=_params(("parallel",)),
        name="rope_tables",
    )(pos_b)


_HG_CHUNK = 64
_HG_SUB = 16
_TINY = 1e-30
_TN = (((0,), (0,)), ((), ()))


def _split_dot(a, b_f32):
    hi = b_f32.astype(BF16)
    lo = (b_f32 - hi.astype(F32)).astype(BF16)
    return (jnp.dot(a, hi, preferred_element_type=F32) + jnp.dot(a, lo, preferred_element_type=F32))


def _hgrn_kernel(q_ref, f_ref, i_ref, lb_ref, o_ref, st_ref, *, layer, heads):
    c, sc = _HG_CHUNK, _HG_SUB
    nchunk = q_ref.shape[0] // c
    dk = q_ref.shape[1] // heads

    @pl.when(pl.program_id(1) == 0)
    def _():
        st_ref[...] = jnp.zeros_like(st_ref)

    lbs = lb_ref[...]
    e = jnp.exp(lbs - jnp.max(lbs, axis=0, keepdims=True))
    soft = e / jnp.sum(e, axis=0, keepdims=True)
    lb_all = jnp.zeros_like(soft[0:1])
    for l in range(1, layer + 1):
        lb_all = lb_all + soft[l:l + 1]

    r_i = lax.broadcasted_iota(jnp.int32, (c, c), 0)
    c_i = lax.broadcasted_iota(jnp.int32, (c, c), 1)
    tri = (r_i >= c_i).astype(BF16)
    sub_t = lax.broadcasted_iota(jnp.int32, (sc, 1), 0)

    def head_chunk(rows, hd):
        cols = slice(hd * dk, (hd + 1) * dk)
        lb = lb_all[:, cols]
        qr = q_ref[rows, cols]
        fr = f_ref[rows, cols]
        v = i_ref[rows, cols]
        q = qr * jax.nn.sigmoid(qr)
        forget = lb + (1.0 - lb) * jax.nn.sigmoid(fr)
        logf = jnp.log(jnp.maximum(forget, _TINY))
        k = (1.0 - lb) * jax.nn.sigmoid(-fr)
        if _MXU_DTYPE == BF16:
            b = _split_dot(tri, logf)
        else:
            b = jnp.dot(tri.astype(F32), logf, preferred_element_type=F32)
        st = st_ref[hd]
        o = lax.dot_general((q * jnp.exp(b)).astype(_MXU_DTYPE), st.astype(_MXU_DTYPE), _NT,
                            preferred_element_type=F32)
        vb = v.astype(_MXU_DTYPE)
        outs = []
        for i in range(c // sc):
            lo, hi = i * sc, (i + 1) * sc
            qi, bi = q[lo:hi], b[lo:hi]
            oi = o[lo:hi]
            if i > 0:
                ref_b = b[lo - 1:lo]
                qt = (qi * jnp.exp(bi - ref_b)).astype(_MXU_DTYPE)
                kt = (k[:lo] * jnp.exp(ref_b - b[:lo])).astype(_MXU_DTYPE)
                att = lax.dot_general(qt, kt, _NT, preferred_element_type=F32)
                oi = oi + jnp.dot(att.astype(_MXU_DTYPE), vb[:lo], preferred_element_type=F32)
            for s in range(sc):
                dec = jnp.exp(jnp.minimum(bi - bi[s:s + 1], 0.0))
                col = jnp.sum(qi * k[lo + s:lo + s + 1] * dec, axis=1, keepdims=True)
                col = jnp.where(sub_t >= s, col, 0.0)
                oi = oi + col * v[lo + s:lo + s + 1]
            outs.append(oi)
        o_ref[rows, cols] = jnp.concatenate(outs, axis=0)
        b_last = b[c - 1:c]
        kd = (k * jnp.exp(b_last - b)).astype(_MXU_DTYPE)
        st_ref[hd] = jnp.exp(b_last) * st + lax.dot_general(vb, kd, _TN, preferred_element_type=F32)

    def chunk(ci, carry):
        rows = pl.ds(pl.multiple_of(ci * c, c), c)
        for hd in range(heads):
            head_chunk(rows, hd)
        return carry

    lax.fori_loop(0, nchunk, chunk, 0)


def _hgrn_scan(p, hg_lb, *, batch, seq, heads, dk, layer, blk):
    t = p.shape[0]
    nblk = seq // blk
    width = heads * dk
    row = lambda b, s: b * nblk + s
    return pl.pallas_call(
        functools.partial(_hgrn_kernel, layer=layer, heads=heads),
        grid=(batch, nblk),
        in_specs=[
            pl.BlockSpec((blk, width), lambda b, s: (row(b, s), 0)),
            pl.BlockSpec((blk, width), lambda b, s: (row(b, s), 1)),
            pl.BlockSpec((blk, width), lambda b, s: (row(b, s), 2)),
            pl.BlockSpec((hg_lb.shape[0], width), lambda b, s: (0, 0)),
        ],
        out_specs=pl.BlockSpec((blk, width), lambda b, s: (row(b, s), 0)),
        out_shape=jax.ShapeDtypeStruct((t, width), F32),
        scratch_shapes=[pltpu.VMEM((heads, dk, dk), F32)],
        compiler_params=_params(("parallel", "arbitrary")),
        name="hgrn_scan",
    )(p, p, p, hg_lb)


_RT_CHUNK = 128


def _retention_kernel(q_ref, k_ref, v_ref, cos_ref, sin_ref, o_ref, st_ref):
    c = _RT_CHUNK
    nchunk = q_ref.shape[0] // c
    dk = q_ref.shape[1]
    half = dk // 2

    @pl.when(pl.program_id(2) == 0)
    def _():
        st_ref[...] = jnp.zeros_like(st_ref)

    hv = jnp.full((1, _LANES), pl.program_id(1), jnp.int32).astype(F32)
    lg = jnp.log1p(-jnp.exp2(-5.0 - hv))
    t_i = lax.broadcasted_iota(jnp.int32, (c, c), 0)
    s_i = lax.broadcasted_iota(jnp.int32, (c, c), 1)
    rel = (t_i - s_i).astype(F32)
    decay = jnp.where(rel >= 0, jnp.exp(lg * jnp.maximum(rel, 0.0)), 0.0)
    idx = lax.broadcasted_iota(jnp.int32, (c, _LANES), 0).astype(F32)
    q_dec = jnp.exp(lg * (idx + 1.0))
    k_dec = jnp.exp(lg * (c - 1.0 - idx))
    chunk_dec = jnp.exp(lg * c)[:, :1]

    def rope(x, cos, sin):
        x1, x2 = x[:, :half], x[:, half:]
        return x1 * cos - x2 * sin, x2 * cos + x1 * sin

    def chunk(ci, carry):
        rows = pl.ds(pl.multiple_of(ci * c, c), c)
        cos, sin = cos_ref[rows, :], sin_ref[rows, :]
        q1, q2 = rope(q_ref[rows, :], cos, sin)
        k1, k2 = rope(k_ref[rows, :], cos, sin)
        scale = dk ** -0.5
        k1, k2 = k1 * scale, k2 * scale
        cat = lambda a, b: jnp.concatenate([a, b], axis=1).astype(_MXU_DTYPE)
        qb, kb = cat(q1, q2), cat(k1, k2)
        vb = v_ref[rows, :].astype(_MXU_DTYPE)
        att = lax.dot_general(qb, kb, _NT, preferred_element_type=F32) * decay
        o = jnp.dot(att.astype(_MXU_DTYPE), vb, preferred_element_type=F32)
        st = st_ref[...]
        o = o + jnp.dot(cat(q1 * q_dec, q2 * q_dec), st.astype(_MXU_DTYPE), preferred_element_type=F32)
        o_ref[rows, :] = o
        st_ref[...] = chunk_dec * st + lax.dot_general(cat(k1 * k_dec, k2 * k_dec), vb, _TN,
                                                       preferred_element_type=F32)
        return carry

    lax.fori_loop(0, nchunk, chunk, 0)


def _retention_scan(p, cos, sin, *, batch, seq, heads, dk, dv, blk):
    t = p.shape[0]
    nblk = seq // blk
    row = lambda b, h, s: b * nblk + s
    v_off = 2 * heads * dk // dv
    return pl.pallas_call(
        _retention_kernel,
        grid=(batch, heads, nblk),
        in_specs=[
            pl.BlockSpec((blk, dk), lambda b, h, s: (row(b, h, s), h)),
            pl.BlockSpec((blk, dk), lambda b, h, s: (row(b, h, s), heads + h)),
            pl.BlockSpec((blk, dv), lambda b, h, s: (row(b, h, s), v_off + h)),
            pl.BlockSpec((blk, _LANES), lambda b, h, s: (row(b, h, s), 0)),
            pl.BlockSpec((blk, _LANES), lambda b, h, s: (row(b, h, s), 0)),
        ],
        out_specs=pl.BlockSpec((blk, dv), lambda b, h, s: (row(b, h, s), h)),
        out_shape=jax.ShapeDtypeStruct((t, heads * dv), F32),
        scratch_shapes=[pltpu.VMEM((dk, dv), F32)],
        compiler_params=_params(("parallel", "parallel", "arbitrary")),
        name="retention_scan",
    )(p, p, p, cos, sin)


_AT_BLOCK = 128
_MASK_VALUE = -1e30


def _attn_prep_kernel(p_ref, gain_ref, cos_ref, s1_ref, s2_ref, o_ref, *, head_dim):
    x = p_ref[...]
    width = x.shape[1]
    r_i = lax.broadcasted_iota(jnp.int32, (_LANES, _LANES), 0) // head_dim
    c_i = lax.broadcasted_iota(jnp.int32, (_LANES, _LANES), 1) // head_dim
    seg = jnp.where(r_i == c_i, 1.0 / head_dim, 0.0)
    cos, s1, s2 = cos_ref[...], s1_ref[...], s2_ref[...]
    gain = gain_ref[0]
    for j in range(width // _LANES):
        cols = slice(j * _LANES, (j + 1) * _LANES)
        xj = x[:, cols]
        if _MXU_DTYPE == BF16:
            msq = _split_dot_lhs(xj * xj, seg.astype(BF16))
        else:
            msq = jnp.dot(xj * xj, seg, preferred_element_type=F32)
        y = xj * lax.rsqrt(msq + _EPS) * gain[:, cols]
        half = head_dim // 2
        y = y * cos + pltpu.roll(y, _LANES - half, 1) * s1 + pltpu.roll(y, half, 1) * s2
        o_ref[:, cols] = y.astype(o_ref.dtype)


def _split_dot_lhs(a_f32, b):
    hi = a_f32.astype(BF16)
    lo = (a_f32 - hi.astype(F32)).astype(BF16)
    return (jnp.dot(hi, b, preferred_element_type=F32) + jnp.dot(lo, b, preferred_element_type=F32))


def _attn_prep(p, gains, cos, s1, s2, *, head_dim, tm):
    t = p.shape[0]
    nblk, _, width = gains.shape
    tab = pl.BlockSpec((tm, _LANES), lambda i, j: (i, 0))
    return pl.pallas_call(
        functools.partial(_attn_prep_kernel, head_dim=head_dim),
        grid=(t // tm, nblk),
        in_specs=[
            pl.BlockSpec((tm, width), lambda i, j: (i, j)),
            pl.BlockSpec((1, 1, width), lambda i, j: (j, 0, 0)),
            tab, tab, tab,
        ],
        out_specs=pl.BlockSpec((tm, width), lambda i, j: (i, j)),
        out_shape=jax.ShapeDtypeStruct((t, nblk * width), _MXU_DTYPE),
        compiler_params=_params(("parallel", "parallel")),
        name="attn_prep",
    )(p, gains, cos, s1, s2)


def _attn_kernel(q_ref, kp_ref, k_ref, vp_ref, v_ref, o_ref, l_ref, *, head_dim, span):
    blk = _AT_BLOCK
    nsub = q_ref.shape[0] // blk
    lane = lax.broadcasted_iota(jnp.int32, (1, _LANES), 1)
    qi = lax.broadcasted_iota(jnp.int32, (blk, 2 * blk), 0) + blk
    ki = lax.broadcasted_iota(jnp.int32, (blk, 2 * blk), 1)
    band = (ki <= qi) & (ki >= qi - span)
    first_lo = jnp.where(pl.program_id(3) == 0, blk, 0)
    band_first = (ki <= qi) & (ki >= jnp.maximum(qi - span, first_lo))
    scale = head_dim ** -0.5
    for j in range(nsub):
        rows = slice(j * blk, (j + 1) * blk)
        q = q_ref[rows, :]
        if j == 0:
            kk = jnp.concatenate([kp_ref[...], k_ref[rows, :]], axis=0)
            vv = jnp.concatenate([vp_ref[...], v_ref[rows, :]], axis=0)
            mask = band_first
        else:
            kk = k_ref[(j - 1) * blk:(j + 1) * blk, :]
            vv = v_ref[(j - 1) * blk:(j + 1) * blk, :]
            mask = band
        vv = vv.astype(_MXU_DTYPE)
        o_pair = None
        l_pair = None
        for hd in range(_LANES // head_dim):
            in_head = (lane // head_dim) == hd
            qh = jnp.where(in_head, q, jnp.zeros_like(q))
            s = lax.dot_general(qh, kk, _NT, preferred_element_type=F32) * scale
            s = jnp.where(mask, s, _MASK_VALUE)
            m = jnp.max(s, axis=-1, keepdims=True)
            pr = jnp.exp(s - m)
            l = jnp.sum(pr, axis=-1, keepdims=True)
            o = jnp.dot(pr.astype(_MXU_DTYPE), vv, preferred_element_type=F32) / l
            lse = m + jnp.log(l)
            if o_pair is None:
                o_pair, l_pair = o, jnp.broadcast_to(lse, o.shape)
            else:
                o_pair = jnp.where(in_head, o, o_pair)
                l_pair = jnp.where(in_head, lse, l_pair)
        o_ref[rows, :] = o_pair
        l_ref[rows, :] = l_pair


def _attn_group(qk, p, *, batch, seq, group, ngroup, heads, head_dim, window, dilation):
    t = p.shape[0]
    d = dilation
    width = heads * head_dim
    assert seq % (d * _AT_BLOCK) == 0 and window // d <= _AT_BLOCK
    ls = seq // d
    nb = ls // _AT_BLOCK
    r_sub = min(4, nb)
    nstep = nb // r_sub
    rows = r_sub * _AT_BLOCK
    qk_w = qk.shape[1] // _LANES
    p_w = p.shape[1] // _LANES
    pairs = width // _LANES
    qk_s = qk.reshape(t // d, d * qk.shape[1])
    p_s = p.reshape(t // d, d * p.shape[1])
    q_col = lambda r, hp: r * qk_w + group * pairs + hp
    k_col = lambda r, hp: r * qk_w + (ngroup + group) * pairs + hp
    v_col = lambda r, hp: r * p_w + (2 * ngroup + group) * pairs + hp
    o_col = lambda r, hp: r * pairs + hp
    main = lambda b, n: b * nstep + n
    prev = lambda b, n: jnp.maximum(b * nb + n * r_sub - 1, 0)
    out_spec = pl.BlockSpec((rows, _LANES), lambda b, r, hp, n: (main(b, n), o_col(r, hp)))
    o, lse = pl.pallas_call(
        functools.partial(_attn_kernel, head_dim=head_dim, span=window // d),
        grid=(batch, d, pairs, nstep),
        in_specs=[
            pl.BlockSpec((rows, _LANES), lambda b, r, hp, n: (main(b, n), q_col(r, hp))),
            pl.BlockSpec((_AT_BLOCK, _LANES), lambda b, r, hp, n: (prev(b, n), k_col(r, hp))),
            pl.BlockSpec((rows, _LANES), lambda b, r, hp, n: (main(b, n), k_col(r, hp))),
            pl.BlockSpec((_AT_BLOCK, _LANES), lambda b, r, hp, n: (prev(b, n), v_col(r, hp))),
            pl.BlockSpec((rows, _LANES), lambda b, r, hp, n: (main(b, n), v_col(r, hp))),
        ],
        out_specs=[out_spec, out_spec],
        out_shape=[jax.ShapeDtypeStruct((t // d, d * width), F32)] * 2,
        compiler_params=_params(("parallel", "parallel", "parallel", "arbitrary")),
        name=f"attn_group{group}",
    )(qk_s, qk_s, qk_s, p_s, p_s)
    return o.reshape(t, width), lse.reshape(t, width)


def _attn_out_kernel(*refs, ngroup):
    o_refs, l_refs = refs[:ngroup], refs[ngroup:2 * ngroup]
    w_ref, x_ref, y_ref = refs[2 * ngroup:]
    ls = [r[...] for r in l_refs]
    m = functools.reduce(jnp.maximum, ls)
    es = [jnp.exp(l - m) for l in ls]
    den = functools.reduce(lambda a, b: a + b, es)
    o = functools.reduce(lambda a, b: a + b, [e * r[...] for e, r in zip(es, o_refs)]) / den
    y_ref[...] = x_ref[...] + jnp.dot(o.astype(_MXU_DTYPE), w_ref[...], preferred_element_type=F32)


def _attn_out(os_, ls_, w, x2d, *, tm):
    t, width = os_[0].shape
    d = x2d.shape[1]
    ng = len(os_)
    blk = pl.BlockSpec((tm, width), lambda i: (i, 0))
    return pl.pallas_call(
        functools.partial(_attn_out_kernel, ngroup=ng),
        grid=(t // tm,),
        in_specs=[blk] * (2 * ng) + [pl.BlockSpec((width, d), lambda i: (0, 0)),
                                     pl.BlockSpec((tm, d), lambda i: (i, 0))],
        out_specs=pl.BlockSpec((tm, d), lambda i: (i, 0)),
        out_shape=jax.ShapeDtypeStruct((t, d), F32),
        compiler_params=_params(("parallel",)),
        name="attn_out",
    )(*os_, *ls_, w, x2d)


_HG_HEADS, _HG_DK = 8, 128
_AT_GROUPS = ((128, 1), (512, 4), (2048, 16))
_AT_HEADS, _AT_HEAD_DIM = 8, 64
_RT_HEADS = 4
_N_MIXERS = 3


def _pick(n, candidates):
    for c in candidates:
        if n % c == 0:
            return c
    return n


def _hgrn_mixer(x2d, gain, w_in, hg_lb, o_gain, w_out, *, batch, seq, layer, blk):
    t = x2d.shape[0]
    p = _norm_proj(x2d, gain, w_in, tm=_pick(t, (512, 256, 128)), tn=_pick(w_in.shape[1], (1024, 512)))
    o = _hgrn_scan(p, hg_lb, batch=batch, seq=seq, heads=_HG_HEADS, dk=_HG_DK, layer=layer, blk=min(blk, seq))
    return _gated_out(o, p, 3, o_gain, w_out, x2d, dv=_HG_DK, tm=_pick(t, (256, 128)))


def _retention_mixer(x2d, gain, w_in, o_gain, w_out, cos, sin, *, batch, seq, blk):
    t, d = x2d.shape
    dk = d // _RT_HEADS
    dv = 2 * dk
    assert dk == 2 * _LANES
    p = _norm_proj(x2d, gain, w_in, tm=_pick(t, (512, 256, 128)), tn=_pick(w_in.shape[1], (1536, 1024, 512)))
    o = _retention_scan(p, cos, sin, batch=batch, seq=seq, heads=_RT_HEADS, dk=dk, dv=dv, blk=min(blk, seq))
    return _gated_out(o, p, 2, o_gain, w_out, x2d, dv=dv, tm=_pick(t, (256, 128)))


def _attn_mixer(x2d, gain, w_in, q_gain, k_gain, w_out, cos, s1, s2, *, batch, seq):
    t = x2d.shape[0]
    ng = len(_AT_GROUPS)
    width = _AT_HEADS * _AT_HEAD_DIM
    p = _norm_proj(x2d, gain, w_in, tm=_pick(t, (512, 256, 128)), tn=_pick(w_in.shape[1], (1536, 512)))
    gains = jnp.concatenate([jnp.tile(q_gain, (1, _AT_HEADS)), jnp.tile(k_gain, (1, _AT_HEADS))], axis=0)
    qk = _attn_prep(p, gains[:, None, :], cos, s1, s2, head_dim=_AT_HEAD_DIM, tm=_pick(t, (512, 256, 128)))
    os_, ls_ = [], []
    for g, (window, dilation) in enumerate(_AT_GROUPS):
        o, lse = _attn_group(qk, p, batch=batch, seq=seq, group=g, ngroup=ng, heads=_AT_HEADS,
                             head_dim=_AT_HEAD_DIM, window=window, dilation=dilation)
        os_.append(o)
        ls_.append(lse)
    return _attn_out(os_, ls_, w_out, x2d, tm=_pick(t, (512, 256, 128)))


def kernel(x, positions, mix_norm, ffn_norm, hg_lb, hg_w_in, hg_onorm, hg_w_out, at_w_in, at_qnorm, at_knorm,
           at_w_out, rt_w_in, rt_onorm, rt_w_out, pk_w_q, pk_keys, pk_u, pk_v):
    b, s, d = x.shape
    depth = mix_norm.shape[0]
    x2d = x.reshape(b * s, d)
    pos_b = jnp.broadcast_to(positions.reshape(b * s, 1).astype(F32), (b * s, _LANES))
    rt_dk = d // _RT_HEADS
    cos_r, sin_r, cos_a, s1_a, s2_a = _rope_tables(pos_b, rt_half=rt_dk // 2, at_half=_AT_HEAD_DIM // 2, tm=512)
    ia = ib = ic = 0
    for layer in range(depth):
        kind = layer % _N_MIXERS
        gain = mix_norm[layer][None, :]
        if kind == 0:
            x2d = _hgrn_mixer(x2d, gain, hg_w_in[ia].astype(_MXU_DTYPE), hg_lb, hg_onorm[ia][None, :],
                              hg_w_out[ia].astype(_MXU_DTYPE), batch=b, seq=s, layer=layer, blk=256)
            ia += 1
        elif kind == 1:
            x2d = _attn_mixer(x2d, gain, at_w_in[ib].astype(_MXU_DTYPE), at_qnorm[ib], at_knorm[ib],
                              at_w_out[ib].astype(_MXU_DTYPE), cos_a, s1_a, s2_a, batch=b, seq=s)
            ib += 1
        else:
            x2d = _retention_mixer(x2d, gain, rt_w_in[ic].astype(_MXU_DTYPE), rt_onorm[ic][None, :],
                                   rt_w_out[ic].astype(_MXU_DTYPE), cos_r, sin_r, batch=b, seq=s, blk=256)
            ic += 1
        keys = pk_keys[layer].reshape(-1, pk_keys.shape[-2], pk_keys.shape[-1]).astype(_MXU_DTYPE)
        x2d = _peer_layer(x2d, ffn_norm[layer][None, :], pk_w_q[layer].astype(_MXU_DTYPE), keys,
                          pk_u[layer].astype(_MXU_DTYPE), pk_v[layer].T.astype(_MXU_DTYPE),
                          tb_route=512, tb_dense=512, a_t=8)
    return x2d.reshape(b, s, d)
```

```python
import functools
import math

import jax
import jax.numpy as jnp
from jax import lax
from jax.experimental import pallas as pl
from jax.experimental.pallas import tpu as pltpu

F32 = jnp.float32
BF16 = jnp.bfloat16
_MXU_DTYPE = BF16
_EPS = 1e-6
_LANES = 128
_SUBLANES = 8
_VMEM_LIMIT = 56 * 1024 * 1024

_NT = (((1,), (1,)), ((), ()))


def _params(sem, flags=None):
    return pltpu.CompilerParams(dimension_semantics=sem, vmem_limit_bytes=_VMEM_LIMIT, flags=flags)


def _rms(x, gain):
    return x * lax.rsqrt(jnp.mean(x * x, axis=-1, keepdims=True) + _EPS) * gain


def _merge_desc(c):
    c = list(c)
    n = len(c)
    j = n // 2
    while j >= 1:
        for i in range(n):
            l = i ^ j
            if l > i:
                hi = jnp.maximum(c[i], c[l])
                lo = jnp.minimum(c[i], c[l])
                c[i], c[l] = hi, lo
        j //= 2
    return c


def _sort_desc(xs):
    xs = list(xs)
    n = len(xs)
    k = 2
    while k <= n:
        j = k // 2
        while j >= 1:
            for i in range(n):
                l = i ^ j
                if l > i:
                    hi = jnp.maximum(xs[i], xs[l])
                    lo = jnp.minimum(xs[i], xs[l])
                    if (i & k) == 0:
                        xs[i], xs[l] = hi, lo
                    else:
                        xs[i], xs[l] = lo, hi
            j //= 2
        k *= 2
    return xs


def _top_of_two(a, b):
    n = len(a)
    return _merge_desc([jnp.maximum(a[i], b[n - 1 - i]) for i in range(n)])


_PK_TOPK = 16
_PK_PAIRS = [(i, j) for i in range(_PK_TOPK) for j in range(_PK_TOPK) if (i + 1) * (j + 1) <= _PK_TOPK]


_NEG = -3.0e38


def _top16_sums(a_top, b_top):
    cand = [a_top[i] + b_top[j] for (i, j) in _PK_PAIRS]
    pad = jnp.full_like(cand[0], _NEG)
    cand = cand + [pad] * (64 - len(cand))
    groups = [_sort_desc(cand[16 * g:16 * (g + 1)]) for g in range(4)]
    return _top_of_two(_top_of_two(groups[0], groups[1]), _top_of_two(groups[2], groups[3]))


def _gate_word(x, dtype):
    if dtype == F32:
        return x
    u = lax.bitcast_convert_type(x.astype(BF16).astype(F32), jnp.uint32)
    return u | (u >> 16)


def _gate_rows(ref, hd, a, lanes, wdt):
    row = jnp.broadcast_to(ref[hd, pl.ds(a, 1), lanes], (_SUBLANES, _LANES))
    return row if wdt == F32 else pltpu.bitcast(row, wdt)


def _peer_route_kernel(x_ref, g_ref, wq_ref, keys_ref, h_ref, ea_ref, ca_ref, eb_ref, rb_ref,
                       s_scr, top_scr, btop_scr, *, heads, nkeys):
    tb = x_ref.shape[0]
    hf = _rms(x_ref[...], g_ref[...])
    hb = hf.astype(_MXU_DTYPE)
    h_ref[...] = hf.T.astype(_MXU_DTYPE)
    q = jnp.dot(hb, wq_ref[...], preferred_element_type=F32)
    for hc in range(2 * heads):
        qs = q[:, hc * nkeys:(hc + 1) * nkeys].astype(_MXU_DTYPE)
        s_scr[hc] = lax.dot_general(keys_ref[hc], qs, _NT, preferred_element_type=F32)

    nrow = nkeys // _SUBLANES
    bcast = lambda row: jnp.broadcast_to(row, (_SUBLANES, _LANES))

    def group(gi, carry):
        lanes = pl.ds(pl.multiple_of(gi * _LANES, _LANES), _LANES)
        for hc in range(2 * heads):
            hd, c = divmod(hc, 2)
            rows = [s_scr[hc, pl.ds(_SUBLANES * m, _SUBLANES), lanes] for m in range(nrow)]
            srt = _sort_desc(rows)
            for shift in (4, 2, 1):
                srt = _top_of_two(srt, [pltpu.roll(r, shift, 0) for r in srt])
            for i in range(_PK_TOPK):
                top_scr[c, i, pl.ds(hd, 1), :] = srt[i][0:1, :]
            if c == 1:
                for i in range(_PK_TOPK):
                    btop_scr[hd, i] = srt[i]
                ebs, rbs = [], []
                for m in range(nrow):
                    ebs.append(jnp.exp(rows[m] - srt[0]))
                    rank = jnp.full_like(rows[m], float(_PK_TOPK))
                    for i in reversed(range(_PK_TOPK)):
                        rank = jnp.where(srt[i] <= rows[m], float(i), rank)
                    rbs.append(rank)
                for m in range(0, nrow, 2):
                    sl = pl.ds(_SUBLANES * m, 2 * _SUBLANES)
                    eb_ref[hd, sl, lanes] = jnp.concatenate(ebs[m:m + 2], axis=0).astype(eb_ref.dtype)
                    rb_ref[hd, sl, lanes] = jnp.concatenate(rbs[m:m + 2], axis=0).astype(rb_ref.dtype)
        a_s = [top_scr[0, i] for i in range(_PK_TOPK)]
        b_s = [top_scr[1, i] for i in range(_PK_TOPK)]
        top = _top16_sums(a_s, b_s)
        z = jnp.exp(top[0] - top[0])
        for t in top[1:]:
            z = z + jnp.exp(t - top[0])
        rz = 1.0 / z
        theta = top[_PK_TOPK - 1]
        for hd in range(heads):
            amax = bcast(a_s[0][hd:hd + 1, :])
            rz_h = bcast(rz[hd:hd + 1, :])
            th_h = bcast(theta[hd:hd + 1, :])
            for m in range(nrow):
                sl = pl.ds(_SUBLANES * m, _SUBLANES)
                sa = s_scr[2 * hd, sl, lanes]
                ea_ref[hd, sl, lanes] = _gate_word(jnp.exp(sa - amax) * rz_h, ea_ref.dtype)
                cnt = jnp.full_like(sa, float(_PK_TOPK))
                for j in reversed(range(_PK_TOPK)):
                    cnt = jnp.where(sa + btop_scr[hd, j] < th_h, float(j), cnt)
                ca_ref[hd, sl, lanes] = _gate_word(cnt, ca_ref.dtype)
        return carry

    lax.fori_loop(0, tb // _LANES, group, 0)


def _peer_route(x2d, gain, wq, keys, *, tb):
    t, d = x2d.shape
    hc, nkeys, dk = keys.shape
    heads = hc // 2
    kern = functools.partial(_peer_route_kernel, heads=heads, nkeys=nkeys)
    tab = pl.BlockSpec((heads, nkeys, tb), lambda i: (0, 0, i))
    word = F32 if _MXU_DTYPE == F32 else jnp.uint32
    return pl.pallas_call(
        kern,
        grid=(t // tb,),
        in_specs=[
            pl.BlockSpec((tb, d), lambda i: (i, 0)),
            pl.BlockSpec((1, d), lambda i: (0, 0)),
            pl.BlockSpec(wq.shape, lambda i: (0, 0)),
            pl.BlockSpec(keys.shape, lambda i: (0, 0, 0)),
        ],
        out_specs=[pl.BlockSpec((d, tb), lambda i: (0, i)), tab, tab, tab, tab],
        out_shape=[
            jax.ShapeDtypeStruct((d, t), _MXU_DTYPE),
            jax.ShapeDtypeStruct((heads, nkeys, t), word),
            jax.ShapeDtypeStruct((heads, nkeys, t), word),
            jax.ShapeDtypeStruct((heads, nkeys, t), _MXU_DTYPE),
            jax.ShapeDtypeStruct((heads, nkeys, t), _MXU_DTYPE),
        ],
        scratch_shapes=[
            pltpu.VMEM((hc, nkeys, tb), F32),
            pltpu.VMEM((2, _PK_TOPK, heads, _LANES), F32),
            pltpu.VMEM((heads, _PK_TOPK, _SUBLANES, _LANES), F32),
        ],
        compiler_params=_params(("parallel",)),
        name="peer_route",
    )(x2d, gain, wq, keys)


def _gelu(x):
    return 0.5 * x * (1.0 + lax.erf(x * (1.0 / math.sqrt(2.0))))


def _peer_dense_kernel(h_ref, u_ref, vt_ref, ea_ref, ca_ref, eb_ref, rb_ref, x_ref, o_ref, acc_ref,
                       *, heads, nkeys):
    e = pl.program_id(1)
    tb = h_ref.shape[1]
    a_t = ea_ref.shape[1]
    wdt = vt_ref.dtype
    pk = _SUBLANES * (4 // jnp.dtype(wdt).itemsize)
    zero = jnp.zeros((pk, _LANES), wdt)
    mc = _pick(tb, (256,))

    @pl.when(e == 0)
    def _():
        acc_ref[...] = jnp.zeros_like(acc_ref)

    for c in range(tb // mc):
        toks = slice(c * mc, (c + 1) * mc)
        mr = 2 * nkeys
        hu = jnp.concatenate(
            [jnp.dot(u_ref[r * mr:(r + 1) * mr, :], h_ref[:, toks], preferred_element_type=F32)
             for r in range(a_t * nkeys // mr)], axis=0)
        cols = [[] for _ in range(mc // _LANES)]
        for a in range(a_t):
            for tc in range(mc // _LANES):
                lanes = slice(c * mc + tc * _LANES, c * mc + (tc + 1) * _LANES)
                pieces = cols[tc]
                for bc in range(nkeys // pk):
                    rows = slice(bc * pk, (bc + 1) * pk)
                    acc = None
                    for hd in range(heads):
                        ea_b = _gate_rows(ea_ref, hd, a, lanes, wdt)
                        ca_b = _gate_rows(ca_ref, hd, a, lanes, wdt)
                        sel = lax.clamp(zero, ca_b - rb_ref[hd, rows, lanes], eb_ref[hd, rows, lanes])
                        term = ea_b * sel
                        acc = term if acc is None else acc + term
                    r0 = a * nkeys + bc * pk
                    act = _gelu(hu[r0:r0 + pk, tc * _LANES:(tc + 1) * _LANES])
                    pieces.append(acc * act.astype(wdt))
        w = jnp.concatenate([jnp.concatenate(p, axis=0) for p in cols], axis=1)
        acc_ref[:, toks] += jnp.dot(vt_ref[...], w, preferred_element_type=F32)

    @pl.when(e == pl.num_programs(1) - 1)
    def _():
        o_ref[...] = x_ref[...] + acc_ref[...].T


def _peer_dense(h, u, vt, ea, ca, eb, rb, x2d, *, tb, a_t):
    t, d = x2d.shape
    heads, nkeys, _ = ea.shape
    e_t = a_t * nkeys
    n_e = u.shape[0] // e_t
    kern = functools.partial(_peer_dense_kernel, heads=heads, nkeys=nkeys)
    gate_a = pl.BlockSpec((heads, a_t, tb), lambda i, e: (0, e, i))
    gate_b = pl.BlockSpec((heads, nkeys, tb), lambda i, e: (0, 0, i))
    return pl.pallas_call(
        kern,
        grid=(t // tb, n_e),
        in_specs=[
            pl.BlockSpec((d, tb), lambda i, e: (0, i)),
            pl.BlockSpec((e_t, d), lambda i, e: (e, 0)),
            pl.BlockSpec((d, e_t), lambda i, e: (0, e)),
            gate_a, gate_a, gate_b, gate_b,
            pl.BlockSpec((tb, d), lambda i, e: (i, 0)),
        ],
        out_specs=pl.BlockSpec((tb, d), lambda i, e: (i, 0)),
        out_shape=jax.ShapeDtypeStruct((t, d), F32),
        scratch_shapes=[pltpu.VMEM((d, tb), F32)],
        compiler_params=_params(("parallel", "arbitrary")),
        name="peer_dense",
    )(h, u, vt, ea, ca, eb, rb, x2d)


def _peer_layer(x2d, gain, wq, keys, u, vt, *, tb_route, tb_dense, a_t):
    h, ea, ca, eb, rb = _peer_route(x2d, gain, wq, keys, tb=tb_route)
    return _peer_dense(h, u, vt, ea, ca, eb, rb, x2d, tb=tb_dense, a_t=a_t)


def _norm_proj_kernel(x_ref, g_ref, w_ref, o_ref, hb_ref):
    @pl.when(pl.program_id(1) == 0)
    def _():
        hb_ref[...] = _rms(x_ref[...], g_ref[...]).astype(hb_ref.dtype)

    o_ref[...] = jnp.dot(hb_ref[...], w_ref[...], preferred_element_type=F32)


def _norm_proj(x2d, gain, w, *, tm, tn):
    t, d = x2d.shape
    n = w.shape[1]
    return pl.pallas_call(
        _norm_proj_kernel,
        grid=(t // tm, n // tn),
        in_specs=[
            pl.BlockSpec((tm, d), lambda i, j: (i, 0)),
            pl.BlockSpec((1, d), lambda i, j: (0, 0)),
            pl.BlockSpec((d, tn), lambda i, j: (0, j)),
        ],
        out_specs=pl.BlockSpec((tm, tn), lambda i, j: (i, j)),
        out_shape=jax.ShapeDtypeStruct((t, n), F32),
        scratch_shapes=[pltpu.VMEM((tm, d), _MXU_DTYPE)],
        compiler_params=_params(("parallel", "arbitrary")),
        name="norm_proj",
    )(x2d, gain, w)


def _gated_out_kernel(o_ref, g_ref, gain_ref, w_ref, x_ref, y_ref, a_ref, *, dv):
    vd = o_ref.shape[1]
    for hd in range(vd // dv):
        cols = slice(hd * dv, (hd + 1) * dv)
        o = o_ref[:, cols]
        g = g_ref[:, cols]
        a_ref[:, cols] = (_rms(o, gain_ref[:, cols]) * (g * jax.nn.sigmoid(g))).astype(a_ref.dtype)
    y_ref[...] = x_ref[...] + jnp.dot(a_ref[...], w_ref[...], preferred_element_type=F32)


def _gated_out(o, p, g_block, gain, w, x2d, *, dv, tm):
    t, vd = o.shape
    d = x2d.shape[1]
    return pl.pallas_call(
        functools.partial(_gated_out_kernel, dv=dv),
        grid=(t // tm,),
        in_specs=[
            pl.BlockSpec((tm, vd), lambda i: (i, 0)),
            pl.BlockSpec((tm, vd), lambda i: (i, g_block)),
            pl.BlockSpec((1, vd), lambda i: (0, 0)),
            pl.BlockSpec((vd, d), lambda i: (0, 0)),
            pl.BlockSpec((tm, d), lambda i: (i, 0)),
        ],
        out_specs=pl.BlockSpec((tm, d), lambda i: (i, 0)),
        out_shape=jax.ShapeDtypeStruct((t, d), F32),
        scratch_shapes=[pltpu.VMEM((tm, vd), _MXU_DTYPE)],
        compiler_params=_params(("parallel",)),
        name="gated_out",
    )(o, p, gain, w, x2d)


_ROPE_THETA = 10000.0


def _rope_kernel(pos_ref, cr_ref, sr_ref, ta_ref, *, rt_half, at_half):
    pos = pos_ref[...]
    lane = lax.broadcasted_iota(jnp.int32, pos.shape, 1)
    inv = jnp.exp(lane.astype(F32) * (-math.log(_ROPE_THETA) / rt_half))
    ang = pos * inv
    cr_ref[...] = jnp.cos(ang)
    sr_ref[...] = jnp.sin(ang)
    j = lane % (2 * at_half)
    inv = jnp.exp((j % at_half).astype(F32) * (-math.log(_ROPE_THETA) / at_half))
    ang = pos * inv
    sn = jnp.sin(ang)
    ta_ref[:, :_LANES] = jnp.cos(ang)
    ta_ref[:, _LANES:2 * _LANES] = jnp.where(j < at_half, -sn, 0.0)
    ta_ref[:, 2 * _LANES:] = jnp.where(j < at_half, 0.0, sn)


def _rope_tables(pos_b, *, rt_half, at_half, tm):
    t = pos_b.shape[0]
    spec = pl.BlockSpec((tm, _LANES), lambda i: (i, 0))
    return pl.pallas_call(
        functools.partial(_rope_kernel, rt_half=rt_half, at_half=at_half),
        grid=(t // tm,),
        in_specs=[spec],
        out_specs=[spec, spec, pl.BlockSpec((tm, 3 * _LANES), lambda i: (i, 0))],
        out_shape=[jax.ShapeDtypeStruct((t, _LANES), F32)] * 2 + [jax.ShapeDtypeStruct((t, 3 * _LANES), F32)],
        compiler_params=_params(("parallel",)),
        name="rope_tables",
    )(pos_b)


_HG_CHUNK = 64
_HG_SUB = 16
_TINY = 1e-30
_TN = (((0,), (0,)), ((), ()))


def _split_dot(a, b_f32):
    hi = b_f32.astype(BF16)
    lo = (b_f32 - hi.astype(F32)).astype(BF16)
    return (jnp.dot(a, hi, preferred_element_type=F32) + jnp.dot(a, lo, preferred_element_type=F32))


def _hgrn_kernel(q_ref, f_ref, i_ref, lb_ref, o_ref, st_ref, *, layer, heads):
    c, sc = _HG_CHUNK, _HG_SUB
    nchunk = q_ref.shape[0] // c
    dk = q_ref.shape[1] // heads

    @pl.when(pl.program_id(1) == 0)
    def _():
        st_ref[...] = jnp.zeros_like(st_ref)

    lbs = lb_ref[...]
    e = jnp.exp(lbs - jnp.max(lbs, axis=0, keepdims=True))
    soft = e / jnp.sum(e, axis=0, keepdims=True)
    lb_all = jnp.zeros_like(soft[0:1])
    for l in range(1, layer + 1):
        lb_all = lb_all + soft[l:l + 1]

    r_i = lax.broadcasted_iota(jnp.int32, (c, c), 0)
    c_i = lax.broadcasted_iota(jnp.int32, (c, c), 1)
    tri = (r_i >= c_i).astype(BF16)
    sub_t = lax.broadcasted_iota(jnp.int32, (sc, 1), 0)

    def head_chunk(rows, hd):
        cols = slice(hd * dk, (hd + 1) * dk)
        lb = lb_all[:, cols]
        qr = q_ref[rows, cols]
        fr = f_ref[rows, cols]
        v = i_ref[rows, cols]
        q = qr * jax.nn.sigmoid(qr)
        forget = lb + (1.0 - lb) * jax.nn.sigmoid(fr)
        logf = jnp.log(jnp.maximum(forget, _TINY))
        k = (1.0 - lb) * jax.nn.sigmoid(-fr)
        if _MXU_DTYPE == BF16:
            b = _split_dot(tri, logf)
        else:
            b = jnp.dot(tri.astype(F32), logf, preferred_element_type=F32)
        st = st_ref[hd]
        o = lax.dot_general((q * jnp.exp(b)).astype(_MXU_DTYPE), st.astype(_MXU_DTYPE), _NT,
                            preferred_element_type=F32)
        vb = v.astype(_MXU_DTYPE)
        outs = []
        for i in range(c // sc):
            lo, hi = i * sc, (i + 1) * sc
            qi, bi = q[lo:hi], b[lo:hi]
            oi = o[lo:hi]
            if i > 0:
                ref_b = b[lo - 1:lo]
                qt = (qi * jnp.exp(bi - ref_b)).astype(_MXU_DTYPE)
                kt = (k[:lo] * jnp.exp(ref_b - b[:lo])).astype(_MXU_DTYPE)
                att = lax.dot_general(qt, kt, _NT, preferred_element_type=F32)
                oi = oi + jnp.dot(att.astype(_MXU_DTYPE), vb[:lo], preferred_element_type=F32)
            for s in range(sc):
                dec = jnp.exp(jnp.minimum(bi - bi[s:s + 1], 0.0))
                col = jnp.sum(qi * k[lo + s:lo + s + 1] * dec, axis=1, keepdims=True)
                col = jnp.where(sub_t >= s, col, 0.0)
                oi = oi + col * v[lo + s:lo + s + 1]
            outs.append(oi)
        o_ref[rows, cols] = jnp.concatenate(outs, axis=0)
        b_last = b[c - 1:c]
        kd = (k * jnp.exp(b_last - b)).astype(_MXU_DTYPE)
        st_ref[hd] = jnp.exp(b_last) * st + lax.dot_general(vb, kd, _TN, preferred_element_type=F32)

    def chunk(ci, carry):
        rows = pl.ds(pl.multiple_of(ci * c, c), c)
        for hd in range(heads):
            head_chunk(rows, hd)
        return carry

    lax.fori_loop(0, nchunk, chunk, 0)


def _hgrn_scan(p, hg_lb, *, batch, seq, heads, dk, layer, blk):
    t = p.shape[0]
    nblk = seq // blk
    width = heads * dk
    row = lambda b, s: b * nblk + s
    return pl.pallas_call(
        functools.partial(_hgrn_kernel, layer=layer, heads=heads),
        grid=(batch, nblk),
        in_specs=[
            pl.BlockSpec((blk, width), lambda b, s: (row(b, s), 0)),
            pl.BlockSpec((blk, width), lambda b, s: (row(b, s), 1)),
            pl.BlockSpec((blk, width), lambda b, s: (row(b, s), 2)),
            pl.BlockSpec((hg_lb.shape[0], width), lambda b, s: (0, 0)),
        ],
        out_specs=pl.BlockSpec((blk, width), lambda b, s: (row(b, s), 0)),
        out_shape=jax.ShapeDtypeStruct((t, width), F32),
        scratch_shapes=[pltpu.VMEM((heads, dk, dk), F32)],
        compiler_params=_params(("parallel", "arbitrary")),
        name="hgrn_scan",
    )(p, p, p, hg_lb)


_RT_CHUNK = 128


def _retention_kernel(q_ref, k_ref, v_ref, cos_ref, sin_ref, o_ref, st_ref):
    c = _RT_CHUNK
    nchunk = q_ref.shape[0] // c
    dk = q_ref.shape[1]
    half = dk // 2

    @pl.when(pl.program_id(2) == 0)
    def _():
        st_ref[...] = jnp.zeros_like(st_ref)

    hv = jnp.full((1, _LANES), pl.program_id(1), jnp.int32).astype(F32)
    lg = jnp.log1p(-jnp.exp2(-5.0 - hv))
    t_i = lax.broadcasted_iota(jnp.int32, (c, c), 0)
    s_i = lax.broadcasted_iota(jnp.int32, (c, c), 1)
    rel = (t_i - s_i).astype(F32)
    decay = jnp.where(rel >= 0, jnp.exp(lg * jnp.maximum(rel, 0.0)), 0.0)
    idx = lax.broadcasted_iota(jnp.int32, (c, _LANES), 0).astype(F32)
    q_dec = jnp.exp(lg * (idx + 1.0))
    k_dec = jnp.exp(lg * (c - 1.0 - idx))
    chunk_dec = jnp.exp(lg * c)[:, :1]

    def rope(x, cos, sin):
        x1, x2 = x[:, :half], x[:, half:]
        return x1 * cos - x2 * sin, x2 * cos + x1 * sin

    def chunk(ci, carry):
        rows = pl.ds(pl.multiple_of(ci * c, c), c)
        cos, sin = cos_ref[rows, :], sin_ref[rows, :]
        q1, q2 = rope(q_ref[rows, :], cos, sin)
        k1, k2 = rope(k_ref[rows, :], cos, sin)
        scale = dk ** -0.5
        k1, k2 = k1 * scale, k2 * scale
        cat = lambda a, b: jnp.concatenate([a, b], axis=1).astype(_MXU_DTYPE)
        qb, kb = cat(q1, q2), cat(k1, k2)
        vb = v_ref[rows, :].astype(_MXU_DTYPE)
        att = lax.dot_general(qb, kb, _NT, preferred_element_type=F32) * decay
        o = jnp.dot(att.astype(_MXU_DTYPE), vb, preferred_element_type=F32)
        st = st_ref[...]
        o = o + jnp.dot(cat(q1 * q_dec, q2 * q_dec), st.astype(_MXU_DTYPE), preferred_element_type=F32)
        o_ref[rows, :] = o
        st_ref[...] = chunk_dec * st + lax.dot_general(cat(k1 * k_dec, k2 * k_dec), vb, _TN,
                                                       preferred_element_type=F32)
        return carry

    lax.fori_loop(0, nchunk, chunk, 0)


def _retention_scan(p, cos, sin, *, batch, seq, heads, dk, dv, blk):
    t = p.shape[0]
    nblk = seq // blk
    row = lambda b, h, s: b * nblk + s
    v_off = 2 * heads * dk // dv
    return pl.pallas_call(
        _retention_kernel,
        grid=(batch, heads, nblk),
        in_specs=[
            pl.BlockSpec((blk, dk), lambda b, h, s: (row(b, h, s), h)),
            pl.BlockSpec((blk, dk), lambda b, h, s: (row(b, h, s), heads + h)),
            pl.BlockSpec((blk, dv), lambda b, h, s: (row(b, h, s), v_off + h)),
            pl.BlockSpec((blk, _LANES), lambda b, h, s: (row(b, h, s), 0)),
            pl.BlockSpec((blk, _LANES), lambda b, h, s: (row(b, h, s), 0)),
        ],
        out_specs=pl.BlockSpec((blk, dv), lambda b, h, s: (row(b, h, s), h)),
        out_shape=jax.ShapeDtypeStruct((t, heads * dv), F32),
        scratch_shapes=[pltpu.VMEM((dk, dv), F32)],
        compiler_params=_params(("parallel", "parallel", "arbitrary")),
        name="retention_scan",
    )(p, p, p, cos, sin)


_AT_BLOCK = 128
_MASK_VALUE = -1e30


def _split_dot_lhs(a_f32, b):
    hi = a_f32.astype(BF16)
    lo = (a_f32 - hi.astype(F32)).astype(BF16)
    return (jnp.dot(hi, b, preferred_element_type=F32) + jnp.dot(lo, b, preferred_element_type=F32))


def _attn_kernel(q_ref, kp_ref, k_ref, vp_ref, v_ref, t0_ref, t1_ref, t2_ref, p0_ref, p1_ref, p2_ref,
                 gq_ref, gk_ref, o_ref, l_ref, *, head_dim, span, dil, nsub):
    blk = _AT_BLOCK
    half = head_dim // 2
    lane = lax.broadcasted_iota(jnp.int32, (1, _LANES), 1)
    qi = lax.broadcasted_iota(jnp.int32, (blk, 2 * blk), 0) + blk
    ki = lax.broadcasted_iota(jnp.int32, (blk, 2 * blk), 1)
    band = (ki <= qi) & (ki >= qi - span)
    first_lo = jnp.where(pl.program_id(2) == 0, blk, 0)
    band_first = (ki <= qi) & (ki >= jnp.maximum(qi - span, first_lo))
    scale = head_dim ** -0.5
    r_i = lax.broadcasted_iota(jnp.int32, (_LANES, _LANES), 0) // head_dim
    c_i = lax.broadcasted_iota(jnp.int32, (_LANES, _LANES), 1) // head_dim
    seg = jnp.where(r_i == c_i, 1.0 / head_dim, 0.0)
    gq, gk = gq_ref[...], gk_ref[...]

    def prep(x, gain, tab):
        if _MXU_DTYPE == BF16:
            msq = _split_dot_lhs(x * x, seg.astype(BF16))
        else:
            msq = jnp.dot(x * x, seg, preferred_element_type=F32)
        y = x * lax.rsqrt(msq + _EPS) * gain
        cos, s1, s2 = tab
        y = y * cos + pltpu.roll(y, _LANES - half, 1) * s1 + pltpu.roll(y, half, 1) * s2
        return y.astype(_MXU_DTYPE)

    def residue(r, carry):
        def take(ref, j):
            if dil == 1:
                return ref[j * blk:(j + 1) * blk, :]
            return ref[pl.ds(r + j * blk * dil, blk, stride=dil), :]

        kprev = prep(take(kp_ref, 0), gk, [take(t, 0) for t in (p0_ref, p1_ref, p2_ref)])
        vprev = take(vp_ref, 0).astype(_MXU_DTYPE)
        for j in range(nsub):
            tab = [take(t, j) for t in (t0_ref, t1_ref, t2_ref)]
            q = prep(take(q_ref, j), gq, tab)
            kcur = prep(take(k_ref, j), gk, tab)
            vcur = take(v_ref, j).astype(_MXU_DTYPE)
            kk = jnp.concatenate([kprev, kcur], axis=0)
            vv = jnp.concatenate([vprev, vcur], axis=0)
            mask = band_first if j == 0 else band
            o_pair = None
            l_pair = None
            for hd in range(_LANES // head_dim):
                in_head = (lane // head_dim) == hd
                qh = jnp.where(in_head, q, jnp.zeros_like(q))
                s = lax.dot_general(qh, kk, _NT, preferred_element_type=F32) * scale
                s = jnp.where(mask, s, _MASK_VALUE)
                m = jnp.max(s, axis=-1, keepdims=True)
                pr = jnp.exp(s - m)
                l = jnp.sum(pr, axis=-1, keepdims=True)
                o = jnp.dot(pr.astype(_MXU_DTYPE), vv, preferred_element_type=F32) / l
                lse = m + jnp.log(l)
                if o_pair is None:
                    o_pair, l_pair = o, jnp.broadcast_to(lse, o.shape)
                else:
                    o_pair = jnp.where(in_head, o, o_pair)
                    l_pair = jnp.where(in_head, lse, l_pair)
            if dil == 1:
                o_ref[j * blk:(j + 1) * blk, :] = o_pair
                l_ref[j * blk:(j + 1) * blk, :] = l_pair
            else:
                o_ref[pl.ds(r + j * blk * dil, blk, stride=dil), :] = o_pair
                l_ref[pl.ds(r + j * blk * dil, blk, stride=dil), :] = l_pair
            kprev, vprev = kcur, vcur
        return carry

    if dil == 1:
        residue(0, 0)
    else:
        lax.fori_loop(0, dil, residue, 0)


def _attn_group(p, tabs, gq, gk, *, batch, seq, group, ngroup, heads, head_dim, window, dilation):
    t = p.shape[0]
    d = dilation
    width = heads * head_dim
    assert seq % (d * _AT_BLOCK) == 0 and window // d <= _AT_BLOCK
    nb = seq // d // _AT_BLOCK
    nsub = max(1, min(4, nb, 2048 // (_AT_BLOCK * d)))
    nstep = nb // nsub
    rows = nsub * _AT_BLOCK * d
    prow = _AT_BLOCK * d
    pairs = width // _LANES
    col = lambda kind, hp: (kind * ngroup + group) * pairs + hp
    main = lambda b, n: b * nstep + n
    prev = lambda b, n: jnp.maximum((b * nstep + n) * nsub - 1, 0)
    out_spec = pl.BlockSpec((rows, _LANES), lambda b, hp, n: (main(b, n), hp))
    return pl.pallas_call(
        functools.partial(_attn_kernel, head_dim=head_dim, span=window // d, dil=d, nsub=nsub),
        grid=(batch, pairs, nstep),
        in_specs=[
            pl.BlockSpec((rows, _LANES), lambda b, hp, n: (main(b, n), col(0, hp))),
            pl.BlockSpec((prow, _LANES), lambda b, hp, n: (prev(b, n), col(1, hp))),
            pl.BlockSpec((rows, _LANES), lambda b, hp, n: (main(b, n), col(1, hp))),
            pl.BlockSpec((prow, _LANES), lambda b, hp, n: (prev(b, n), col(2, hp))),
            pl.BlockSpec((rows, _LANES), lambda b, hp, n: (main(b, n), col(2, hp))),
            pl.BlockSpec((rows, _LANES), lambda b, hp, n: (main(b, n), 0)),
            pl.BlockSpec((rows, _LANES), lambda b, hp, n: (main(b, n), 1)),
            pl.BlockSpec((rows, _LANES), lambda b, hp, n: (main(b, n), 2)),
            pl.BlockSpec((prow, _LANES), lambda b, hp, n: (prev(b, n), 0)),
            pl.BlockSpec((prow, _LANES), lambda b, hp, n: (prev(b, n), 1)),
            pl.BlockSpec((prow, _LANES), lambda b, hp, n: (prev(b, n), 2)),
            pl.BlockSpec((1, _LANES), lambda b, hp, n: (0, 0)),
            pl.BlockSpec((1, _LANES), lambda b, hp, n: (0, 0)),
        ],
        out_specs=[out_spec, out_spec],
        out_shape=[jax.ShapeDtypeStruct((t, width), F32)] * 2,
        compiler_params=_params(("parallel", "parallel", "arbitrary")),
        name=f"attn_group{group}",
    )(p, p, p, p, p, tabs, tabs, tabs, tabs, tabs, tabs, gq, gk)


def _attn_out_kernel(*refs, ngroup):
    o_refs, l_refs = refs[:ngroup], refs[ngroup:2 * ngroup]
    w_ref, x_ref, y_ref = refs[2 * ngroup:]
    ls = [r[...] for r in l_refs]
    m = functools.reduce(jnp.maximum, ls)
    es = [jnp.exp(l - m) for l in ls]
    den = functools.reduce(lambda a, b: a + b, es)
    o = functools.reduce(lambda a, b: a + b, [e * r[...] for e, r in zip(es, o_refs)]) / den
    y_ref[...] = x_ref[...] + jnp.dot(o.astype(_MXU_DTYPE), w_ref[...], preferred_element_type=F32)


def _attn_out(os_, ls_, w, x2d, *, tm):
    t, width = os_[0].shape
    d = x2d.shape[1]
    ng = len(os_)
    blk = pl.BlockSpec((tm, width), lambda i: (i, 0))
    return pl.pallas_call(
        functools.partial(_attn_out_kernel, ngroup=ng),
        grid=(t // tm,),
        in_specs=[blk] * (2 * ng) + [pl.BlockSpec((width, d), lambda i: (0, 0)),
                                     pl.BlockSpec((tm, d), lambda i: (i, 0))],
        out_specs=pl.BlockSpec((tm, d), lambda i: (i, 0)),
        out_shape=jax.ShapeDtypeStruct((t, d), F32),
        compiler_params=_params(("parallel",)),
        name="attn_out",
    )(*os_, *ls_, w, x2d)


_HG_HEADS, _HG_DK = 8, 128
_AT_GROUPS = ((128, 1), (512, 4), (2048, 16))
_AT_HEADS, _AT_HEAD_DIM = 8, 64
_RT_HEADS = 4
_N_MIXERS = 3


def _pick(n, candidates):
    for c in candidates:
        if n % c == 0:
            return c
    return n


def _hgrn_mixer(x2d, gain, w_in, hg_lb, o_gain, w_out, *, batch, seq, layer, blk):
    t = x2d.shape[0]
    p = _norm_proj(x2d, gain, w_in, tm=_pick(t, (512, 256, 128)), tn=_pick(w_in.shape[1], (1024, 512)))
    o = _hgrn_scan(p, hg_lb, batch=batch, seq=seq, heads=_HG_HEADS, dk=_HG_DK, layer=layer, blk=min(blk, seq))
    return _gated_out(o, p, 3, o_gain, w_out, x2d, dv=_HG_DK, tm=_pick(t, (256, 128)))


def _retention_mixer(x2d, gain, w_in, o_gain, w_out, cos, sin, *, batch, seq, blk):
    t, d = x2d.shape
    dk = d // _RT_HEADS
    dv = 2 * dk
    assert dk == 2 * _LANES
    p = _norm_proj(x2d, gain, w_in, tm=_pick(t, (512, 256, 128)), tn=_pick(w_in.shape[1], (1536, 1024, 512)))
    o = _retention_scan(p, cos, sin, batch=batch, seq=seq, heads=_RT_HEADS, dk=dk, dv=dv, blk=min(blk, seq))
    return _gated_out(o, p, 2, o_gain, w_out, x2d, dv=dv, tm=_pick(t, (256, 128)))


def _attn_mixer(x2d, gain, w_in, q_gain, k_gain, w_out, tabs, *, batch, seq):
    t = x2d.shape[0]
    ng = len(_AT_GROUPS)
    p = _norm_proj(x2d, gain, w_in, tm=_pick(t, (512, 256, 128)), tn=_pick(w_in.shape[1], (1536, 512)))
    os_, ls_ = [], []
    for g, (window, dilation) in enumerate(_AT_GROUPS):
        gq = jnp.tile(q_gain[g], _LANES // _AT_HEAD_DIM)[None, :]
        gk = jnp.tile(k_gain[g], _LANES // _AT_HEAD_DIM)[None, :]
        o, lse = _attn_group(p, tabs, gq, gk, batch=batch, seq=seq, group=g, ngroup=ng, heads=_AT_HEADS,
                             head_dim=_AT_HEAD_DIM, window=window, dilation=dilation)
        os_.append(o)
        ls_.append(lse)
    return _attn_out(os_, ls_, w_out, x2d, tm=_pick(t, (512, 256, 128)))


def kernel(x, positions, mix_norm, ffn_norm, hg_lb, hg_w_in, hg_onorm, hg_w_out, at_w_in, at_qnorm, at_knorm,
           at_w_out, rt_w_in, rt_onorm, rt_w_out, pk_w_q, pk_keys, pk_u, pk_v):
    b, s, d = x.shape
    depth = mix_norm.shape[0]
    x2d = x.reshape(b * s, d)
    pos_b = jnp.broadcast_to(positions.reshape(b * s, 1).astype(F32), (b * s, _LANES))
    rt_dk = d // _RT_HEADS
    cos_r, sin_r, tabs_a = _rope_tables(pos_b, rt_half=rt_dk // 2, at_half=_AT_HEAD_DIM // 2, tm=512)
    ia = ib = ic = 0
    for layer in range(depth):
        kind = layer % _N_MIXERS
        gain = mix_norm[layer][None, :]
        if kind == 0:
            x2d = _hgrn_mixer(x2d, gain, hg_w_in[ia].astype(_MXU_DTYPE), hg_lb, hg_onorm[ia][None, :],
                              hg_w_out[ia].astype(_MXU_DTYPE), batch=b, seq=s, layer=layer, blk=256)
            ia += 1
        elif kind == 1:
            x2d = _attn_mixer(x2d, gain, at_w_in[ib].astype(_MXU_DTYPE), at_qnorm[ib], at_knorm[ib],
                              at_w_out[ib].astype(_MXU_DTYPE), tabs_a, batch=b, seq=s)
            ib += 1
        else:
            x2d = _retention_mixer(x2d, gain, rt_w_in[ic].astype(_MXU_DTYPE), rt_onorm[ic][None, :],
                                   rt_w_out[ic].astype(_MXU_DTYPE), cos_r, sin_r, batch=b, seq=s, blk=256)
            ic += 1
        keys = pk_keys[layer].reshape(-1, pk_keys.shape[-2], pk_keys.shape[-1]).astype(_MXU_DTYPE)
        x2d = _peer_layer(x2d, ffn_norm[layer][None, :], pk_w_q[layer].astype(_MXU_DTYPE), keys,
                          pk_u[layer].astype(_MXU_DTYPE), pk_v[layer].T.astype(_MXU_DTYPE),
                          tb_route=512, tb_dense=512, a_t=8)
    return x2d.reshape(b, s, d)
```

```python
import functools
import math

import jax
import jax.numpy as jnp
from jax import lax
from jax.experimental import pallas as pl
from jax.experimental.pallas import tpu as pltpu

F32 = jnp.float32
BF16 = jnp.bfloat16
_MXU_DTYPE = BF16
_EPS = 1e-6
_LANES = 128
_SUBLANES = 8
_VMEM_LIMIT = 56 * 1024 * 1024

_NT = (((1,), (1,)), ((), ()))


def _params(sem, flags=None):
    return pltpu.CompilerParams(dimension_semantics=sem, vmem_limit_bytes=_VMEM_LIMIT, flags=flags)


def _rms(x, gain):
    return x * lax.rsqrt(jnp.mean(x * x, axis=-1, keepdims=True) + _EPS) * gain


def _merge_desc(c):
    c = list(c)
    n = len(c)
    j = n // 2
    while j >= 1:
        for i in range(n):
            l = i ^ j
            if l > i:
                hi = jnp.maximum(c[i], c[l])
                lo = jnp.minimum(c[i], c[l])
                c[i], c[l] = hi, lo
        j //= 2
    return c


def _sort_desc(xs):
    xs = list(xs)
    n = len(xs)
    k = 2
    while k <= n:
        j = k // 2
        while j >= 1:
            for i in range(n):
                l = i ^ j
                if l > i:
                    hi = jnp.maximum(xs[i], xs[l])
                    lo = jnp.minimum(xs[i], xs[l])
                    if (i & k) == 0:
                        xs[i], xs[l] = hi, lo
                    else:
                        xs[i], xs[l] = lo, hi
            j //= 2
        k *= 2
    return xs


def _top_of_two(a, b):
    n = len(a)
    return _merge_desc([jnp.maximum(a[i], b[n - 1 - i]) for i in range(n)])


_PK_TOPK = 16
_PK_PAIRS = [(i, j) for i in range(_PK_TOPK) for j in range(_PK_TOPK) if (i + 1) * (j + 1) <= _PK_TOPK]


_NEG = -3.0e38


def _top16_sums(a_top, b_top):
    cand = [a_top[i] + b_top[j] for (i, j) in _PK_PAIRS]
    pad = jnp.full_like(cand[0], _NEG)
    cand = cand + [pad] * (64 - len(cand))
    groups = [_sort_desc(cand[16 * g:16 * (g + 1)]) for g in range(4)]
    return _top_of_two(_top_of_two(groups[0], groups[1]), _top_of_two(groups[2], groups[3]))


def _gate_word(x, dtype):
    if dtype == F32:
        return x
    u = lax.bitcast_convert_type(x.astype(BF16).astype(F32), jnp.uint32)
    return u | (u >> 16)


def _gate_rows(ref, hd, a, lanes, wdt):
    row = jnp.broadcast_to(ref[hd, pl.ds(a, 1), lanes], (_SUBLANES, _LANES))
    return row if wdt == F32 else pltpu.bitcast(row, wdt)


def _peer_route_kernel(x_ref, g_ref, wq_ref, keys_ref, h_ref, ea_ref, ca_ref, eb_ref, rb_ref,
                       s_scr, top_scr, btop_scr, *, heads, nkeys):
    tb = x_ref.shape[0]
    hf = _rms(x_ref[...], g_ref[...])
    hb = hf.astype(_MXU_DTYPE)
    h_ref[...] = hf.T.astype(_MXU_DTYPE)
    q = jnp.dot(hb, wq_ref[...], preferred_element_type=F32)
    for hc in range(2 * heads):
        qs = q[:, hc * nkeys:(hc + 1) * nkeys].astype(_MXU_DTYPE)
        s_scr[hc] = lax.dot_general(keys_ref[hc], qs, _NT, preferred_element_type=F32)

    nrow = nkeys // _SUBLANES
    bcast = lambda row: jnp.broadcast_to(row, (_SUBLANES, _LANES))

    def group(gi, carry):
        lanes = pl.ds(pl.multiple_of(gi * _LANES, _LANES), _LANES)
        for hc in range(2 * heads):
            hd, c = divmod(hc, 2)
            rows = [s_scr[hc, pl.ds(_SUBLANES * m, _SUBLANES), lanes] for m in range(nrow)]
            srt = _sort_desc(rows)
            for shift in (4, 2, 1):
                srt = _top_of_two(srt, [pltpu.roll(r, shift, 0) for r in srt])
            for i in range(_PK_TOPK):
                top_scr[c, i, pl.ds(hd, 1), :] = srt[i][0:1, :]
            if c == 1:
                for i in range(_PK_TOPK):
                    btop_scr[hd, i] = srt[i]
                ebs, rbs = [], []
                for m in range(nrow):
                    ebs.append(jnp.exp(rows[m] - srt[0]))
                    rank = jnp.full_like(rows[m], float(_PK_TOPK))
                    for i in reversed(range(_PK_TOPK)):
                        rank = jnp.where(srt[i] <= rows[m], float(i), rank)
                    rbs.append(rank)
                for m in range(0, nrow, 2):
                    sl = pl.ds(_SUBLANES * m, 2 * _SUBLANES)
                    eb_ref[hd, sl, lanes] = jnp.concatenate(ebs[m:m + 2], axis=0).astype(eb_ref.dtype)
                    rb_ref[hd, sl, lanes] = jnp.concatenate(rbs[m:m + 2], axis=0).astype(rb_ref.dtype)
        a_s = [top_scr[0, i] for i in range(_PK_TOPK)]
        b_s = [top_scr[1, i] for i in range(_PK_TOPK)]
        top = _top16_sums(a_s, b_s)
        z = jnp.exp(top[0] - top[0])
        for t in top[1:]:
            z = z + jnp.exp(t - top[0])
        rz = 1.0 / z
        theta = top[_PK_TOPK - 1]
        for hd in range(heads):
            amax = bcast(a_s[0][hd:hd + 1, :])
            rz_h = bcast(rz[hd:hd + 1, :])
            th_h = bcast(theta[hd:hd + 1, :])
            for m in range(nrow):
                sl = pl.ds(_SUBLANES * m, _SUBLANES)
                sa = s_scr[2 * hd, sl, lanes]
                ea_ref[hd, sl, lanes] = _gate_word(jnp.exp(sa - amax) * rz_h, ea_ref.dtype)
                cnt = jnp.full_like(sa, float(_PK_TOPK))
                for j in reversed(range(_PK_TOPK)):
                    cnt = jnp.where(sa + btop_scr[hd, j] < th_h, float(j), cnt)
                ca_ref[hd, sl, lanes] = _gate_word(cnt, ca_ref.dtype)
        return carry

    lax.fori_loop(0, tb // _LANES, group, 0)


def _peer_route(x2d, gain, wq, keys, *, tb):
    t, d = x2d.shape
    hc, nkeys, dk = keys.shape
    heads = hc // 2
    kern = functools.partial(_peer_route_kernel, heads=heads, nkeys=nkeys)
    tab = pl.BlockSpec((heads, nkeys, tb), lambda i: (0, 0, i))
    word = F32 if _MXU_DTYPE == F32 else jnp.uint32
    return pl.pallas_call(
        kern,
        grid=(t // tb,),
        in_specs=[
            pl.BlockSpec((tb, d), lambda i: (i, 0)),
            pl.BlockSpec((1, d), lambda i: (0, 0)),
            pl.BlockSpec(wq.shape, lambda i: (0, 0)),
            pl.BlockSpec(keys.shape, lambda i: (0, 0, 0)),
        ],
        out_specs=[pl.BlockSpec((d, tb), lambda i: (0, i)), tab, tab, tab, tab],
        out_shape=[
            jax.ShapeDtypeStruct((d, t), _MXU_DTYPE),
            jax.ShapeDtypeStruct((heads, nkeys, t), word),
            jax.ShapeDtypeStruct((heads, nkeys, t), word),
            jax.ShapeDtypeStruct((heads, nkeys, t), _MXU_DTYPE),
            jax.ShapeDtypeStruct((heads, nkeys, t), _MXU_DTYPE),
        ],
        scratch_shapes=[
            pltpu.VMEM((hc, nkeys, tb), F32),
            pltpu.VMEM((2, _PK_TOPK, heads, _LANES), F32),
            pltpu.VMEM((heads, _PK_TOPK, _SUBLANES, _LANES), F32),
        ],
        compiler_params=_params(("parallel",)),
        name="peer_route",
    )(x2d, gain, wq, keys)


def _gelu(x):
    return 0.5 * x * (1.0 + lax.erf(x * (1.0 / math.sqrt(2.0))))


def _peer_dense_kernel(h_ref, u_ref, vt_ref, ea_ref, ca_ref, eb_ref, rb_ref, x_ref, o_ref, acc_ref,
                       *, heads, nkeys):
    e = pl.program_id(1)
    tb = h_ref.shape[1]
    a_t = ea_ref.shape[1]
    wdt = vt_ref.dtype
    pk = _SUBLANES * (4 // jnp.dtype(wdt).itemsize)
    zero = jnp.zeros((pk, _LANES), wdt)
    mc = _pick(tb, (256,))

    @pl.when(e == 0)
    def _():
        acc_ref[...] = jnp.zeros_like(acc_ref)

    for c in range(tb // mc):
        toks = slice(c * mc, (c + 1) * mc)
        mr = 2 * nkeys
        hu = jnp.concatenate(
            [jnp.dot(u_ref[r * mr:(r + 1) * mr, :], h_ref[:, toks], preferred_element_type=F32)
             for r in range(a_t * nkeys // mr)], axis=0)
        cols = [[] for _ in range(mc // _LANES)]
        for a in range(a_t):
            for tc in range(mc // _LANES):
                lanes = slice(c * mc + tc * _LANES, c * mc + (tc + 1) * _LANES)
                pieces = cols[tc]
                for bc in range(nkeys // pk):
                    rows = slice(bc * pk, (bc + 1) * pk)
                    acc = None
                    for hd in range(heads):
                        ea_b = _gate_rows(ea_ref, hd, a, lanes, wdt)
                        ca_b = _gate_rows(ca_ref, hd, a, lanes, wdt)
                        sel = lax.clamp(zero, ca_b - rb_ref[hd, rows, lanes], eb_ref[hd, rows, lanes])
                        term = ea_b * sel
                        acc = term if acc is None else acc + term
                    r0 = a * nkeys + bc * pk
                    act = _gelu(hu[r0:r0 + pk, tc * _LANES:(tc + 1) * _LANES].astype(wdt))
                    pieces.append(acc * act)
        w = jnp.concatenate([jnp.concatenate(p, axis=0) for p in cols], axis=1)
        acc_ref[:, toks] += jnp.dot(vt_ref[...], w, preferred_element_type=F32)

    @pl.when(e == pl.num_programs(1) - 1)
    def _():
        o_ref[...] = x_ref[...] + acc_ref[...].T


def _peer_dense(h, u, vt, ea, ca, eb, rb, x2d, *, tb, a_t):
    t, d = x2d.shape
    heads, nkeys, _ = ea.shape
    e_t = a_t * nkeys
    n_e = u.shape[0] // e_t
    kern = functools.partial(_peer_dense_kernel, heads=heads, nkeys=nkeys)
    gate_a = pl.BlockSpec((heads, a_t, tb), lambda i, e: (0, e, i))
    gate_b = pl.BlockSpec((heads, nkeys, tb), lambda i, e: (0, 0, i))
    return pl.pallas_call(
        kern,
        grid=(t // tb, n_e),
        in_specs=[
            pl.BlockSpec((d, tb), lambda i, e: (0, i)),
            pl.BlockSpec((e_t, d), lambda i, e: (e, 0)),
            pl.BlockSpec((d, e_t), lambda i, e: (0, e)),
            gate_a, gate_a, gate_b, gate_b,
            pl.BlockSpec((tb, d), lambda i, e: (i, 0)),
        ],
        out_specs=pl.BlockSpec((tb, d), lambda i, e: (i, 0)),
        out_shape=jax.ShapeDtypeStruct((t, d), F32),
        scratch_shapes=[pltpu.VMEM((d, tb), F32)],
        compiler_params=_params(("parallel", "arbitrary")),
        name="peer_dense",
    )(h, u, vt, ea, ca, eb, rb, x2d)


def _peer_layer(x2d, gain, wq, keys, u, vt, *, tb_route, tb_dense, a_t):
    h, ea, ca, eb, rb = _peer_route(x2d, gain, wq, keys, tb=tb_route)
    return _peer_dense(h, u, vt, ea, ca, eb, rb, x2d, tb=tb_dense, a_t=a_t)


def _norm_proj_kernel(x_ref, g_ref, w_ref, o_ref, hb_ref):
    @pl.when(pl.program_id(1) == 0)
    def _():
        hb_ref[...] = _rms(x_ref[...], g_ref[...]).astype(hb_ref.dtype)

    o_ref[...] = jnp.dot(hb_ref[...], w_ref[...], preferred_element_type=F32)


def _norm_proj(x2d, gain, w, *, tm, tn):
    t, d = x2d.shape
    n = w.shape[1]
    return pl.pallas_call(
        _norm_proj_kernel,
        grid=(t // tm, n // tn),
        in_specs=[
            pl.BlockSpec((tm, d), lambda i, j: (i, 0)),
            pl.BlockSpec((1, d), lambda i, j: (0, 0)),
            pl.BlockSpec((d, tn), lambda i, j: (0, j)),
        ],
        out_specs=pl.BlockSpec((tm, tn), lambda i, j: (i, j)),
        out_shape=jax.ShapeDtypeStruct((t, n), F32),
        scratch_shapes=[pltpu.VMEM((tm, d), _MXU_DTYPE)],
        compiler_params=_params(("parallel", "arbitrary")),
        name="norm_proj",
    )(x2d, gain, w)


def _gated_out_kernel(o_ref, g_ref, gain_ref, w_ref, x_ref, y_ref, a_ref, *, dv):
    vd = o_ref.shape[1]
    for hd in range(vd // dv):
        cols = slice(hd * dv, (hd + 1) * dv)
        o = o_ref[:, cols]
        g = g_ref[:, cols]
        a_ref[:, cols] = (_rms(o, gain_ref[:, cols]) * (g * jax.nn.sigmoid(g))).astype(a_ref.dtype)
    y_ref[...] = x_ref[...] + jnp.dot(a_ref[...], w_ref[...], preferred_element_type=F32)


def _gated_out(o, p, g_block, gain, w, x2d, *, dv, tm):
    t, vd = o.shape
    d = x2d.shape[1]
    return pl.pallas_call(
        functools.partial(_gated_out_kernel, dv=dv),
        grid=(t // tm,),
        in_specs=[
            pl.BlockSpec((tm, vd), lambda i: (i, 0)),
            pl.BlockSpec((tm, vd), lambda i: (i, g_block)),
            pl.BlockSpec((1, vd), lambda i: (0, 0)),
            pl.BlockSpec((vd, d), lambda i: (0, 0)),
            pl.BlockSpec((tm, d), lambda i: (i, 0)),
        ],
        out_specs=pl.BlockSpec((tm, d), lambda i: (i, 0)),
        out_shape=jax.ShapeDtypeStruct((t, d), F32),
        scratch_shapes=[pltpu.VMEM((tm, vd), _MXU_DTYPE)],
        compiler_params=_params(("parallel",)),
        name="gated_out",
    )(o, p, gain, w, x2d)


_ROPE_THETA = 10000.0


def _rope_kernel(pos_ref, cr_ref, sr_ref, ta_ref, *, rt_half, at_half):
    pos = pos_ref[...]
    lane = lax.broadcasted_iota(jnp.int32, pos.shape, 1)
    inv = jnp.exp(lane.astype(F32) * (-math.log(_ROPE_THETA) / rt_half))
    ang = pos * inv
    cr_ref[...] = jnp.cos(ang)
    sr_ref[...] = jnp.sin(ang)
    j = lane % (2 * at_half)
    inv = jnp.exp((j % at_half).astype(F32) * (-math.log(_ROPE_THETA) / at_half))
    ang = pos * inv
    sn = jnp.sin(ang)
    ta_ref[:, :_LANES] = jnp.cos(ang)
    ta_ref[:, _LANES:2 * _LANES] = jnp.where(j < at_half, -sn, 0.0)
    ta_ref[:, 2 * _LANES:] = jnp.where(j < at_half, 0.0, sn)


def _rope_tables(pos_b, *, rt_half, at_half, tm):
    t = pos_b.shape[0]
    spec = pl.BlockSpec((tm, _LANES), lambda i: (i, 0))
    return pl.pallas_call(
        functools.partial(_rope_kernel, rt_half=rt_half, at_half=at_half),
        grid=(t // tm,),
        in_specs=[spec],
        out_specs=[spec, spec, pl.BlockSpec((tm, 3 * _LANES), lambda i: (i, 0))],
        out_shape=[jax.ShapeDtypeStruct((t, _LANES), F32)] * 2 + [jax.ShapeDtypeStruct((t, 3 * _LANES), F32)],
        compiler_params=_params(("parallel",)),
        name="rope_tables",
    )(pos_b)


_HG_CHUNK = 64
_HG_SUB = 16
_TINY = 1e-30
_TN = (((0,), (0,)), ((), ()))


def _split_dot(a, b_f32):
    hi = b_f32.astype(BF16)
    lo = (b_f32 - hi.astype(F32)).astype(BF16)
    return (jnp.dot(a, hi, preferred_element_type=F32) + jnp.dot(a, lo, preferred_element_type=F32))


def _hgrn_kernel(q_ref, f_ref, i_ref, lb_ref, o_ref, st_ref, *, layer, heads):
    c, sc = _HG_CHUNK, _HG_SUB
    nchunk = q_ref.shape[0] // c
    dk = q_ref.shape[1] // heads

    @pl.when(pl.program_id(1) == 0)
    def _():
        st_ref[...] = jnp.zeros_like(st_ref)

    lbs = lb_ref[...]
    e = jnp.exp(lbs - jnp.max(lbs, axis=0, keepdims=True))
    soft = e / jnp.sum(e, axis=0, keepdims=True)
    lb_all = jnp.zeros_like(soft[0:1])
    for l in range(1, layer + 1):
        lb_all = lb_all + soft[l:l + 1]

    r_i = lax.broadcasted_iota(jnp.int32, (c, c), 0)
    c_i = lax.broadcasted_iota(jnp.int32, (c, c), 1)
    tri = (r_i >= c_i).astype(BF16)
    sub_t = lax.broadcasted_iota(jnp.int32, (sc, 1), 0)

    def head_chunk(rows, hd):
        cols = slice(hd * dk, (hd + 1) * dk)
        lb = lb_all[:, cols]
        qr = q_ref[rows, cols]
        fr = f_ref[rows, cols]
        v = i_ref[rows, cols]
        q = qr * jax.nn.sigmoid(qr)
        forget = lb + (1.0 - lb) * jax.nn.sigmoid(fr)
        logf = jnp.log(jnp.maximum(forget, _TINY))
        k = (1.0 - lb) * jax.nn.sigmoid(-fr)
        if _MXU_DTYPE == BF16:
            b = _split_dot(tri, logf)
        else:
            b = jnp.dot(tri.astype(F32), logf, preferred_element_type=F32)
        st = st_ref[hd]
        o = lax.dot_general((q * jnp.exp(b)).astype(_MXU_DTYPE), st.astype(_MXU_DTYPE), _NT,
                            preferred_element_type=F32)
        vb = v.astype(_MXU_DTYPE)
        outs = []
        for i in range(c // sc):
            lo, hi = i * sc, (i + 1) * sc
            qi, bi = q[lo:hi], b[lo:hi]
            oi = o[lo:hi]
            if i > 0:
                ref_b = b[lo - 1:lo]
                qt = (qi * jnp.exp(bi - ref_b)).astype(_MXU_DTYPE)
                kt = (k[:lo] * jnp.exp(ref_b - b[:lo])).astype(_MXU_DTYPE)
                att = lax.dot_general(qt, kt, _NT, preferred_element_type=F32)
                oi = oi + jnp.dot(att.astype(_MXU_DTYPE), vb[:lo], preferred_element_type=F32)
            for s in range(sc):
                dec = jnp.exp(bi - bi[s:s + 1])
                col = jnp.sum(qi * k[lo + s:lo + s + 1] * dec, axis=1, keepdims=True)
                col = jnp.where(sub_t >= s, col, 0.0)
                oi = oi + col * v[lo + s:lo + s + 1]
            outs.append(oi)
        o_ref[rows, cols] = jnp.concatenate(outs, axis=0)
        b_last = b[c - 1:c]
        kd = (k * jnp.exp(b_last - b)).astype(_MXU_DTYPE)
        st_ref[hd] = jnp.exp(b_last) * st + lax.dot_general(vb, kd, _TN, preferred_element_type=F32)

    def chunk(ci, carry):
        rows = pl.ds(pl.multiple_of(ci * c, c), c)
        for hd in range(heads):
            head_chunk(rows, hd)
        return carry

    lax.fori_loop(0, nchunk, chunk, 0)


def _hgrn_scan(p, hg_lb, *, batch, seq, heads, dk, layer, blk):
    t = p.shape[0]
    nblk = seq // blk
    width = heads * dk
    row = lambda b, s: b * nblk + s
    return pl.pallas_call(
        functools.partial(_hgrn_kernel, layer=layer, heads=heads),
        grid=(batch, nblk),
        in_specs=[
            pl.BlockSpec((blk, width), lambda b, s: (row(b, s), 0)),
            pl.BlockSpec((blk, width), lambda b, s: (row(b, s), 1)),
            pl.BlockSpec((blk, width), lambda b, s: (row(b, s), 2)),
            pl.BlockSpec((hg_lb.shape[0], width), lambda b, s: (0, 0)),
        ],
        out_specs=pl.BlockSpec((blk, width), lambda b, s: (row(b, s), 0)),
        out_shape=jax.ShapeDtypeStruct((t, width), F32),
        scratch_shapes=[pltpu.VMEM((heads, dk, dk), F32)],
        compiler_params=_params(("parallel", "arbitrary")),
        name="hgrn_scan",
    )(p, p, p, hg_lb)


_RT_CHUNK = 128


def _retention_kernel(q_ref, k_ref, v_ref, cos_ref, sin_ref, o_ref, st_ref):
    c = _RT_CHUNK
    nchunk = q_ref.shape[0] // c
    dk = q_ref.shape[1]
    half = dk // 2

    @pl.when(pl.program_id(2) == 0)
    def _():
        st_ref[...] = jnp.zeros_like(st_ref)

    hv = jnp.full((1, _LANES), pl.program_id(1), jnp.int32).astype(F32)
    lg = jnp.log1p(-jnp.exp2(-5.0 - hv))
    t_i = lax.broadcasted_iota(jnp.int32, (c, c), 0)
    s_i = lax.broadcasted_iota(jnp.int32, (c, c), 1)
    rel = (t_i - s_i).astype(F32)
    decay = jnp.where(rel >= 0, jnp.exp(lg * jnp.maximum(rel, 0.0)), 0.0)
    idx = lax.broadcasted_iota(jnp.int32, (c, _LANES), 0).astype(F32)
    q_dec = jnp.exp(lg * (idx + 1.0))
    k_dec = jnp.exp(lg * (c - 1.0 - idx))
    chunk_dec = jnp.exp(lg * c)[:, :1]

    def rope(x, cos, sin):
        x1, x2 = x[:, :half], x[:, half:]
        return x1 * cos - x2 * sin, x2 * cos + x1 * sin

    def chunk(ci, carry):
        rows = pl.ds(pl.multiple_of(ci * c, c), c)
        cos, sin = cos_ref[rows, :], sin_ref[rows, :]
        q1, q2 = rope(q_ref[rows, :], cos, sin)
        k1, k2 = rope(k_ref[rows, :], cos, sin)
        scale = dk ** -0.5
        k1, k2 = k1 * scale, k2 * scale
        cat = lambda a, b: jnp.concatenate([a, b], axis=1).astype(_MXU_DTYPE)
        qb, kb = cat(q1, q2), cat(k1, k2)
        vb = v_ref[rows, :].astype(_MXU_DTYPE)
        att = lax.dot_general(qb, kb, _NT, preferred_element_type=F32) * decay
        o = jnp.dot(att.astype(_MXU_DTYPE), vb, preferred_element_type=F32)
        st = st_ref[...]
        o = o + jnp.dot(cat(q1 * q_dec, q2 * q_dec), st.astype(_MXU_DTYPE), preferred_element_type=F32)
        o_ref[rows, :] = o
        st_ref[...] = chunk_dec * st + lax.dot_general(cat(k1 * k_dec, k2 * k_dec), vb, _TN,
                                                       preferred_element_type=F32)
        return carry

    lax.fori_loop(0, nchunk, chunk, 0)


def _retention_scan(p, cos, sin, *, batch, seq, heads, dk, dv, blk):
    t = p.shape[0]
    nblk = seq // blk
    row = lambda b, h, s: b * nblk + s
    v_off = 2 * heads * dk // dv
    return pl.pallas_call(
        _retention_kernel,
        grid=(batch, heads, nblk),
        in_specs=[
            pl.BlockSpec((blk, dk), lambda b, h, s: (row(b, h, s), h)),
            pl.BlockSpec((blk, dk), lambda b, h, s: (row(b, h, s), heads + h)),
            pl.BlockSpec((blk, dv), lambda b, h, s: (row(b, h, s), v_off + h)),
            pl.BlockSpec((blk, _LANES), lambda b, h, s: (row(b, h, s), 0)),
            pl.BlockSpec((blk, _LANES), lambda b, h, s: (row(b, h, s), 0)),
        ],
        out_specs=pl.BlockSpec((blk, dv), lambda b, h, s: (row(b, h, s), h)),
        out_shape=jax.ShapeDtypeStruct((t, heads * dv), F32),
        scratch_shapes=[pltpu.VMEM((dk, dv), F32)],
        compiler_params=_params(("parallel", "parallel", "arbitrary")),
        name="retention_scan",
    )(p, p, p, cos, sin)


_AT_BLOCK = 128
_MASK_VALUE = -1e30


def _split_dot_lhs(a_f32, b):
    hi = a_f32.astype(BF16)
    lo = (a_f32 - hi.astype(F32)).astype(BF16)
    return (jnp.dot(hi, b, preferred_element_type=F32) + jnp.dot(lo, b, preferred_element_type=F32))


def _attn_kernel(q_ref, kp_ref, k_ref, vp_ref, v_ref, t0_ref, t1_ref, t2_ref, p0_ref, p1_ref, p2_ref,
                 gq_ref, gk_ref, o_ref, l_ref, *, head_dim, span, dil, nsub):
    blk = _AT_BLOCK
    half = head_dim // 2
    lane = lax.broadcasted_iota(jnp.int32, (1, _LANES), 1)
    qi = lax.broadcasted_iota(jnp.int32, (blk, 2 * blk), 0) + blk
    ki = lax.broadcasted_iota(jnp.int32, (blk, 2 * blk), 1)
    band = (ki <= qi) & (ki >= qi - span)
    first_lo = jnp.where(pl.program_id(2) == 0, blk, 0)
    band_first = (ki <= qi) & (ki >= jnp.maximum(qi - span, first_lo))
    scale = head_dim ** -0.5
    r_i = lax.broadcasted_iota(jnp.int32, (_LANES, _LANES), 0) // head_dim
    c_i = lax.broadcasted_iota(jnp.int32, (_LANES, _LANES), 1) // head_dim
    seg = jnp.where(r_i == c_i, 1.0 / head_dim, 0.0)
    gq, gk = gq_ref[...], gk_ref[...]

    nh = _LANES // head_dim
    gain_qk = jnp.concatenate([jnp.broadcast_to(gq, (blk, _LANES)), jnp.broadcast_to(gk, (blk, _LANES))], axis=0)
    heads_of = lane // head_dim
    segb = seg.astype(_MXU_DTYPE)

    def prep(x, gain, tab):
        msq = jnp.dot((x * x).astype(_MXU_DTYPE), segb, preferred_element_type=F32)
        y = x * lax.rsqrt(msq + _EPS) * gain
        cos, s1, s2 = tab
        y = y * cos + pltpu.roll(y, _LANES - half, 1) * s1 + pltpu.roll(y, half, 1) * s2
        return y.astype(_MXU_DTYPE)

    def residue(r, carry):
        def take(ref, j):
            if dil == 1:
                return ref[j * blk:(j + 1) * blk, :]
            return ref[pl.ds(r + j * blk * dil, blk, stride=dil), :]

        kprev = prep(take(kp_ref, 0), gk, [take(t, 0) for t in (p0_ref, p1_ref, p2_ref)])
        vprev = take(vp_ref, 0).astype(_MXU_DTYPE)
        for j in range(nsub):
            tab = [take(t, j) for t in (t0_ref, t1_ref, t2_ref)]
            qk = prep(jnp.concatenate([take(q_ref, j), take(k_ref, j)], axis=0), gain_qk,
                      [jnp.concatenate([t, t], axis=0) for t in tab])
            q, kcur = qk[:blk], qk[blk:]
            vcur = take(v_ref, j).astype(_MXU_DTYPE)
            kk = jnp.concatenate([kprev, kcur], axis=0)
            vv = jnp.concatenate([vprev, vcur], axis=0)
            mask = band_first if j == 0 else band
            qh = jnp.concatenate([jnp.where(heads_of == hd, q, jnp.zeros_like(q)) for hd in range(nh)], axis=0)
            s = lax.dot_general(qh, kk, _NT, preferred_element_type=F32) * scale
            s = jnp.where(jnp.concatenate([mask] * nh, axis=0), s, _MASK_VALUE)
            m = jnp.max(s, axis=-1, keepdims=True)
            pr = jnp.exp(s - m)
            l = jnp.sum(pr, axis=-1, keepdims=True)
            o = jnp.dot(pr.astype(_MXU_DTYPE), vv, preferred_element_type=F32) / l
            lse = m + jnp.log(l)
            o_pair = o[:blk]
            l_pair = jnp.broadcast_to(lse[:blk], o_pair.shape)
            for hd in range(1, nh):
                o_pair = jnp.where(heads_of == hd, o[hd * blk:(hd + 1) * blk], o_pair)
                l_pair = jnp.where(heads_of == hd, lse[hd * blk:(hd + 1) * blk], l_pair)
            if dil == 1:
                o_ref[j * blk:(j + 1) * blk, :] = o_pair
                l_ref[j * blk:(j + 1) * blk, :] = l_pair
            else:
                o_ref[pl.ds(r + j * blk * dil, blk, stride=dil), :] = o_pair
                l_ref[pl.ds(r + j * blk * dil, blk, stride=dil), :] = l_pair
            kprev, vprev = kcur, vcur
        return carry

    if dil == 1:
        residue(0, 0)
    else:
        def two(i, carry):
            residue(2 * i, carry)
            return residue(2 * i + 1, carry)

        lax.fori_loop(0, dil // 2, two, 0)


def _attn_group(p, tabs, gq, gk, *, batch, seq, group, ngroup, heads, head_dim, window, dilation):
    t = p.shape[0]
    d = dilation
    width = heads * head_dim
    assert seq % (d * _AT_BLOCK) == 0 and window // d <= _AT_BLOCK
    nb = seq // d // _AT_BLOCK
    nsub = max(1, min(4, nb, 2048 // (_AT_BLOCK * d)))
    nstep = nb // nsub
    rows = nsub * _AT_BLOCK * d
    prow = _AT_BLOCK * d
    pairs = width // _LANES
    col = lambda kind, hp: (kind * ngroup + group) * pairs + hp
    main = lambda b, n: b * nstep + n
    prev = lambda b, n: jnp.maximum((b * nstep + n) * nsub - 1, 0)
    out_spec = pl.BlockSpec((rows, _LANES), lambda b, hp, n: (main(b, n), hp))
    return pl.pallas_call(
        functools.partial(_attn_kernel, head_dim=head_dim, span=window // d, dil=d, nsub=nsub),
        grid=(batch, pairs, nstep),
        in_specs=[
            pl.BlockSpec((rows, _LANES), lambda b, hp, n: (main(b, n), col(0, hp))),
            pl.BlockSpec((prow, _LANES), lambda b, hp, n: (prev(b, n), col(1, hp))),
            pl.BlockSpec((rows, _LANES), lambda b, hp, n: (main(b, n), col(1, hp))),
            pl.BlockSpec((prow, _LANES), lambda b, hp, n: (prev(b, n), col(2, hp))),
            pl.BlockSpec((rows, _LANES), lambda b, hp, n: (main(b, n), col(2, hp))),
            pl.BlockSpec((rows, _LANES), lambda b, hp, n: (main(b, n), 0)),
            pl.BlockSpec((rows, _LANES), lambda b, hp, n: (main(b, n), 1)),
            pl.BlockSpec((rows, _LANES), lambda b, hp, n: (main(b, n), 2)),
            pl.BlockSpec((prow, _LANES), lambda b, hp, n: (prev(b, n), 0)),
            pl.BlockSpec((prow, _LANES), lambda b, hp, n: (prev(b, n), 1)),
            pl.BlockSpec((prow, _LANES), lambda b, hp, n: (prev(b, n), 2)),
            pl.BlockSpec((1, _LANES), lambda b, hp, n: (0, 0)),
            pl.BlockSpec((1, _LANES), lambda b, hp, n: (0, 0)),
        ],
        out_specs=[out_spec, out_spec],
        out_shape=[jax.ShapeDtypeStruct((t, width), F32)] * 2,
        compiler_params=_params(("parallel", "parallel", "arbitrary")),
        name=f"attn_group{group}",
    )(p, p, p, p, p, tabs, tabs, tabs, tabs, tabs, tabs, gq, gk)


def _attn_out_kernel(*refs, ngroup):
    o_refs, l_refs = refs[:ngroup], refs[ngroup:2 * ngroup]
    w_ref, x_ref, y_ref = refs[2 * ngroup:]
    ls = [r[...] for r in l_refs]
    m = functools.reduce(jnp.maximum, ls)
    es = [jnp.exp(l - m) for l in ls]
    den = functools.reduce(lambda a, b: a + b, es)
    o = functools.reduce(lambda a, b: a + b, [e * r[...] for e, r in zip(es, o_refs)]) / den
    y_ref[...] = x_ref[...] + jnp.dot(o.astype(_MXU_DTYPE), w_ref[...], preferred_element_type=F32)


def _attn_out(os_, ls_, w, x2d, *, tm):
    t, width = os_[0].shape
    d = x2d.shape[1]
    ng = len(os_)
    blk = pl.BlockSpec((tm, width), lambda i: (i, 0))
    return pl.pallas_call(
        functools.partial(_attn_out_kernel, ngroup=ng),
        grid=(t // tm,),
        in_specs=[blk] * (2 * ng) + [pl.BlockSpec((width, d), lambda i: (0, 0)),
                                     pl.BlockSpec((tm, d), lambda i: (i, 0))],
        out_specs=pl.BlockSpec((tm, d), lambda i: (i, 0)),
        out_shape=jax.ShapeDtypeStruct((t, d), F32),
        compiler_params=_params(("parallel",)),
        name="attn_out",
    )(*os_, *ls_, w, x2d)


_HG_HEADS, _HG_DK = 8, 128
_AT_GROUPS = ((128, 1), (512, 4), (2048, 16))
_AT_HEADS, _AT_HEAD_DIM = 8, 64
_RT_HEADS = 4
_N_MIXERS = 3


def _pick(n, candidates):
    for c in candidates:
        if n % c == 0:
            return c
    return n


def _hgrn_mixer(x2d, gain, w_in, hg_lb, o_gain, w_out, *, batch, seq, layer, blk):
    t = x2d.shape[0]
    p = _norm_proj(x2d, gain, w_in, tm=_pick(t, (512, 256, 128)), tn=_pick(w_in.shape[1], (1024, 512)))
    o = _hgrn_scan(p, hg_lb, batch=batch, seq=seq, heads=_HG_HEADS, dk=_HG_DK, layer=layer, blk=min(blk, seq))
    return _gated_out(o, p, 3, o_gain, w_out, x2d, dv=_HG_DK, tm=_pick(t, (256, 128)))


def _retention_mixer(x2d, gain, w_in, o_gain, w_out, cos, sin, *, batch, seq, blk):
    t, d = x2d.shape
    dk = d // _RT_HEADS
    dv = 2 * dk
    assert dk == 2 * _LANES
    p = _norm_proj(x2d, gain, w_in, tm=_pick(t, (512, 256, 128)), tn=_pick(w_in.shape[1], (1536, 1024, 512)))
    o = _retention_scan(p, cos, sin, batch=batch, seq=seq, heads=_RT_HEADS, dk=dk, dv=dv, blk=min(blk, seq))
    return _gated_out(o, p, 2, o_gain, w_out, x2d, dv=dv, tm=_pick(t, (256, 128)))


def _attn_mixer(x2d, gain, w_in, q_gain, k_gain, w_out, tabs, *, batch, seq):
    t = x2d.shape[0]
    ng = len(_AT_GROUPS)
    p = _norm_proj(x2d, gain, w_in, tm=_pick(t, (512, 256, 128)), tn=_pick(w_in.shape[1], (1536, 512)))
    os_, ls_ = [], []
    for g, (window, dilation) in enumerate(_AT_GROUPS):
        gq = jnp.tile(q_gain[g], _LANES // _AT_HEAD_DIM)[None, :]
        gk = jnp.tile(k_gain[g], _LANES // _AT_HEAD_DIM)[None, :]
        o, lse = _attn_group(p, tabs, gq, gk, batch=batch, seq=seq, group=g, ngroup=ng, heads=_AT_HEADS,
                             head_dim=_AT_HEAD_DIM, window=window, dilation=dilation)
        os_.append(o)
        ls_.append(lse)
    return _attn_out(os_, ls_, w_out, x2d, tm=_pick(t, (512, 256, 128)))


def kernel(x, positions, mix_norm, ffn_norm, hg_lb, hg_w_in, hg_onorm, hg_w_out, at_w_in, at_qnorm, at_knorm,
           at_w_out, rt_w_in, rt_onorm, rt_w_out, pk_w_q, pk_keys, pk_u, pk_v):
    b, s, d = x.shape
    depth = mix_norm.shape[0]
    x2d = x.reshape(b * s, d)
    pos_b = jnp.broadcast_to(positions.reshape(b * s, 1).astype(F32), (b * s, _LANES))
    rt_dk = d // _RT_HEADS
    cos_r, sin_r, tabs_a = _rope_tables(pos_b, rt_half=rt_dk // 2, at_half=_AT_HEAD_DIM // 2, tm=512)
    ia = ib = ic = 0
    for layer in range(depth):
        kind = layer % _N_MIXERS
        gain = mix_norm[layer][None, :]
        if kind == 0:
            x2d = _hgrn_mixer(x2d, gain, hg_w_in[ia].astype(_MXU_DTYPE), hg_lb, hg_onorm[ia][None, :],
                              hg_w_out[ia].astype(_MXU_DTYPE), batch=b, seq=s, layer=layer, blk=256)
            ia += 1
        elif kind == 1:
            x2d = _attn_mixer(x2d, gain, at_w_in[ib].astype(_MXU_DTYPE), at_qnorm[ib], at_knorm[ib],
                              at_w_out[ib].astype(_MXU_DTYPE), tabs_a, batch=b, seq=s)
            ib += 1
        else:
            x2d = _retention_mixer(x2d, gain, rt_w_in[ic].astype(_MXU_DTYPE), rt_onorm[ic][None, :],
                                   rt_w_out[ic].astype(_MXU_DTYPE), cos_r, sin_r, batch=b, seq=s, blk=256)
            ic += 1
        keys = pk_keys[layer].reshape(-1, pk_keys.shape[-2], pk_keys.shape[-1]).astype(_MXU_DTYPE)
        x2d = _peer_layer(x2d, ffn_norm[layer][None, :], pk_w_q[layer].astype(_MXU_DTYPE), keys,
                          pk_u[layer].astype(_MXU_DTYPE), pk_v[layer].T.astype(_MXU_DTYPE),
                          tb_route=512, tb_dense=512, a_t=8)
    return x2d.reshape(b, s, d)
```

```python
import functools
import math

import jax
import jax.numpy as jnp
from jax import lax
from jax.experimental import pallas as pl
from jax.experimental.pallas import tpu as pltpu

F32 = jnp.float32
BF16 = jnp.bfloat16
_MXU_DTYPE = BF16
_EPS = 1e-6
_LANES = 128
_SUBLANES = 8
_VMEM_LIMIT = 56 * 1024 * 1024

_NT = (((1,), (1,)), ((), ()))


def _params(sem, flags=None):
    return pltpu.CompilerParams(dimension_semantics=sem, vmem_limit_bytes=_VMEM_LIMIT, flags=flags)


def _rms(x, gain):
    return x * lax.rsqrt(jnp.mean(x * x, axis=-1, keepdims=True) + _EPS) * gain


def _merge_desc(c):
    c = list(c)
    n = len(c)
    j = n // 2
    while j >= 1:
        for i in range(n):
            l = i ^ j
            if l > i:
                hi = jnp.maximum(c[i], c[l])
                lo = jnp.minimum(c[i], c[l])
                c[i], c[l] = hi, lo
        j //= 2
    return c


def _sort_desc(xs):
    xs = list(xs)
    n = len(xs)
    k = 2
    while k <= n:
        j = k // 2
        while j >= 1:
            for i in range(n):
                l = i ^ j
                if l > i:
                    hi = jnp.maximum(xs[i], xs[l])
                    lo = jnp.minimum(xs[i], xs[l])
                    if (i & k) == 0:
                        xs[i], xs[l] = hi, lo
                    else:
                        xs[i], xs[l] = lo, hi
            j //= 2
        k *= 2
    return xs


def _top_of_two(a, b):
    n = len(a)
    return _merge_desc([jnp.maximum(a[i], b[n - 1 - i]) for i in range(n)])


_PK_TOPK = 16
_PK_PAIRS = [(i, j) for i in range(_PK_TOPK) for j in range(_PK_TOPK) if (i + 1) * (j + 1) <= _PK_TOPK]


_NEG = -3.0e38


def _top16_sums(a_top, b_top):
    cand = [a_top[i] + b_top[j] for (i, j) in _PK_PAIRS]
    pad = jnp.full_like(cand[0], _NEG)
    cand = cand + [pad] * (64 - len(cand))
    groups = [_sort_desc(cand[16 * g:16 * (g + 1)]) for g in range(4)]
    return _top_of_two(_top_of_two(groups[0], groups[1]), _top_of_two(groups[2], groups[3]))


def _gate_word(x, dtype):
    if dtype == F32:
        return x
    u = lax.bitcast_convert_type(x.astype(BF16).astype(F32), jnp.uint32)
    return u | (u >> 16)


def _gate_rows(ref, hd, a, lanes, wdt):
    row = jnp.broadcast_to(ref[hd, pl.ds(a, 1), lanes], (_SUBLANES, _LANES))
    return row if wdt == F32 else pltpu.bitcast(row, wdt)


def _peer_route_kernel(x_ref, g_ref, wq_ref, keys_ref, h_ref, ea_ref, ca_ref, eb_ref, rb_ref,
                       s_scr, top_scr, btop_scr, *, heads, nkeys):
    tb = x_ref.shape[0]
    hf = _rms(x_ref[...], g_ref[...])
    hb = hf.astype(_MXU_DTYPE)
    h_ref[...] = hf.T.astype(_MXU_DTYPE)
    q = jnp.dot(hb, wq_ref[...], preferred_element_type=F32)
    for hc in range(2 * heads):
        qs = q[:, hc * nkeys:(hc + 1) * nkeys].astype(_MXU_DTYPE)
        s_scr[hc] = lax.dot_general(keys_ref[hc], qs, _NT, preferred_element_type=F32)

    nrow = nkeys // _SUBLANES
    bcast = lambda row: jnp.broadcast_to(row, (_SUBLANES, _LANES))

    def group(gi, carry):
        lanes = pl.ds(pl.multiple_of(gi * _LANES, _LANES), _LANES)
        for hc in range(2 * heads):
            hd, c = divmod(hc, 2)
            rows = [s_scr[hc, pl.ds(_SUBLANES * m, _SUBLANES), lanes] for m in range(nrow)]
            srt = _sort_desc(rows)
            for shift in (4, 2, 1):
                srt = _top_of_two(srt, [pltpu.roll(r, shift, 0) for r in srt])
            for i in range(_PK_TOPK):
                top_scr[c, i, pl.ds(hd, 1), :] = srt[i][0:1, :]
            if c == 1:
                for i in range(_PK_TOPK):
                    btop_scr[hd, i] = srt[i]
                ebs, rbs = [], []
                for m in range(nrow):
                    ebs.append(jnp.exp(rows[m] - srt[0]))
                    rank = jnp.full_like(rows[m], float(_PK_TOPK))
                    for i in reversed(range(_PK_TOPK)):
                        rank = jnp.where(srt[i] <= rows[m], float(i), rank)
                    rbs.append(rank)
                for m in range(0, nrow, 2):
                    sl = pl.ds(_SUBLANES * m, 2 * _SUBLANES)
                    eb_ref[hd, sl, lanes] = jnp.concatenate(ebs[m:m + 2], axis=0).astype(eb_ref.dtype)
                    rb_ref[hd, sl, lanes] = jnp.concatenate(rbs[m:m + 2], axis=0).astype(rb_ref.dtype)
        a_s = [top_scr[0, i] for i in range(_PK_TOPK)]
        b_s = [top_scr[1, i] for i in range(_PK_TOPK)]
        top = _top16_sums(a_s, b_s)
        z = jnp.exp(top[0] - top[0])
        for t in top[1:]:
            z = z + jnp.exp(t - top[0])
        rz = 1.0 / z
        theta = top[_PK_TOPK - 1]
        for hd in range(heads):
            amax = bcast(a_s[0][hd:hd + 1, :])
            rz_h = bcast(rz[hd:hd + 1, :])
            th_h = bcast(theta[hd:hd + 1, :])
            for m in range(nrow):
                sl = pl.ds(_SUBLANES * m, _SUBLANES)
                sa = s_scr[2 * hd, sl, lanes]
                ea_ref[hd, sl, lanes] = _gate_word(jnp.exp(sa - amax) * rz_h, ea_ref.dtype)
                cnt = jnp.full_like(sa, float(_PK_TOPK))
                for j in reversed(range(_PK_TOPK)):
                    cnt = jnp.where(sa + btop_scr[hd, j] < th_h, float(j), cnt)
                ca_ref[hd, sl, lanes] = _gate_word(cnt, ca_ref.dtype)
        return carry

    lax.fori_loop(0, tb // _LANES, group, 0)


def _peer_route(x2d, gain, wq, keys, *, tb):
    t, d = x2d.shape
    hc, nkeys, dk = keys.shape
    heads = hc // 2
    kern = functools.partial(_peer_route_kernel, heads=heads, nkeys=nkeys)
    tab = pl.BlockSpec((heads, nkeys, tb), lambda i: (0, 0, i))
    word = F32 if _MXU_DTYPE == F32 else jnp.uint32
    return pl.pallas_call(
        kern,
        grid=(t // tb,),
        in_specs=[
            pl.BlockSpec((tb, d), lambda i: (i, 0)),
            pl.BlockSpec((1, d), lambda i: (0, 0)),
            pl.BlockSpec(wq.shape, lambda i: (0, 0)),
            pl.BlockSpec(keys.shape, lambda i: (0, 0, 0)),
        ],
        out_specs=[pl.BlockSpec((d, tb), lambda i: (0, i)), tab, tab, tab, tab],
        out_shape=[
            jax.ShapeDtypeStruct((d, t), _MXU_DTYPE),
            jax.ShapeDtypeStruct((heads, nkeys, t), word),
            jax.ShapeDtypeStruct((heads, nkeys, t), word),
            jax.ShapeDtypeStruct((heads, nkeys, t), _MXU_DTYPE),
            jax.ShapeDtypeStruct((heads, nkeys, t), _MXU_DTYPE),
        ],
        scratch_shapes=[
            pltpu.VMEM((hc, nkeys, tb), F32),
            pltpu.VMEM((2, _PK_TOPK, heads, _LANES), F32),
            pltpu.VMEM((heads, _PK_TOPK, _SUBLANES, _LANES), F32),
        ],
        compiler_params=_params(("parallel",)),
        name="peer_route",
    )(x2d, gain, wq, keys)


def _gelu(x):
    return 0.5 * x * (1.0 + lax.erf(x * (1.0 / math.sqrt(2.0))))


def _peer_dense_kernel(h_ref, u_ref, vt_ref, ea_ref, ca_ref, eb_ref, rb_ref, x_ref, o_ref, acc_ref,
                       hu_ref, w_ref, *, heads, nkeys):
    e = pl.program_id(1)
    tb = h_ref.shape[1]
    a_t = ea_ref.shape[1]
    wdt = vt_ref.dtype
    pk = _SUBLANES * (4 // jnp.dtype(wdt).itemsize)
    zero = jnp.zeros((pk, _LANES), wdt)
    mc = _pick(tb, (512, 256))

    @pl.when(e == 0)
    def _():
        acc_ref[...] = jnp.zeros_like(acc_ref)

    mr = 2 * nkeys
    for c in range(tb // mc):
        toks = slice(c * mc, (c + 1) * mc)
        for r in range(a_t * nkeys // mr):
            rows = slice(r * mr, (r + 1) * mr)
            hu_ref[rows, toks] = jnp.dot(u_ref[rows, :], h_ref[:, toks], preferred_element_type=F32)

    for c in range(tb // mc):
        toks = slice(c * mc, (c + 1) * mc)
        for a0 in range(0, a_t, 2):
            for tc in range(mc // _LANES):
                lanes = slice(c * mc + tc * _LANES, c * mc + (tc + 1) * _LANES)
                accs = [[None] * (nkeys // pk) for _ in range(2)]
                for hd in range(heads):
                    ea_b = [_gate_rows(ea_ref, hd, a0 + i, lanes, wdt) for i in range(2)]
                    ca_b = [_gate_rows(ca_ref, hd, a0 + i, lanes, wdt) for i in range(2)]
                    for bc in range(nkeys // pk):
                        rows = slice(bc * pk, (bc + 1) * pk)
                        rbv, ebv = rb_ref[hd, rows, lanes], eb_ref[hd, rows, lanes]
                        for i in range(2):
                            term = ea_b[i] * lax.clamp(zero, ca_b[i] - rbv, ebv)
                            accs[i][bc] = term if accs[i][bc] is None else accs[i][bc] + term
                for i in range(2):
                    for bc in range(nkeys // pk):
                        erows = slice((a0 + i) * nkeys + bc * pk, (a0 + i) * nkeys + (bc + 1) * pk)
                        w_ref[erows, lanes] = accs[i][bc] * _gelu(hu_ref[erows, lanes].astype(wdt))
        acc_ref[:, toks] += jnp.dot(vt_ref[...], w_ref[:, toks], preferred_element_type=F32)

    @pl.when(e == pl.num_programs(1) - 1)
    def _():
        o_ref[...] = x_ref[...] + acc_ref[...].T


def _peer_dense(h, u, vt, ea, ca, eb, rb, x2d, *, tb, a_t):
    t, d = x2d.shape
    heads, nkeys, _ = ea.shape
    e_t = a_t * nkeys
    n_e = u.shape[0] // e_t
    kern = functools.partial(_peer_dense_kernel, heads=heads, nkeys=nkeys)
    gate_a = pl.BlockSpec((heads, a_t, tb), lambda i, e: (0, e, i))
    gate_b = pl.BlockSpec((heads, nkeys, tb), lambda i, e: (0, 0, i))
    return pl.pallas_call(
        kern,
        grid=(t // tb, n_e),
        in_specs=[
            pl.BlockSpec((d, tb), lambda i, e: (0, i)),
            pl.BlockSpec((e_t, d), lambda i, e: (e, 0)),
            pl.BlockSpec((d, e_t), lambda i, e: (0, e)),
            gate_a, gate_a, gate_b, gate_b,
            pl.BlockSpec((tb, d), lambda i, e: (i, 0)),
        ],
        out_specs=pl.BlockSpec((tb, d), lambda i, e: (i, 0)),
        out_shape=jax.ShapeDtypeStruct((t, d), F32),
        scratch_shapes=[pltpu.VMEM((d, tb), F32), pltpu.VMEM((e_t, tb), F32), pltpu.VMEM((e_t, tb), _MXU_DTYPE)],
        compiler_params=_params(("parallel", "arbitrary")),
        name="peer_dense",
    )(h, u, vt, ea, ca, eb, rb, x2d)


def _peer_layer(x2d, gain, wq, keys, u, vt, *, tb_route, tb_dense, a_t):
    h, ea, ca, eb, rb = _peer_route(x2d, gain, wq, keys, tb=tb_route)
    return _peer_dense(h, u, vt, ea, ca, eb, rb, x2d, tb=tb_dense, a_t=a_t)


def _norm_proj_kernel(x_ref, g_ref, w_ref, o_ref, hb_ref):
    @pl.when(pl.program_id(1) == 0)
    def _():
        hb_ref[...] = _rms(x_ref[...], g_ref[...]).astype(hb_ref.dtype)

    o_ref[...] = jnp.dot(hb_ref[...], w_ref[...], preferred_element_type=F32)


def _norm_proj(x2d, gain, w, *, tm, tn):
    t, d = x2d.shape
    n = w.shape[1]
    return pl.pallas_call(
        _norm_proj_kernel,
        grid=(t // tm, n // tn),
        in_specs=[
            pl.BlockSpec((tm, d), lambda i, j: (i, 0)),
            pl.BlockSpec((1, d), lambda i, j: (0, 0)),
            pl.BlockSpec((d, tn), lambda i, j: (0, j)),
        ],
        out_specs=pl.BlockSpec((tm, tn), lambda i, j: (i, j)),
        out_shape=jax.ShapeDtypeStruct((t, n), F32),
        scratch_shapes=[pltpu.VMEM((tm, d), _MXU_DTYPE)],
        compiler_params=_params(("parallel", "arbitrary")),
        name="norm_proj",
    )(x2d, gain, w)


def _gated_out_kernel(o_ref, g_ref, gain_ref, w_ref, x_ref, y_ref, a_ref, *, dv):
    vd = o_ref.shape[1]
    for hd in range(vd // dv):
        cols = slice(hd * dv, (hd + 1) * dv)
        o = o_ref[:, cols]
        g = g_ref[:, cols]
        a_ref[:, cols] = (_rms(o, gain_ref[:, cols]) * (g * jax.nn.sigmoid(g))).astype(a_ref.dtype)
    y_ref[...] = x_ref[...] + jnp.dot(a_ref[...], w_ref[...], preferred_element_type=F32)


def _gated_out(o, p, g_block, gain, w, x2d, *, dv, tm):
    t, vd = o.shape
    d = x2d.shape[1]
    return pl.pallas_call(
        functools.partial(_gated_out_kernel, dv=dv),
        grid=(t // tm,),
        in_specs=[
            pl.BlockSpec((tm, vd), lambda i: (i, 0)),
            pl.BlockSpec((tm, vd), lambda i: (i, g_block)),
            pl.BlockSpec((1, vd), lambda i: (0, 0)),
            pl.BlockSpec((vd, d), lambda i: (0, 0)),
            pl.BlockSpec((tm, d), lambda i: (i, 0)),
        ],
        out_specs=pl.BlockSpec((tm, d), lambda i: (i, 0)),
        out_shape=jax.ShapeDtypeStruct((t, d), F32),
        scratch_shapes=[pltpu.VMEM((tm, vd), _MXU_DTYPE)],
        compiler_params=_params(("parallel",)),
        name="gated_out",
    )(o, p, gain, w, x2d)


_ROPE_THETA = 10000.0


def _rope_kernel(pos_ref, cr_ref, sr_ref, ta_ref, *, rt_half, at_half):
    pos = pos_ref[...]
    lane = lax.broadcasted_iota(jnp.int32, pos.shape, 1)
    inv = jnp.exp(lane.astype(F32) * (-math.log(_ROPE_THETA) / rt_half))
    ang = pos * inv
    cr_ref[...] = jnp.cos(ang)
    sr_ref[...] = jnp.sin(ang)
    j = lane % (2 * at_half)
    inv = jnp.exp((j % at_half).astype(F32) * (-math.log(_ROPE_THETA) / at_half))
    ang = pos * inv
    sn = jnp.sin(ang)
    ta_ref[:, :_LANES] = jnp.cos(ang)
    ta_ref[:, _LANES:2 * _LANES] = jnp.where(j < at_half, -sn, 0.0)
    ta_ref[:, 2 * _LANES:] = jnp.where(j < at_half, 0.0, sn)


def _rope_tables(pos_b, *, rt_half, at_half, tm):
    t = pos_b.shape[0]
    spec = pl.BlockSpec((tm, _LANES), lambda i: (i, 0))
    return pl.pallas_call(
        functools.partial(_rope_kernel, rt_half=rt_half, at_half=at_half),
        grid=(t // tm,),
        in_specs=[spec],
        out_specs=[spec, spec, pl.BlockSpec((tm, 3 * _LANES), lambda i: (i, 0))],
        out_shape=[jax.ShapeDtypeStruct((t, _LANES), F32)] * 2 + [jax.ShapeDtypeStruct((t, 3 * _LANES), F32)],
        compiler_params=_params(("parallel",)),
        name="rope_tables",
    )(pos_b)


_HG_CHUNK = 64
_HG_SUB = 16
_TINY = 1e-30
_TN = (((0,), (0,)), ((), ()))


def _split_dot(a, b_f32):
    hi = b_f32.astype(BF16)
    lo = (b_f32 - hi.astype(F32)).astype(BF16)
    return (jnp.dot(a, hi, preferred_element_type=F32) + jnp.dot(a, lo, preferred_element_type=F32))


def _hgrn_kernel(q_ref, f_ref, i_ref, lb_ref, o_ref, st_ref, *, layer, heads):
    c, sc = _HG_CHUNK, _HG_SUB
    nchunk = q_ref.shape[0] // c
    dk = q_ref.shape[1] // heads

    @pl.when(pl.program_id(1) == 0)
    def _():
        st_ref[...] = jnp.zeros_like(st_ref)

    lbs = lb_ref[...]
    e = jnp.exp(lbs - jnp.max(lbs, axis=0, keepdims=True))
    soft = e / jnp.sum(e, axis=0, keepdims=True)
    lb_all = jnp.zeros_like(soft[0:1])
    for l in range(1, layer + 1):
        lb_all = lb_all + soft[l:l + 1]

    r_i = lax.broadcasted_iota(jnp.int32, (c, c), 0)
    c_i = lax.broadcasted_iota(jnp.int32, (c, c), 1)
    tri = (r_i >= c_i).astype(BF16)
    sub_t = lax.broadcasted_iota(jnp.int32, (sc, 1), 0)

    def head_chunk(rows, hd):
        cols = slice(hd * dk, (hd + 1) * dk)
        lb = lb_all[:, cols]
        qr = q_ref[rows, cols]
        fr = f_ref[rows, cols]
        v = i_ref[rows, cols]
        q = qr * jax.nn.sigmoid(qr)
        forget = lb + (1.0 - lb) * jax.nn.sigmoid(fr)
        logf = jnp.log(jnp.maximum(forget, _TINY))
        k = (1.0 - lb) * jax.nn.sigmoid(-fr)
        if _MXU_DTYPE == BF16:
            b = _split_dot(tri, logf)
        else:
            b = jnp.dot(tri.astype(F32), logf, preferred_element_type=F32)
        st = st_ref[hd]
        o = lax.dot_general((q * jnp.exp(b)).astype(_MXU_DTYPE), st.astype(_MXU_DTYPE), _NT,
                            preferred_element_type=F32)
        vb = v.astype(_MXU_DTYPE)
        outs = []
        for i in range(c // sc):
            lo, hi = i * sc, (i + 1) * sc
            qi, bi = q[lo:hi], b[lo:hi]
            oi = o[lo:hi]
            if i > 0:
                ref_b = b[lo - 1:lo]
                qt = (qi * jnp.exp(bi - ref_b)).astype(_MXU_DTYPE)
                kt = (k[:lo] * jnp.exp(ref_b - b[:lo])).astype(_MXU_DTYPE)
                att = lax.dot_general(qt, kt, _NT, preferred_element_type=F32)
                oi = oi + jnp.dot(att.astype(_MXU_DTYPE), vb[:lo], preferred_element_type=F32)
            for s in range(sc):
                dec = jnp.exp(bi - bi[s:s + 1])
                col = jnp.sum(qi * k[lo + s:lo + s + 1] * dec, axis=1, keepdims=True)
                col = jnp.where(sub_t >= s, col, 0.0)
                oi = oi + col * v[lo + s:lo + s + 1]
            outs.append(oi)
        o_ref[rows, cols] = jnp.concatenate(outs, axis=0)
        b_last = b[c - 1:c]
        kd = (k * jnp.exp(b_last - b)).astype(_MXU_DTYPE)
        st_ref[hd] = jnp.exp(b_last) * st + lax.dot_general(vb, kd, _TN, preferred_element_type=F32)

    def chunk(ci, carry):
        rows = pl.ds(pl.multiple_of(ci * c, c), c)
        for hd in range(heads):
            head_chunk(rows, hd)
        return carry

    lax.fori_loop(0, nchunk, chunk, 0)


def _hgrn_scan(p, hg_lb, *, batch, seq, heads, dk, layer, blk):
    t = p.shape[0]
    nblk = seq // blk
    width = heads * dk
    row = lambda b, s: b * nblk + s
    return pl.pallas_call(
        functools.partial(_hgrn_kernel, layer=layer, heads=heads),
        grid=(batch, nblk),
        in_specs=[
            pl.BlockSpec((blk, width), lambda b, s: (row(b, s), 0)),
            pl.BlockSpec((blk, width), lambda b, s: (row(b, s), 1)),
            pl.BlockSpec((blk, width), lambda b, s: (row(b, s), 2)),
            pl.BlockSpec((hg_lb.shape[0], width), lambda b, s: (0, 0)),
        ],
        out_specs=pl.BlockSpec((blk, width), lambda b, s: (row(b, s), 0)),
        out_shape=jax.ShapeDtypeStruct((t, width), F32),
        scratch_shapes=[pltpu.VMEM((heads, dk, dk), F32)],
        compiler_params=_params(("parallel", "arbitrary")),
        name="hgrn_scan",
    )(p, p, p, hg_lb)


_RT_CHUNK = 128


def _retention_kernel(q_ref, k_ref, v_ref, cos_ref, sin_ref, o_ref, st_ref):
    c = _RT_CHUNK
    nchunk = q_ref.shape[0] // c
    dk = q_ref.shape[1]
    half = dk // 2

    @pl.when(pl.program_id(2) == 0)
    def _():
        st_ref[...] = jnp.zeros_like(st_ref)

    hv = jnp.full((1, _LANES), pl.program_id(1), jnp.int32).astype(F32)
    lg = jnp.log1p(-jnp.exp2(-5.0 - hv))
    t_i = lax.broadcasted_iota(jnp.int32, (c, c), 0)
    s_i = lax.broadcasted_iota(jnp.int32, (c, c), 1)
    rel = (t_i - s_i).astype(F32)
    decay = jnp.where(rel >= 0, jnp.exp(lg * jnp.maximum(rel, 0.0)), 0.0)
    idx = lax.broadcasted_iota(jnp.int32, (c, _LANES), 0).astype(F32)
    q_dec = jnp.exp(lg * (idx + 1.0))
    k_dec = jnp.exp(lg * (c - 1.0 - idx))
    chunk_dec = jnp.exp(lg * c)[:, :1]

    def rope(x, cos, sin):
        x1, x2 = x[:, :half], x[:, half:]
        return x1 * cos - x2 * sin, x2 * cos + x1 * sin

    def chunk(ci, carry):
        rows = pl.ds(pl.multiple_of(ci * c, c), c)
        cos, sin = cos_ref[rows, :], sin_ref[rows, :]
        q1, q2 = rope(q_ref[rows, :], cos, sin)
        k1, k2 = rope(k_ref[rows, :], cos, sin)
        scale = dk ** -0.5
        k1, k2 = k1 * scale, k2 * scale
        cat = lambda a, b: jnp.concatenate([a, b], axis=1).astype(_MXU_DTYPE)
        qb, kb = cat(q1, q2), cat(k1, k2)
        vb = v_ref[rows, :].astype(_MXU_DTYPE)
        att = lax.dot_general(qb, kb, _NT, preferred_element_type=F32) * decay
        o = jnp.dot(att.astype(_MXU_DTYPE), vb, preferred_element_type=F32)
        st = st_ref[...]
        o = o + jnp.dot(cat(q1 * q_dec, q2 * q_dec), st.astype(_MXU_DTYPE), preferred_element_type=F32)
        o_ref[rows, :] = o
        st_ref[...] = chunk_dec * st + lax.dot_general(cat(k1 * k_dec, k2 * k_dec), vb, _TN,
                                                       preferred_element_type=F32)
        return carry

    lax.fori_loop(0, nchunk, chunk, 0)


def _retention_scan(p, cos, sin, *, batch, seq, heads, dk, dv, blk):
    t = p.shape[0]
    nblk = seq // blk
    row = lambda b, h, s: b * nblk + s
    v_off = 2 * heads * dk // dv
    return pl.pallas_call(
        _retention_kernel,
        grid=(batch, heads, nblk),
        in_specs=[
            pl.BlockSpec((blk, dk), lambda b, h, s: (row(b, h, s), h)),
            pl.BlockSpec((blk, dk), lambda b, h, s: (row(b, h, s), heads + h)),
            pl.BlockSpec((blk, dv), lambda b, h, s: (row(b, h, s), v_off + h)),
            pl.BlockSpec((blk, _LANES), lambda b, h, s: (row(b, h, s), 0)),
            pl.BlockSpec((blk, _LANES), lambda b, h, s: (row(b, h, s), 0)),
        ],
        out_specs=pl.BlockSpec((blk, dv), lambda b, h, s: (row(b, h, s), h)),
        out_shape=jax.ShapeDtypeStruct((t, heads * dv), F32),
        scratch_shapes=[pltpu.VMEM((dk, dv), F32)],
        compiler_params=_params(("parallel", "parallel", "arbitrary")),
        name="retention_scan",
    )(p, p, p, cos, sin)


_AT_BLOCK = 128
_MASK_VALUE = -1e30


def _split_dot_lhs(a_f32, b):
    hi = a_f32.astype(BF16)
    lo = (a_f32 - hi.astype(F32)).astype(BF16)
    return (jnp.dot(hi, b, preferred_element_type=F32) + jnp.dot(lo, b, preferred_element_type=F32))


def _attn_kernel(q_ref, kp_ref, k_ref, vp_ref, v_ref, t0_ref, t1_ref, t2_ref, p0_ref, p1_ref, p2_ref,
                 gq_ref, gk_ref, o_ref, l_ref, *, head_dim, span, dil, nsub):
    blk = _AT_BLOCK
    half = head_dim // 2
    lane = lax.broadcasted_iota(jnp.int32, (1, _LANES), 1)
    qi = lax.broadcasted_iota(jnp.int32, (blk, 2 * blk), 0) + blk
    ki = lax.broadcasted_iota(jnp.int32, (blk, 2 * blk), 1)
    band = (ki <= qi) & (ki >= qi - span)
    first_lo = jnp.where(pl.program_id(2) == 0, blk, 0)
    band_first = (ki <= qi) & (ki >= jnp.maximum(qi - span, first_lo))
    scale = head_dim ** -0.5
    r_i = lax.broadcasted_iota(jnp.int32, (_LANES, _LANES), 0) // head_dim
    c_i = lax.broadcasted_iota(jnp.int32, (_LANES, _LANES), 1) // head_dim
    seg = jnp.where(r_i == c_i, 1.0 / head_dim, 0.0)
    gq, gk = gq_ref[...], gk_ref[...]

    nh = _LANES // head_dim
    gain_qk = jnp.concatenate([jnp.broadcast_to(gq, (blk, _LANES)), jnp.broadcast_to(gk, (blk, _LANES))], axis=0)
    heads_of = lane // head_dim
    segb = seg.astype(_MXU_DTYPE)

    def prep(x, gain, tab):
        msq = jnp.dot((x * x).astype(_MXU_DTYPE), segb, preferred_element_type=F32)
        y = x * lax.rsqrt(msq + _EPS) * gain
        cos, s1, s2 = tab
        y = y * cos + pltpu.roll(y, _LANES - half, 1) * s1 + pltpu.roll(y, half, 1) * s2
        return y.astype(_MXU_DTYPE)

    def residue(r, carry):
        def take(ref, j):
            if dil == 1:
                return ref[j * blk:(j + 1) * blk, :]
            return ref[pl.ds(r + j * blk * dil, blk, stride=dil), :]

        kprev = prep(take(kp_ref, 0), gk, [take(t, 0) for t in (p0_ref, p1_ref, p2_ref)])
        vprev = take(vp_ref, 0).astype(_MXU_DTYPE)
        for j in range(nsub):
            tab = [take(t, j) for t in (t0_ref, t1_ref, t2_ref)]
            qk = prep(jnp.concatenate([take(q_ref, j), take(k_ref, j)], axis=0), gain_qk,
                      [jnp.concatenate([t, t], axis=0) for t in tab])
            q, kcur = qk[:blk], qk[blk:]
            vcur = take(v_ref, j).astype(_MXU_DTYPE)
            kk = jnp.concatenate([kprev, kcur], axis=0)
            vv = jnp.concatenate([vprev, vcur], axis=0)
            mask = band_first if j == 0 else band
            qh = jnp.concatenate([jnp.where(heads_of == hd, q, jnp.zeros_like(q)) for hd in range(nh)], axis=0)
            s = lax.dot_general(qh, kk, _NT, preferred_element_type=F32) * scale
            s = jnp.where(jnp.concatenate([mask] * nh, axis=0), s, _MASK_VALUE)
            m = jnp.max(s, axis=-1, keepdims=True)
            pr = jnp.exp(s - m)
            l = jnp.sum(pr, axis=-1, keepdims=True)
            o = jnp.dot(pr.astype(_MXU_DTYPE), vv, preferred_element_type=F32) / l
            lse = m + jnp.log(l)
            o_pair = o[:blk]
            l_pair = jnp.broadcast_to(lse[:blk], o_pair.shape)
            for hd in range(1, nh):
                o_pair = jnp.where(heads_of == hd, o[hd * blk:(hd + 1) * blk], o_pair)
                l_pair = jnp.where(heads_of == hd, lse[hd * blk:(hd + 1) * blk], l_pair)
            if dil == 1:
                o_ref[j * blk:(j + 1) * blk, :] = o_pair
                l_ref[j * blk:(j + 1) * blk, :] = l_pair
            else:
                o_ref[pl.ds(r + j * blk * dil, blk, stride=dil), :] = o_pair
                l_ref[pl.ds(r + j * blk * dil, blk, stride=dil), :] = l_pair
            kprev, vprev = kcur, vcur
        return carry

    if dil == 1:
        residue(0, 0)
    else:
        def two(i, carry):
            residue(2 * i, carry)
            return residue(2 * i + 1, carry)

        lax.fori_loop(0, dil // 2, two, 0)


def _attn_group(p, tabs, gq, gk, *, batch, seq, group, ngroup, heads, head_dim, window, dilation):
    t = p.shape[0]
    d = dilation
    width = heads * head_dim
    assert seq % (d * _AT_BLOCK) == 0 and window // d <= _AT_BLOCK
    nb = seq // d // _AT_BLOCK
    nsub = max(1, min(4, nb, 2048 // (_AT_BLOCK * d)))
    nstep = nb // nsub
    rows = nsub * _AT_BLOCK * d
    prow = _AT_BLOCK * d
    pairs = width // _LANES
    col = lambda kind, hp: (kind * ngroup + group) * pairs + hp
    main = lambda b, n: b * nstep + n
    prev = lambda b, n: jnp.maximum((b * nstep + n) * nsub - 1, 0)
    out_spec = pl.BlockSpec((rows, _LANES), lambda b, hp, n: (main(b, n), hp))
    return pl.pallas_call(
        functools.partial(_attn_kernel, head_dim=head_dim, span=window // d, dil=d, nsub=nsub),
        grid=(batch, pairs, nstep),
        in_specs=[
            pl.BlockSpec((rows, _LANES), lambda b, hp, n: (main(b, n), col(0, hp))),
            pl.BlockSpec((prow, _LANES), lambda b, hp, n: (prev(b, n), col(1, hp))),
            pl.BlockSpec((rows, _LANES), lambda b, hp, n: (main(b, n), col(1, hp))),
            pl.BlockSpec((prow, _LANES), lambda b, hp, n: (prev(b, n), col(2, hp))),
            pl.BlockSpec((rows, _LANES), lambda b, hp, n: (main(b, n), col(2, hp))),
            pl.BlockSpec((rows, _LANES), lambda b, hp, n: (main(b, n), 0)),
            pl.BlockSpec((rows, _LANES), lambda b, hp, n: (main(b, n), 1)),
            pl.BlockSpec((rows, _LANES), lambda b, hp, n: (main(b, n), 2)),
            pl.BlockSpec((prow, _LANES), lambda b, hp, n: (prev(b, n), 0)),
            pl.BlockSpec((prow, _LANES), lambda b, hp, n: (prev(b, n), 1)),
            pl.BlockSpec((prow, _LANES), lambda b, hp, n: (prev(b, n), 2)),
            pl.BlockSpec((1, _LANES), lambda b, hp, n: (0, 0)),
            pl.BlockSpec((1, _LANES), lambda b, hp, n: (0, 0)),
        ],
        out_specs=[out_spec, out_spec],
        out_shape=[jax.ShapeDtypeStruct((t, width), F32)] * 2,
        compiler_params=_params(("parallel", "parallel", "arbitrary")),
        name=f"attn_group{group}",
    )(p, p, p, p, p, tabs, tabs, tabs, tabs, tabs, tabs, gq, gk)


def _attn_out_kernel(*refs, ngroup):
    o_refs, l_refs = refs[:ngroup], refs[ngroup:2 * ngroup]
    w_ref, x_ref, y_ref = refs[2 * ngroup:]
    ls = [r[...] for r in l_refs]
    m = functools.reduce(jnp.maximum, ls)
    es = [jnp.exp(l - m) for l in ls]
    den = functools.reduce(lambda a, b: a + b, es)
    o = functools.reduce(lambda a, b: a + b, [e * r[...] for e, r in zip(es, o_refs)]) / den
    y_ref[...] = x_ref[...] + jnp.dot(o.astype(_MXU_DTYPE), w_ref[...], preferred_element_type=F32)


def _attn_out(os_, ls_, w, x2d, *, tm):
    t, width = os_[0].shape
    d = x2d.shape[1]
    ng = len(os_)
    blk = pl.BlockSpec((tm, width), lambda i: (i, 0))
    return pl.pallas_call(
        functools.partial(_attn_out_kernel, ngroup=ng),
        grid=(t // tm,),
        in_specs=[blk] * (2 * ng) + [pl.BlockSpec((width, d), lambda i: (0, 0)),
                                     pl.BlockSpec((tm, d), lambda i: (i, 0))],
        out_specs=pl.BlockSpec((tm, d), lambda i: (i, 0)),
        out_shape=jax.ShapeDtypeStruct((t, d), F32),
        compiler_params=_params(("parallel",)),
        name="attn_out",
    )(*os_, *ls_, w, x2d)


_HG_HEADS, _HG_DK = 8, 128
_AT_GROUPS = ((128, 1), (512, 4), (2048, 16))
_AT_HEADS, _AT_HEAD_DIM = 8, 64
_RT_HEADS = 4
_N_MIXERS = 3


def _pick(n, candidates):
    for c in candidates:
        if n % c == 0:
            return c
    return n


def _hgrn_mixer(x2d, gain, w_in, hg_lb, o_gain, w_out, *, batch, seq, layer, blk):
    t = x2d.shape[0]
    p = _norm_proj(x2d, gain, w_in, tm=_pick(t, (512, 256, 128)), tn=_pick(w_in.shape[1], (1024, 512)))
    o = _hgrn_scan(p, hg_lb, batch=batch, seq=seq, heads=_HG_HEADS, dk=_HG_DK, layer=layer, blk=min(blk, seq))
    return _gated_out(o, p, 3, o_gain, w_out, x2d, dv=_HG_DK, tm=_pick(t, (256, 128)))


def _retention_mixer(x2d, gain, w_in, o_gain, w_out, cos, sin, *, batch, seq, blk):
    t, d = x2d.shape
    dk = d // _RT_HEADS
    dv = 2 * dk
    assert dk == 2 * _LANES
    p = _norm_proj(x2d, gain, w_in, tm=_pick(t, (512, 256, 128)), tn=_pick(w_in.shape[1], (1536, 1024, 512)))
    o = _retention_scan(p, cos, sin, batch=batch, seq=seq, heads=_RT_HEADS, dk=dk, dv=dv, blk=min(blk, seq))
    return _gated_out(o, p, 2, o_gain, w_out, x2d, dv=dv, tm=_pick(t, (256, 128)))


def _attn_mixer(x2d, gain, w_in, q_gain, k_gain, w_out, tabs, *, batch, seq):
    t = x2d.shape[0]
    ng = len(_AT_GROUPS)
    p = _norm_proj(x2d, gain, w_in, tm=_pick(t, (512, 256, 128)), tn=_pick(w_in.shape[1], (1536, 512)))
    os_, ls_ = [], []
    for g, (window, dilation) in enumerate(_AT_GROUPS):
        gq = jnp.tile(q_gain[g], _LANES // _AT_HEAD_DIM)[None, :]
        gk = jnp.tile(k_gain[g], _LANES // _AT_HEAD_DIM)[None, :]
        o, lse = _attn_group(p, tabs, gq, gk, batch=batch, seq=seq, group=g, ngroup=ng, heads=_AT_HEADS,
                             head_dim=_AT_HEAD_DIM, window=window, dilation=dilation)
        os_.append(o)
        ls_.append(lse)
    return _attn_out(os_, ls_, w_out, x2d, tm=_pick(t, (512, 256, 128)))


def kernel(x, positions, mix_norm, ffn_norm, hg_lb, hg_w_in, hg_onorm, hg_w_out, at_w_in, at_qnorm, at_knorm,
           at_w_out, rt_w_in, rt_onorm, rt_w_out, pk_w_q, pk_keys, pk_u, pk_v):
    b, s, d = x.shape
    depth = mix_norm.shape[0]
    x2d = x.reshape(b * s, d)
    pos_b = jnp.broadcast_to(positions.reshape(b * s, 1).astype(F32), (b * s, _LANES))
    rt_dk = d // _RT_HEADS
    cos_r, sin_r, tabs_a = _rope_tables(pos_b, rt_half=rt_dk // 2, at_half=_AT_HEAD_DIM // 2, tm=512)
    ia = ib = ic = 0
    for layer in range(depth):
        kind = layer % _N_MIXERS
        gain = mix_norm[layer][None, :]
        if kind == 0:
            x2d = _hgrn_mixer(x2d, gain, hg_w_in[ia].astype(_MXU_DTYPE), hg_lb, hg_onorm[ia][None, :],
                              hg_w_out[ia].astype(_MXU_DTYPE), batch=b, seq=s, layer=layer, blk=256)
            ia += 1
        elif kind == 1:
            x2d = _attn_mixer(x2d, gain, at_w_in[ib].astype(_MXU_DTYPE), at_qnorm[ib], at_knorm[ib],
                              at_w_out[ib].astype(_MXU_DTYPE), tabs_a, batch=b, seq=s)
            ib += 1
        else:
            x2d = _retention_mixer(x2d, gain, rt_w_in[ic].astype(_MXU_DTYPE), rt_onorm[ic][None, :],
                                   rt_w_out[ic].astype(_MXU_DTYPE), cos_r, sin_r, batch=b, seq=s, blk=256)
            ic += 1
        keys = pk_keys[layer].reshape(-1, pk_keys.shape[-2], pk_keys.shape[-1]).astype(_MXU_DTYPE)
        x2d = _peer_layer(x2d, ffn_norm[layer][None, :], pk_w_q[layer].astype(_MXU_DTYPE), keys,
                          pk_u[layer].astype(_MXU_DTYPE), pk_v[layer].T.astype(_MXU_DTYPE),
                          tb_route=512, tb_dense=512, a_t=16)
    return x2d.reshape(b, s, d)
```

```python
import functools
import math

import jax
import jax.numpy as jnp
from jax import lax
from jax.experimental import pallas as pl
from jax.experimental.pallas import tpu as pltpu

F32 = jnp.float32
BF16 = jnp.bfloat16
_MXU_DTYPE = BF16
_EPS = 1e-6
_LANES = 128
_SUBLANES = 8
_VMEM_LIMIT = 56 * 1024 * 1024

_NT = (((1,), (1,)), ((), ()))


def _params(sem, flags=None):
    return pltpu.CompilerParams(dimension_semantics=sem, vmem_limit_bytes=_VMEM_LIMIT, flags=flags)


def _rms(x, gain):
    return x * lax.rsqrt(jnp.mean(x * x, axis=-1, keepdims=True) + _EPS) * gain


def _merge_desc(c):
    c = list(c)
    n = len(c)
    j = n // 2
    while j >= 1:
        for i in range(n):
            l = i ^ j
            if l > i:
                hi = jnp.maximum(c[i], c[l])
                lo = jnp.minimum(c[i], c[l])
                c[i], c[l] = hi, lo
        j //= 2
    return c


def _sort_desc(xs):
    xs = list(xs)
    n = len(xs)
    k = 2
    while k <= n:
        j = k // 2
        while j >= 1:
            for i in range(n):
                l = i ^ j
                if l > i:
                    hi = jnp.maximum(xs[i], xs[l])
                    lo = jnp.minimum(xs[i], xs[l])
                    if (i & k) == 0:
                        xs[i], xs[l] = hi, lo
                    else:
                        xs[i], xs[l] = lo, hi
            j //= 2
        k *= 2
    return xs


def _top_of_two(a, b):
    n = len(a)
    return _merge_desc([jnp.maximum(a[i], b[n - 1 - i]) for i in range(n)])


_PK_TOPK = 16
_PK_PAIRS = [(i, j) for i in range(_PK_TOPK) for j in range(_PK_TOPK) if (i + 1) * (j + 1) <= _PK_TOPK]


_NEG = -3.0e38


def _top16_sums(a_top, b_top):
    cand = [a_top[i] + b_top[j] for (i, j) in _PK_PAIRS]
    pad = jnp.full_like(cand[0], _NEG)
    cand = cand + [pad] * (64 - len(cand))
    groups = [_sort_desc(cand[16 * g:16 * (g + 1)]) for g in range(4)]
    return _top_of_two(_top_of_two(groups[0], groups[1]), _top_of_two(groups[2], groups[3]))


def _gate_word(x, dtype):
    if dtype == F32:
        return x
    u = lax.bitcast_convert_type(x.astype(BF16).astype(F32), jnp.uint32)
    return u | (u >> 16)


def _gate_rows(ref, hd, a, lanes, wdt):
    row = jnp.broadcast_to(ref[hd, pl.ds(a, 1), lanes], (_SUBLANES, _LANES))
    return row if wdt == F32 else pltpu.bitcast(row, wdt)


def _peer_route_kernel(x_ref, g_ref, wq_ref, keys_ref, h_ref, ea_ref, ca_ref, eb_ref, rb_ref,
                       s_scr, top_scr, btop_scr, *, heads, nkeys):
    tb = x_ref.shape[0]
    hf = _rms(x_ref[...], g_ref[...])
    hb = hf.astype(_MXU_DTYPE)
    h_ref[...] = hf.T.astype(_MXU_DTYPE)
    q = jnp.dot(hb, wq_ref[...], preferred_element_type=F32)
    for hc in range(2 * heads):
        qs = q[:, hc * nkeys:(hc + 1) * nkeys].astype(_MXU_DTYPE)
        s_scr[hc] = lax.dot_general(keys_ref[hc], qs, _NT, preferred_element_type=F32)

    nrow = nkeys // _SUBLANES
    bcast = lambda row: jnp.broadcast_to(row, (_SUBLANES, _LANES))

    def group(gi, carry):
        lanes = pl.ds(pl.multiple_of(gi * _LANES, _LANES), _LANES)
        for hc in range(2 * heads):
            hd, c = divmod(hc, 2)
            rows = [s_scr[hc, pl.ds(_SUBLANES * m, _SUBLANES), lanes] for m in range(nrow)]
            srt = _sort_desc(rows)
            for shift in (4, 2, 1):
                srt = _top_of_two(srt, [pltpu.roll(r, shift, 0) for r in srt])
            for i in range(_PK_TOPK):
                top_scr[c, i, pl.ds(hd, 1), :] = srt[i][0:1, :]
            if c == 1:
                for i in range(_PK_TOPK):
                    btop_scr[hd, i] = srt[i]
                ebs, rbs = [], []
                for m in range(nrow):
                    ebs.append(jnp.exp(rows[m] - srt[0]))
                    rank = jnp.full_like(rows[m], float(_PK_TOPK))
                    for i in reversed(range(_PK_TOPK)):
                        rank = jnp.where(srt[i] <= rows[m], float(i), rank)
                    rbs.append(rank)
                for m in range(0, nrow, 2):
                    sl = pl.ds(_SUBLANES * m, 2 * _SUBLANES)
                    eb_ref[hd, sl, lanes] = jnp.concatenate(ebs[m:m + 2], axis=0).astype(eb_ref.dtype)
                    rb_ref[hd, sl, lanes] = jnp.concatenate(rbs[m:m + 2], axis=0).astype(rb_ref.dtype)
        a_s = [top_scr[0, i] for i in range(_PK_TOPK)]
        b_s = [top_scr[1, i] for i in range(_PK_TOPK)]
        top = _top16_sums(a_s, b_s)
        z = jnp.exp(top[0] - top[0])
        for t in top[1:]:
            z = z + jnp.exp(t - top[0])
        rz = 1.0 / z
        theta = top[_PK_TOPK - 1]
        for hd in range(heads):
            amax = bcast(a_s[0][hd:hd + 1, :])
            rz_h = bcast(rz[hd:hd + 1, :])
            th_h = bcast(theta[hd:hd + 1, :])
            for m in range(nrow):
                sl = pl.ds(_SUBLANES * m, _SUBLANES)
                sa = s_scr[2 * hd, sl, lanes]
                ea_ref[hd, sl, lanes] = _gate_word(jnp.exp(sa - amax) * rz_h, ea_ref.dtype)
                cnt = jnp.full_like(sa, float(_PK_TOPK))
                for j in reversed(range(_PK_TOPK)):
                    cnt = jnp.where(sa + btop_scr[hd, j] < th_h, float(j), cnt)
                ca_ref[hd, sl, lanes] = _gate_word(cnt, ca_ref.dtype)
        return carry

    lax.fori_loop(0, tb // _LANES, group, 0)


def _peer_route(x2d, gain, wq, keys, *, tb):
    t, d = x2d.shape
    hc, nkeys, dk = keys.shape
    heads = hc // 2
    kern = functools.partial(_peer_route_kernel, heads=heads, nkeys=nkeys)
    tab = pl.BlockSpec((heads, nkeys, tb), lambda i: (0, 0, i))
    word = F32 if _MXU_DTYPE == F32 else jnp.uint32
    return pl.pallas_call(
        kern,
        grid=(t // tb,),
        in_specs=[
            pl.BlockSpec((tb, d), lambda i: (i, 0)),
            pl.BlockSpec((1, d), lambda i: (0, 0)),
            pl.BlockSpec(wq.shape, lambda i: (0, 0)),
            pl.BlockSpec(keys.shape, lambda i: (0, 0, 0)),
        ],
        out_specs=[pl.BlockSpec((d, tb), lambda i: (0, i)), tab, tab, tab, tab],
        out_shape=[
            jax.ShapeDtypeStruct((d, t), _MXU_DTYPE),
            jax.ShapeDtypeStruct((heads, nkeys, t), word),
            jax.ShapeDtypeStruct((heads, nkeys, t), word),
            jax.ShapeDtypeStruct((heads, nkeys, t), _MXU_DTYPE),
            jax.ShapeDtypeStruct((heads, nkeys, t), _MXU_DTYPE),
        ],
        scratch_shapes=[
            pltpu.VMEM((hc, nkeys, tb), F32),
            pltpu.VMEM((2, _PK_TOPK, heads, _LANES), F32),
            pltpu.VMEM((heads, _PK_TOPK, _SUBLANES, _LANES), F32),
        ],
        compiler_params=_params(("parallel",)),
        name="peer_route",
    )(x2d, gain, wq, keys)


def _gelu(x):
    return 0.5 * x * (1.0 + lax.erf(x * (1.0 / math.sqrt(2.0))))


def _peer_dense_kernel(h_ref, u_ref, vt_ref, ea_ref, ca_ref, eb_ref, rb_ref, x_ref, o_ref, acc_ref,
                       hu_ref, w_ref, *, heads, nkeys):
    e = pl.program_id(1)
    tb = h_ref.shape[1]
    a_t = ea_ref.shape[1]
    wdt = vt_ref.dtype
    pk = _SUBLANES * (4 // jnp.dtype(wdt).itemsize)
    zero = jnp.zeros((pk, _LANES), wdt)
    mc = _pick(tb, (512, 256))

    @pl.when(e == 0)
    def _():
        acc_ref[...] = jnp.zeros_like(acc_ref)

    mr = 2 * nkeys
    for c in range(tb // mc):
        toks = slice(c * mc, (c + 1) * mc)
        for r in range(a_t * nkeys // mr):
            rows = slice(r * mr, (r + 1) * mr)
            hu_ref[rows, toks] = jnp.dot(u_ref[rows, :], h_ref[:, toks], preferred_element_type=F32)

    for c in range(tb // mc):
        toks = slice(c * mc, (c + 1) * mc)
        for a0 in range(0, a_t, 2):
            for tc in range(mc // _LANES):
                lanes = slice(c * mc + tc * _LANES, c * mc + (tc + 1) * _LANES)
                accs = [[None] * (nkeys // pk) for _ in range(2)]
                for hd in range(heads):
                    ea_b = [_gate_rows(ea_ref, hd, a0 + i, lanes, wdt) for i in range(2)]
                    ca_b = [_gate_rows(ca_ref, hd, a0 + i, lanes, wdt) for i in range(2)]
                    for bc in range(nkeys // pk):
                        rows = slice(bc * pk, (bc + 1) * pk)
                        rbv, ebv = rb_ref[hd, rows, lanes], eb_ref[hd, rows, lanes]
                        for i in range(2):
                            term = ea_b[i] * lax.clamp(zero, ca_b[i] - rbv, ebv)
                            accs[i][bc] = term if accs[i][bc] is None else accs[i][bc] + term
                for i in range(2):
                    for bc in range(nkeys // pk):
                        erows = slice((a0 + i) * nkeys + bc * pk, (a0 + i) * nkeys + (bc + 1) * pk)
                        w_ref[erows, lanes] = accs[i][bc] * _gelu(hu_ref[erows, lanes].astype(wdt))
        acc_ref[:, toks] += jnp.dot(vt_ref[...], w_ref[:, toks], preferred_element_type=F32)

    @pl.when(e == pl.num_programs(1) - 1)
    def _():
        o_ref[...] = x_ref[...] + acc_ref[...].T


def _peer_dense(h, u, vt, ea, ca, eb, rb, x2d, *, tb, a_t):
    t, d = x2d.shape
    heads, nkeys, _ = ea.shape
    e_t = a_t * nkeys
    n_e = u.shape[0] // e_t
    kern = functools.partial(_peer_dense_kernel, heads=heads, nkeys=nkeys)
    gate_a = pl.BlockSpec((heads, a_t, tb), lambda i, e: (0, e, i))
    gate_b = pl.BlockSpec((heads, nkeys, tb), lambda i, e: (0, 0, i))
    return pl.pallas_call(
        kern,
        grid=(t // tb, n_e),
        in_specs=[
            pl.BlockSpec((d, tb), lambda i, e: (0, i)),
            pl.BlockSpec((e_t, d), lambda i, e: (e, 0)),
            pl.BlockSpec((d, e_t), lambda i, e: (0, e)),
            gate_a, gate_a, gate_b, gate_b,
            pl.BlockSpec((tb, d), lambda i, e: (i, 0)),
        ],
        out_specs=pl.BlockSpec((tb, d), lambda i, e: (i, 0)),
        out_shape=jax.ShapeDtypeStruct((t, d), F32),
        scratch_shapes=[pltpu.VMEM((d, tb), F32), pltpu.VMEM((e_t, tb), F32), pltpu.VMEM((e_t, tb), _MXU_DTYPE)],
        compiler_params=_params(("parallel", "arbitrary")),
        name="peer_dense",
    )(h, u, vt, ea, ca, eb, rb, x2d)


def _peer_layer(x2d, gain, wq, keys, u, vt, *, tb_route, tb_dense, a_t):
    h, ea, ca, eb, rb = _peer_route(x2d, gain, wq, keys, tb=tb_route)
    return _peer_dense(h, u, vt, ea, ca, eb, rb, x2d, tb=tb_dense, a_t=a_t)


def _norm_proj_kernel(x_ref, g_ref, w_ref, o_ref, hb_ref):
    @pl.when(pl.program_id(1) == 0)
    def _():
        hb_ref[...] = _rms(x_ref[...], g_ref[...]).astype(hb_ref.dtype)

    o_ref[...] = jnp.dot(hb_ref[...], w_ref[...], preferred_element_type=F32)


def _norm_proj(x2d, gain, w, *, tm, tn):
    t, d = x2d.shape
    n = w.shape[1]
    return pl.pallas_call(
        _norm_proj_kernel,
        grid=(t // tm, n // tn),
        in_specs=[
            pl.BlockSpec((tm, d), lambda i, j: (i, 0)),
            pl.BlockSpec((1, d), lambda i, j: (0, 0)),
            pl.BlockSpec((d, tn), lambda i, j: (0, j)),
        ],
        out_specs=pl.BlockSpec((tm, tn), lambda i, j: (i, j)),
        out_shape=jax.ShapeDtypeStruct((t, n), F32),
        scratch_shapes=[pltpu.VMEM((tm, d), _MXU_DTYPE)],
        compiler_params=_params(("parallel", "arbitrary")),
        name="norm_proj",
    )(x2d, gain, w)


def _gated_out_kernel(o_ref, g_ref, gain_ref, w_ref, x_ref, y_ref, a_ref, *, dv):
    vd = o_ref.shape[1]
    for hd in range(vd // dv):
        cols = slice(hd * dv, (hd + 1) * dv)
        o = o_ref[:, cols]
        g = g_ref[:, cols]
        a_ref[:, cols] = (_rms(o, gain_ref[:, cols]) * (g * jax.nn.sigmoid(g))).astype(a_ref.dtype)
    y_ref[...] = x_ref[...] + jnp.dot(a_ref[...], w_ref[...], preferred_element_type=F32)


def _gated_out(o, p, g_block, gain, w, x2d, *, dv, tm):
    t, vd = o.shape
    d = x2d.shape[1]
    return pl.pallas_call(
        functools.partial(_gated_out_kernel, dv=dv),
        grid=(t // tm,),
        in_specs=[
            pl.BlockSpec((tm, vd), lambda i: (i, 0)),
            pl.BlockSpec((tm, vd), lambda i: (i, g_block)),
            pl.BlockSpec((1, vd), lambda i: (0, 0)),
            pl.BlockSpec((vd, d), lambda i: (0, 0)),
            pl.BlockSpec((tm, d), lambda i: (i, 0)),
        ],
        out_specs=pl.BlockSpec((tm, d), lambda i: (i, 0)),
        out_shape=jax.ShapeDtypeStruct((t, d), F32),
        scratch_shapes=[pltpu.VMEM((tm, vd), _MXU_DTYPE)],
        compiler_params=_params(("parallel",)),
        name="gated_out",
    )(o, p, gain, w, x2d)


_ROPE_THETA = 10000.0


def _rope_kernel(pos_ref, cr_ref, sr_ref, ta_ref, *, rt_half, at_half):
    pos = pos_ref[...]
    lane = lax.broadcasted_iota(jnp.int32, pos.shape, 1)
    inv = jnp.exp(lane.astype(F32) * (-math.log(_ROPE_THETA) / rt_half))
    ang = pos * inv
    cr_ref[...] = jnp.cos(ang)
    sr_ref[...] = jnp.sin(ang)
    j = lane % (2 * at_half)
    inv = jnp.exp((j % at_half).astype(F32) * (-math.log(_ROPE_THETA) / at_half))
    ang = pos * inv
    sn = jnp.sin(ang)
    ta_ref[:, :_LANES] = jnp.cos(ang)
    ta_ref[:, _LANES:2 * _LANES] = jnp.where(j < at_half, -sn, 0.0)
    ta_ref[:, 2 * _LANES:] = jnp.where(j < at_half, 0.0, sn)


def _rope_tables(pos_b, *, rt_half, at_half, tm):
    t = pos_b.shape[0]
    spec = pl.BlockSpec((tm, _LANES), lambda i: (i, 0))
    return pl.pallas_call(
        functools.partial(_rope_kernel, rt_half=rt_half, at_half=at_half),
        grid=(t // tm,),
        in_specs=[spec],
        out_specs=[spec, spec, pl.BlockSpec((tm, 3 * _LANES), lambda i: (i, 0))],
        out_shape=[jax.ShapeDtypeStruct((t, _LANES), F32)] * 2 + [jax.ShapeDtypeStruct((t, 3 * _LANES), F32)],
        compiler_params=_params(("parallel",)),
        name="rope_tables",
    )(pos_b)


_HG_CHUNK = 64
_HG_SUB = 16
_TINY = 1e-30
_TN = (((0,), (0,)), ((), ()))


def _split_dot(a, b_f32):
    hi = b_f32.astype(BF16)
    lo = (b_f32 - hi.astype(F32)).astype(BF16)
    return (jnp.dot(a, hi, preferred_element_type=F32) + jnp.dot(a, lo, preferred_element_type=F32))


def _hgrn_kernel(q_ref, f_ref, i_ref, lb_ref, o_ref, st_ref, *, layer, heads):
    c, sc = _HG_CHUNK, _HG_SUB
    nchunk = q_ref.shape[0] // c
    dk = q_ref.shape[1] // heads

    @pl.when(pl.program_id(1) == 0)
    def _():
        st_ref[...] = jnp.zeros_like(st_ref)

    lbs = lb_ref[...]
    e = jnp.exp(lbs - jnp.max(lbs, axis=0, keepdims=True))
    soft = e / jnp.sum(e, axis=0, keepdims=True)
    lb_all = jnp.zeros_like(soft[0:1])
    for l in range(1, layer + 1):
        lb_all = lb_all + soft[l:l + 1]

    r_i = lax.broadcasted_iota(jnp.int32, (c, c), 0)
    c_i = lax.broadcasted_iota(jnp.int32, (c, c), 1)
    tri = (r_i >= c_i).astype(BF16)
    sub_t = lax.broadcasted_iota(jnp.int32, (sc, 1), 0)

    def head_chunk(rows, hd):
        cols = slice(hd * dk, (hd + 1) * dk)
        lb = lb_all[:, cols]
        qr = q_ref[rows, cols]
        fr = f_ref[rows, cols]
        v = i_ref[rows, cols]
        q = qr * jax.nn.sigmoid(qr)
        forget = lb + (1.0 - lb) * jax.nn.sigmoid(fr)
        logf = jnp.log(jnp.maximum(forget, _TINY))
        k = (1.0 - lb) * jax.nn.sigmoid(-fr)
        if _MXU_DTYPE == BF16:
            b = _split_dot(tri, logf)
        else:
            b = jnp.dot(tri.astype(F32), logf, preferred_element_type=F32)
        st = st_ref[hd]
        o = lax.dot_general((q * jnp.exp(b)).astype(_MXU_DTYPE), st.astype(_MXU_DTYPE), _NT,
                            preferred_element_type=F32)
        vb = v.astype(_MXU_DTYPE)
        outs = []
        for i in range(c // sc):
            lo, hi = i * sc, (i + 1) * sc
            qi, bi = q[lo:hi], b[lo:hi]
            oi = o[lo:hi]
            if i > 0:
                ref_b = b[lo - 1:lo]
                qt = (qi * jnp.exp(bi - ref_b)).astype(_MXU_DTYPE)
                kt = (k[:lo] * jnp.exp(ref_b - b[:lo])).astype(_MXU_DTYPE)
                att = lax.dot_general(qt, kt, _NT, preferred_element_type=F32)
                oi = oi + jnp.dot(att.astype(_MXU_DTYPE), vb[:lo], preferred_element_type=F32)
            for s in range(sc):
                dec = jnp.exp(bi - bi[s:s + 1])
                col = jnp.sum(qi * k[lo + s:lo + s + 1] * dec, axis=1, keepdims=True)
                col = jnp.where(sub_t >= s, col, 0.0)
                oi = oi + col * v[lo + s:lo + s + 1]
            outs.append(oi)
        o_ref[rows, cols] = jnp.concatenate(outs, axis=0)
        b_last = b[c - 1:c]
        kd = (k * jnp.exp(b_last - b)).astype(_MXU_DTYPE)
        st_ref[hd] = jnp.exp(b_last) * st + lax.dot_general(vb, kd, _TN, preferred_element_type=F32)

    def chunk(ci, carry):
        rows = pl.ds(pl.multiple_of(ci * c, c), c)
        for hd in range(heads):
            head_chunk(rows, hd)
        return carry

    lax.fori_loop(0, nchunk, chunk, 0)


def _hgrn_scan(p, hg_lb, *, batch, seq, heads, dk, layer, blk):
    t = p.shape[0]
    nblk = seq // blk
    width = heads * dk
    row = lambda b, s: b * nblk + s
    return pl.pallas_call(
        functools.partial(_hgrn_kernel, layer=layer, heads=heads),
        grid=(batch, nblk),
        in_specs=[
            pl.BlockSpec((blk, width), lambda b, s: (row(b, s), 0)),
            pl.BlockSpec((blk, width), lambda b, s: (row(b, s), 1)),
            pl.BlockSpec((blk, width), lambda b, s: (row(b, s), 2)),
            pl.BlockSpec((hg_lb.shape[0], width), lambda b, s: (0, 0)),
        ],
        out_specs=pl.BlockSpec((blk, width), lambda b, s: (row(b, s), 0)),
        out_shape=jax.ShapeDtypeStruct((t, width), F32),
        scratch_shapes=[pltpu.VMEM((heads, dk, dk), F32)],
        compiler_params=_params(("parallel", "arbitrary")),
        name="hgrn_scan",
    )(p, p, p, hg_lb)


_RT_CHUNK = 128


def _retention_kernel(q_ref, k_ref, v_ref, cos_ref, sin_ref, o_ref, st_ref, *, heads):
    c = _RT_CHUNK
    nchunk = q_ref.shape[0] // c
    dk = q_ref.shape[1] // heads
    dv = v_ref.shape[1] // heads
    half = dk // 2

    @pl.when(pl.program_id(1) == 0)
    def _():
        st_ref[...] = jnp.zeros_like(st_ref)

    t_i = lax.broadcasted_iota(jnp.int32, (c, c), 0)
    s_i = lax.broadcasted_iota(jnp.int32, (c, c), 1)
    rel = (t_i - s_i).astype(F32)
    idx = lax.broadcasted_iota(jnp.int32, (c, _LANES), 0).astype(F32)
    consts = []
    for hd in range(heads):
        lg = math.log1p(-(2.0 ** (-5.0 - hd)))
        decay = jnp.where(rel >= 0, jnp.exp(lg * jnp.maximum(rel, 0.0)), 0.0)
        consts.append((decay, jnp.exp(lg * (idx + 1.0)), jnp.exp(lg * (c - 1.0 - idx)), math.exp(lg * c)))

    def rope(x, cos, sin):
        x1, x2 = x[:, :half], x[:, half:]
        return x1 * cos - x2 * sin, x2 * cos + x1 * sin

    cat = lambda a, b: jnp.concatenate([a, b], axis=1).astype(_MXU_DTYPE)

    def head_chunk(rows, hd, cos, sin):
        decay, q_dec, k_dec, chunk_dec = consts[hd]
        q1, q2 = rope(q_ref[rows, hd * dk:(hd + 1) * dk], cos, sin)
        k1, k2 = rope(k_ref[rows, hd * dk:(hd + 1) * dk], cos, sin)
        scale = dk ** -0.5
        k1, k2 = k1 * scale, k2 * scale
        qb, kb = cat(q1, q2), cat(k1, k2)
        vb = v_ref[rows, hd * dv:(hd + 1) * dv].astype(_MXU_DTYPE)
        att = lax.dot_general(qb, kb, _NT, preferred_element_type=F32) * decay
        o = jnp.dot(att.astype(_MXU_DTYPE), vb, preferred_element_type=F32)
        st = st_ref[hd]
        o = o + jnp.dot(cat(q1 * q_dec, q2 * q_dec), st.astype(_MXU_DTYPE), preferred_element_type=F32)
        o_ref[rows, hd * dv:(hd + 1) * dv] = o
        st_ref[hd] = chunk_dec * st + lax.dot_general(cat(k1 * k_dec, k2 * k_dec), vb, _TN,
                                                      preferred_element_type=F32)

    def chunk(ci, carry):
        rows = pl.ds(pl.multiple_of(ci * c, c), c)
        cos, sin = cos_ref[rows, :], sin_ref[rows, :]
        for hd in range(heads):
            head_chunk(rows, hd, cos, sin)
        return carry

    lax.fori_loop(0, nchunk, chunk, 0)


def _retention_scan(p, cos, sin, *, batch, seq, heads, dk, dv, blk):
    t = p.shape[0]
    nblk = seq // blk
    row = lambda b, s: b * nblk + s
    kd, vd = heads * dk, heads * dv
    assert 2 * kd == vd and dk // 2 == _LANES and _RT_CHUNK == _LANES
    return pl.pallas_call(
        functools.partial(_retention_kernel, heads=heads),
        grid=(batch, nblk),
        in_specs=[
            pl.BlockSpec((blk, kd), lambda b, s: (row(b, s), 0)),
            pl.BlockSpec((blk, kd), lambda b, s: (row(b, s), 1)),
            pl.BlockSpec((blk, vd), lambda b, s: (row(b, s), 1)),
            pl.BlockSpec((blk, _LANES), lambda b, s: (row(b, s), 0)),
            pl.BlockSpec((blk, _LANES), lambda b, s: (row(b, s), 0)),
        ],
        out_specs=pl.BlockSpec((blk, vd), lambda b, s: (row(b, s), 0)),
        out_shape=jax.ShapeDtypeStruct((t, vd), F32),
        scratch_shapes=[pltpu.VMEM((heads, dk, dv), F32)],
        compiler_params=_params(("parallel", "arbitrary")),
        name="retention_scan",
    )(p, p, p, cos, sin)


_AT_BLOCK = 128
_MASK_VALUE = -1e30


def _split_dot_lhs(a_f32, b):
    hi = a_f32.astype(BF16)
    lo = (a_f32 - hi.astype(F32)).astype(BF16)
    return (jnp.dot(hi, b, preferred_element_type=F32) + jnp.dot(lo, b, preferred_element_type=F32))


def _attn_kernel(q_ref, kp_ref, k_ref, vp_ref, v_ref, t0_ref, t1_ref, t2_ref, p0_ref, p1_ref, p2_ref,
                 gq_ref, gk_ref, o_ref, l_ref, *, head_dim, span, dil, nsub):
    blk = _AT_BLOCK
    half = head_dim // 2
    lane = lax.broadcasted_iota(jnp.int32, (1, _LANES), 1)
    qi = lax.broadcasted_iota(jnp.int32, (blk, 2 * blk), 0) + blk
    ki = lax.broadcasted_iota(jnp.int32, (blk, 2 * blk), 1)
    band = (ki <= qi) & (ki >= qi - span)
    first_lo = jnp.where(pl.program_id(2) == 0, blk, 0)
    band_first = (ki <= qi) & (ki >= jnp.maximum(qi - span, first_lo))
    scale = head_dim ** -0.5
    r_i = lax.broadcasted_iota(jnp.int32, (_LANES, _LANES), 0) // head_dim
    c_i = lax.broadcasted_iota(jnp.int32, (_LANES, _LANES), 1) // head_dim
    seg = jnp.where(r_i == c_i, 1.0 / head_dim, 0.0)
    gq, gk = gq_ref[...], gk_ref[...]

    nh = _LANES // head_dim
    gain_qk = jnp.concatenate([jnp.broadcast_to(gq, (blk, _LANES)), jnp.broadcast_to(gk, (blk, _LANES))], axis=0)
    heads_of = lane // head_dim
    segb = seg.astype(_MXU_DTYPE)

    def prep(x, gain, tab):
        msq = jnp.dot((x * x).astype(_MXU_DTYPE), segb, preferred_element_type=F32)
        y = x * lax.rsqrt(msq + _EPS) * gain
        cos, s1, s2 = tab
        y = y * cos + pltpu.roll(y, _LANES - half, 1) * s1 + pltpu.roll(y, half, 1) * s2
        return y.astype(_MXU_DTYPE)

    def residue(r, carry):
        def take(ref, j):
            if dil == 1:
                return ref[j * blk:(j + 1) * blk, :]
            return ref[pl.ds(r + j * blk * dil, blk, stride=dil), :]

        kprev = prep(take(kp_ref, 0), gk, [take(t, 0) for t in (p0_ref, p1_ref, p2_ref)])
        vprev = take(vp_ref, 0).astype(_MXU_DTYPE)
        for j in range(nsub):
            tab = [take(t, j) for t in (t0_ref, t1_ref, t2_ref)]
            qk = prep(jnp.concatenate([take(q_ref, j), take(k_ref, j)], axis=0), gain_qk,
                      [jnp.concatenate([t, t], axis=0) for t in tab])
            q, kcur = qk[:blk], qk[blk:]
            vcur = take(v_ref, j).astype(_MXU_DTYPE)
            kk = jnp.concatenate([kprev, kcur], axis=0)
            vv = jnp.concatenate([vprev, vcur], axis=0)
            mask = band_first if j == 0 else band
            qh = jnp.concatenate([jnp.where(heads_of == hd, q, jnp.zeros_like(q)) for hd in range(nh)], axis=0)
            s = lax.dot_general(qh, kk, _NT, preferred_element_type=F32) * scale
            s = jnp.where(jnp.concatenate([mask] * nh, axis=0), s, _MASK_VALUE)
            m = jnp.max(s, axis=-1, keepdims=True)
            pr = jnp.exp(s - m)
            l = jnp.sum(pr, axis=-1, keepdims=True)
            o = jnp.dot(pr.astype(_MXU_DTYPE), vv, preferred_element_type=F32) / l
            lse = m + jnp.log(l)
            o_pair = o[:blk]
            l_pair = jnp.broadcast_to(lse[:blk], o_pair.shape)
            for hd in range(1, nh):
                o_pair = jnp.where(heads_of == hd, o[hd * blk:(hd + 1) * blk], o_pair)
                l_pair = jnp.where(heads_of == hd, lse[hd * blk:(hd + 1) * blk], l_pair)
            if dil == 1:
                o_ref[j * blk:(j + 1) * blk, :] = o_pair
                l_ref[j * blk:(j + 1) * blk, :] = l_pair
            else:
                o_ref[pl.ds(r + j * blk * dil, blk, stride=dil), :] = o_pair
                l_ref[pl.ds(r + j * blk * dil, blk, stride=dil), :] = l_pair
            kprev, vprev = kcur, vcur
        return carry

    if dil == 1:
        residue(0, 0)
    else:
        def two(i, carry):
            residue(2 * i, carry)
            return residue(2 * i + 1, carry)

        lax.fori_loop(0, dil // 2, two, 0)


def _attn_group(p, tabs, gq, gk, *, batch, seq, group, ngroup, heads, head_dim, window, dilation):
    t = p.shape[0]
    d = dilation
    width = heads * head_dim
    assert seq % (d * _AT_BLOCK) == 0 and window // d <= _AT_BLOCK
    nb = seq // d // _AT_BLOCK
    nsub = max(1, min(4, nb, 2048 // (_AT_BLOCK * d)))
    nstep = nb // nsub
    rows = nsub * _AT_BLOCK * d
    prow = _AT_BLOCK * d
    pairs = width // _LANES
    col = lambda kind, hp: (kind * ngroup + group) * pairs + hp
    main = lambda b, n: b * nstep + n
    prev = lambda b, n: jnp.maximum((b * nstep + n) * nsub - 1, 0)
    out_spec = pl.BlockSpec((rows, _LANES), lambda b, hp, n: (main(b, n), hp))
    return pl.pallas_call(
        functools.partial(_attn_kernel, head_dim=head_dim, span=window // d, dil=d, nsub=nsub),
        grid=(batch, pairs, nstep),
        in_specs=[
            pl.BlockSpec((rows, _LANES), lambda b, hp, n: (main(b, n), col(0, hp))),
            pl.BlockSpec((prow, _LANES), lambda b, hp, n: (prev(b, n), col(1, hp))),
            pl.BlockSpec((rows, _LANES), lambda b, hp, n: (main(b, n), col(1, hp))),
            pl.BlockSpec((prow, _LANES), lambda b, hp, n: (prev(b, n), col(2, hp))),
            pl.BlockSpec((rows, _LANES), lambda b, hp, n: (main(b, n), col(2, hp))),
            pl.BlockSpec((rows, _LANES), lambda b, hp, n: (main(b, n), 0)),
            pl.BlockSpec((rows, _LANES), lambda b, hp, n: (main(b, n), 1)),
            pl.BlockSpec((rows, _LANES), lambda b, hp, n: (main(b, n), 2)),
            pl.BlockSpec((prow, _LANES), lambda b, hp, n: (prev(b, n), 0)),
            pl.BlockSpec((prow, _LANES), lambda b, hp, n: (prev(b, n), 1)),
            pl.BlockSpec((prow, _LANES), lambda b, hp, n: (prev(b, n), 2)),
            pl.BlockSpec((1, _LANES), lambda b, hp, n: (0, 0)),
            pl.BlockSpec((1, _LANES), lambda b, hp, n: (0, 0)),
        ],
        out_specs=[out_spec, out_spec],
        out_shape=[jax.ShapeDtypeStruct((t, width), F32)] * 2,
        compiler_params=_params(("parallel", "parallel", "arbitrary")),
        name=f"attn_group{group}",
    )(p, p, p, p, p, tabs, tabs, tabs, tabs, tabs, tabs, gq, gk)


def _attn_out_kernel(*refs, ngroup):
    o_refs, l_refs = refs[:ngroup], refs[ngroup:2 * ngroup]
    w_ref, x_ref, y_ref = refs[2 * ngroup:]
    ls = [r[...] for r in l_refs]
    m = functools.reduce(jnp.maximum, ls)
    es = [jnp.exp(l - m) for l in ls]
    den = functools.reduce(lambda a, b: a + b, es)
    o = functools.reduce(lambda a, b: a + b, [e * r[...] for e, r in zip(es, o_refs)]) / den
    y_ref[...] = x_ref[...] + jnp.dot(o.astype(_MXU_DTYPE), w_ref[...], preferred_element_type=F32)


def _attn_out(os_, ls_, w, x2d, *, tm):
    t, width = os_[0].shape
    d = x2d.shape[1]
    ng = len(os_)
    blk = pl.BlockSpec((tm, width), lambda i: (i, 0))
    return pl.pallas_call(
        functools.partial(_attn_out_kernel, ngroup=ng),
        grid=(t // tm,),
        in_specs=[blk] * (2 * ng) + [pl.BlockSpec((width, d), lambda i: (0, 0)),
                                     pl.BlockSpec((tm, d), lambda i: (i, 0))],
        out_specs=pl.BlockSpec((tm, d), lambda i: (i, 0)),
        out_shape=jax.ShapeDtypeStruct((t, d), F32),
        compiler_params=_params(("parallel",)),
        name="attn_out",
    )(*os_, *ls_, w, x2d)


_HG_HEADS, _HG_DK = 8, 128
_AT_GROUPS = ((128, 1), (512, 4), (2048, 16))
_AT_HEADS, _AT_HEAD_DIM = 8, 64
_RT_HEADS = 4
_N_MIXERS = 3


def _pick(n, candidates):
    for c in candidates:
        if n % c == 0:
            return c
    return n


def _hgrn_mixer(x2d, gain, w_in, hg_lb, o_gain, w_out, *, batch, seq, layer, blk):
    t = x2d.shape[0]
    p = _norm_proj(x2d, gain, w_in, tm=_pick(t, (1024, 512, 256, 128)), tn=_pick(w_in.shape[1], (1024, 512)))
    o = _hgrn_scan(p, hg_lb, batch=batch, seq=seq, heads=_HG_HEADS, dk=_HG_DK, layer=layer, blk=min(blk, seq))
    return _gated_out(o, p, 3, o_gain, w_out, x2d, dv=_HG_DK, tm=_pick(t, (256, 128)))


def _retention_mixer(x2d, gain, w_in, o_gain, w_out, cos, sin, *, batch, seq, blk):
    t, d = x2d.shape
    dk = d // _RT_HEADS
    dv = 2 * dk
    assert dk == 2 * _LANES
    p = _norm_proj(x2d, gain, w_in, tm=_pick(t, (1024, 512, 256, 128)), tn=_pick(w_in.shape[1], (1536, 1024, 512)))
    o = _retention_scan(p, cos, sin, batch=batch, seq=seq, heads=_RT_HEADS, dk=dk, dv=dv, blk=min(blk, seq))
    return _gated_out(o, p, 2, o_gain, w_out, x2d, dv=dv, tm=_pick(t, (256, 128)))


def _attn_mixer(x2d, gain, w_in, q_gain, k_gain, w_out, tabs, *, batch, seq):
    t = x2d.shape[0]
    ng = len(_AT_GROUPS)
    p = _norm_proj(x2d, gain, w_in, tm=_pick(t, (1024, 512, 256, 128)), tn=_pick(w_in.shape[1], (1536, 512)))
    os_, ls_ = [], []
    for g, (window, dilation) in enumerate(_AT_GROUPS):
        gq = jnp.tile(q_gain[g], _LANES // _AT_HEAD_DIM)[None, :]
        gk = jnp.tile(k_gain[g], _LANES // _AT_HEAD_DIM)[None, :]
        o, lse = _attn_group(p, tabs, gq, gk, batch=batch, seq=seq, group=g, ngroup=ng, heads=_AT_HEADS,
                             head_dim=_AT_HEAD_DIM, window=window, dilation=dilation)
        os_.append(o)
        ls_.append(lse)
    return _attn_out(os_, ls_, w_out, x2d, tm=_pick(t, (512, 256, 128)))


def kernel(x, positions, mix_norm, ffn_norm, hg_lb, hg_w_in, hg_onorm, hg_w_out, at_w_in, at_qnorm, at_knorm,
           at_w_out, rt_w_in, rt_onorm, rt_w_out, pk_w_q, pk_keys, pk_u, pk_v):
    b, s, d = x.shape
    depth = mix_norm.shape[0]
    x2d = x.reshape(b * s, d)
    pos_b = jnp.broadcast_to(positions.reshape(b * s, 1).astype(F32), (b * s, _LANES))
    rt_dk = d // _RT_HEADS
    cos_r, sin_r, tabs_a = _rope_tables(pos_b, rt_half=rt_dk // 2, at_half=_AT_HEAD_DIM // 2, tm=512)
    ia = ib = ic = 0
    for layer in range(depth):
        kind = layer % _N_MIXERS
        gain = mix_norm[layer][None, :]
        if kind == 0:
            x2d = _hgrn_mixer(x2d, gain, hg_w_in[ia].astype(_MXU_DTYPE), hg_lb, hg_onorm[ia][None, :],
                              hg_w_out[ia].astype(_MXU_DTYPE), batch=b, seq=s, layer=layer, blk=512)
            ia += 1
        elif kind == 1:
            x2d = _attn_mixer(x2d, gain, at_w_in[ib].astype(_MXU_DTYPE), at_qnorm[ib], at_knorm[ib],
                              at_w_out[ib].astype(_MXU_DTYPE), tabs_a, batch=b, seq=s)
            ib += 1
        else:
            x2d = _retention_mixer(x2d, gain, rt_w_in[ic].astype(_MXU_DTYPE), rt_onorm[ic][None, :],
                                   rt_w_out[ic].astype(_MXU_DTYPE), cos_r, sin_r, batch=b, seq=s, blk=512)
            ic += 1
        keys = pk_keys[layer].reshape(-1, pk_keys.shape[-2], pk_keys.shape[-1]).astype(_MXU_DTYPE)
        x2d = _peer_layer(x2d, ffn_norm[layer][None, :], pk_w_q[layer].astype(_MXU_DTYPE), keys,
                          pk_u[layer].astype(_MXU_DTYPE), pk_v[layer].T.astype(_MXU_DTYPE),
                          tb_route=512, tb_dense=512, a_t=16)
    return x2d.reshape(b, s, d)
```

```python
import functools
import math

import jax
import jax.numpy as jnp
from jax import lax
from jax.experimental import pallas as pl
from jax.experimental.pallas import tpu as pltpu

F32 = jnp.float32
BF16 = jnp.bfloat16
_MXU_DTYPE = BF16
_EPS = 1e-6
_LANES = 128
_SUBLANES = 8
_VMEM_LIMIT = 56 * 1024 * 1024

_NT = (((1,), (1,)), ((), ()))


def _params(sem, flags=None):
    return pltpu.CompilerParams(dimension_semantics=sem, vmem_limit_bytes=_VMEM_LIMIT, flags=flags)


def _rms(x, gain):
    return x * lax.rsqrt(jnp.mean(x * x, axis=-1, keepdims=True) + _EPS) * gain


def _merge_desc(c):
    c = list(c)
    n = len(c)
    j = n // 2
    while j >= 1:
        for i in range(n):
            l = i ^ j
            if l > i:
                hi = jnp.maximum(c[i], c[l])
                lo = jnp.minimum(c[i], c[l])
                c[i], c[l] = hi, lo
        j //= 2
    return c


def _sort_desc(xs):
    xs = list(xs)
    n = len(xs)
    k = 2
    while k <= n:
        j = k // 2
        while j >= 1:
            for i in range(n):
                l = i ^ j
                if l > i:
                    hi = jnp.maximum(xs[i], xs[l])
                    lo = jnp.minimum(xs[i], xs[l])
                    if (i & k) == 0:
                        xs[i], xs[l] = hi, lo
                    else:
                        xs[i], xs[l] = lo, hi
            j //= 2
        k *= 2
    return xs


def _top_of_two(a, b):
    n = len(a)
    return _merge_desc([jnp.maximum(a[i], b[n - 1 - i]) for i in range(n)])


_PK_TOPK = 16
_PK_PAIRS = [(i, j) for i in range(_PK_TOPK) for j in range(_PK_TOPK) if (i + 1) * (j + 1) <= _PK_TOPK]


_NEG = -3.0e38


def _top16_sums(a_top, b_top):
    cand = [a_top[i] + b_top[j] for (i, j) in _PK_PAIRS]
    pad = jnp.full_like(cand[0], _NEG)
    cand = cand + [pad] * (64 - len(cand))
    groups = [_sort_desc(cand[16 * g:16 * (g + 1)]) for g in range(4)]
    return _top_of_two(_top_of_two(groups[0], groups[1]), _top_of_two(groups[2], groups[3]))


def _gate_word(x, dtype):
    if dtype == F32:
        return x
    u = lax.bitcast_convert_type(x.astype(BF16).astype(F32), jnp.uint32)
    return u | (u >> 16)


def _gate_rows(ref, hd, a, lanes, wdt):
    row = jnp.broadcast_to(ref[hd, pl.ds(a, 1), lanes], (_SUBLANES, _LANES))
    return row if wdt == F32 else pltpu.bitcast(row, wdt)


def _peer_route_kernel(x_ref, g_ref, wq_ref, keys_ref, h_ref, ea_ref, ca_ref, eb_ref, rb_ref,
                       s_scr, top_scr, btop_scr, *, heads, nkeys):
    tb = x_ref.shape[0]
    hf = _rms(x_ref[...], g_ref[...])
    hb = hf.astype(_MXU_DTYPE)
    h_ref[...] = hf.T.astype(_MXU_DTYPE)
    q = jnp.dot(hb, wq_ref[...], preferred_element_type=F32)
    for hc in range(2 * heads):
        qs = q[:, hc * nkeys:(hc + 1) * nkeys].astype(_MXU_DTYPE)
        s_scr[hc] = lax.dot_general(keys_ref[hc], qs, _NT, preferred_element_type=F32)

    nrow = nkeys // _SUBLANES
    bcast = lambda row: jnp.broadcast_to(row, (_SUBLANES, _LANES))

    def group(gi, carry):
        lanes = pl.ds(pl.multiple_of(gi * _LANES, _LANES), _LANES)
        for hc in range(2 * heads):
            hd, c = divmod(hc, 2)
            rows = [s_scr[hc, pl.ds(_SUBLANES * m, _SUBLANES), lanes] for m in range(nrow)]
            srt = _sort_desc(rows)
            for shift in (4, 2, 1):
                srt = _top_of_two(srt, [pltpu.roll(r, shift, 0) for r in srt])
            for i in range(_PK_TOPK):
                top_scr[c, i, pl.ds(hd, 1), :] = srt[i][0:1, :]
            if c == 1:
                for i in range(_PK_TOPK):
                    btop_scr[hd, i] = srt[i]
                ebs, rbs = [], []
                for m in range(nrow):
                    ebs.append(jnp.exp(rows[m] - srt[0]))
                    rank = jnp.full_like(rows[m], float(_PK_TOPK))
                    for i in reversed(range(_PK_TOPK)):
                        rank = jnp.where(srt[i] <= rows[m], float(i), rank)
                    rbs.append(rank)
                for m in range(0, nrow, 2):
                    sl = pl.ds(_SUBLANES * m, 2 * _SUBLANES)
                    eb_ref[hd, sl, lanes] = jnp.concatenate(ebs[m:m + 2], axis=0).astype(eb_ref.dtype)
                    rb_ref[hd, sl, lanes] = jnp.concatenate(rbs[m:m + 2], axis=0).astype(rb_ref.dtype)
        a_s = [top_scr[0, i] for i in range(_PK_TOPK)]
        b_s = [top_scr[1, i] for i in range(_PK_TOPK)]
        top = _top16_sums(a_s, b_s)
        z = jnp.exp(top[0] - top[0])
        for t in top[1:]:
            z = z + jnp.exp(t - top[0])
        rz = 1.0 / z
        theta = top[_PK_TOPK - 1]
        for hd in range(heads):
            amax = bcast(a_s[0][hd:hd + 1, :])
            rz_h = bcast(rz[hd:hd + 1, :])
            th_h = bcast(theta[hd:hd + 1, :])
            for m in range(nrow):
                sl = pl.ds(_SUBLANES * m, _SUBLANES)
                sa = s_scr[2 * hd, sl, lanes]
                ea_ref[hd, sl, lanes] = _gate_word(jnp.exp(sa - amax) * rz_h, ea_ref.dtype)
                cnt = jnp.full_like(sa, float(_PK_TOPK))
                for j in reversed(range(_PK_TOPK)):
                    cnt = jnp.where(sa + btop_scr[hd, j] < th_h, float(j), cnt)
                ca_ref[hd, sl, lanes] = _gate_word(cnt, ca_ref.dtype)
        return carry

    lax.fori_loop(0, tb // _LANES, group, 0)


def _peer_route(x2d, gain, wq, keys, *, tb):
    t, d = x2d.shape
    hc, nkeys, dk = keys.shape
    heads = hc // 2
    kern = functools.partial(_peer_route_kernel, heads=heads, nkeys=nkeys)
    tab = pl.BlockSpec((heads, nkeys, tb), lambda i: (0, 0, i))
    word = F32 if _MXU_DTYPE == F32 else jnp.uint32
    return pl.pallas_call(
        kern,
        grid=(t // tb,),
        in_specs=[
            pl.BlockSpec((tb, d), lambda i: (i, 0)),
            pl.BlockSpec((1, d), lambda i: (0, 0)),
            pl.BlockSpec(wq.shape, lambda i: (0, 0)),
            pl.BlockSpec(keys.shape, lambda i: (0, 0, 0)),
        ],
        out_specs=[pl.BlockSpec((d, tb), lambda i: (0, i)), tab, tab, tab, tab],
        out_shape=[
            jax.ShapeDtypeStruct((d, t), _MXU_DTYPE),
            jax.ShapeDtypeStruct((heads, nkeys, t), word),
            jax.ShapeDtypeStruct((heads, nkeys, t), word),
            jax.ShapeDtypeStruct((heads, nkeys, t), _MXU_DTYPE),
            jax.ShapeDtypeStruct((heads, nkeys, t), _MXU_DTYPE),
        ],
        scratch_shapes=[
            pltpu.VMEM((hc, nkeys, tb), F32),
            pltpu.VMEM((2, _PK_TOPK, heads, _LANES), F32),
            pltpu.VMEM((heads, _PK_TOPK, _SUBLANES, _LANES), F32),
        ],
        compiler_params=_params(("parallel",)),
        name="peer_route",
    )(x2d, gain, wq, keys)


def _gelu(x):
    return 0.5 * x * (1.0 + lax.erf(x * (1.0 / math.sqrt(2.0))))


def _peer_dense_kernel(h_ref, u_ref, vt_ref, ea_ref, ca_ref, eb_ref, rb_ref, x_ref, o_ref, acc_ref,
                       hu_ref, w_ref, *, heads, nkeys):
    e = pl.program_id(1)
    tb = h_ref.shape[1]
    a_t = ea_ref.shape[1]
    wdt = vt_ref.dtype
    pk = _SUBLANES * (4 // jnp.dtype(wdt).itemsize)
    zero = jnp.zeros((pk, _LANES), wdt)
    mc = _pick(tb, (256,))

    @pl.when(e == 0)
    def _():
        acc_ref[...] = jnp.zeros_like(acc_ref)

    mr = 2 * nkeys
    for c in range(tb // mc):
        toks = slice(c * mc, (c + 1) * mc)
        for r in range(a_t * nkeys // mr):
            rows = slice(r * mr, (r + 1) * mr)
            hu_ref[rows, toks] = jnp.dot(u_ref[rows, :], h_ref[:, toks], preferred_element_type=F32)

    for c in range(tb // mc):
        toks = slice(c * mc, (c + 1) * mc)
        for a0 in range(0, a_t, 2):
            for tc in range(mc // _LANES):
                lanes = slice(c * mc + tc * _LANES, c * mc + (tc + 1) * _LANES)
                accs = [[None] * (nkeys // pk) for _ in range(2)]
                for hd in range(heads):
                    ea_b = [_gate_rows(ea_ref, hd, a0 + i, lanes, wdt) for i in range(2)]
                    ca_b = [_gate_rows(ca_ref, hd, a0 + i, lanes, wdt) for i in range(2)]
                    for bc in range(nkeys // pk):
                        rows = slice(bc * pk, (bc + 1) * pk)
                        rbv, ebv = rb_ref[hd, rows, lanes], eb_ref[hd, rows, lanes]
                        for i in range(2):
                            term = ea_b[i] * lax.clamp(zero, ca_b[i] - rbv, ebv)
                            accs[i][bc] = term if accs[i][bc] is None else accs[i][bc] + term
                for i in range(2):
                    for bc in range(nkeys // pk):
                        erows = slice((a0 + i) * nkeys + bc * pk, (a0 + i) * nkeys + (bc + 1) * pk)
                        w_ref[erows, lanes] = accs[i][bc] * _gelu(hu_ref[erows, lanes].astype(wdt))
        acc_ref[:, toks] += jnp.dot(vt_ref[...], w_ref[:, toks], preferred_element_type=F32)

    @pl.when(e == pl.num_programs(1) - 1)
    def _():
        o_ref[...] = x_ref[...] + acc_ref[...].T


def _peer_dense(h, u, vt, ea, ca, eb, rb, x2d, *, tb, a_t):
    t, d = x2d.shape
    heads, nkeys, _ = ea.shape
    e_t = a_t * nkeys
    n_e = u.shape[0] // e_t
    kern = functools.partial(_peer_dense_kernel, heads=heads, nkeys=nkeys)
    gate_a = pl.BlockSpec((heads, a_t, tb), lambda i, e: (0, e, i))
    gate_b = pl.BlockSpec((heads, nkeys, tb), lambda i, e: (0, 0, i))
    return pl.pallas_call(
        kern,
        grid=(t // tb, n_e),
        in_specs=[
            pl.BlockSpec((d, tb), lambda i, e: (0, i)),
            pl.BlockSpec((e_t, d), lambda i, e: (e, 0)),
            pl.BlockSpec((d, e_t), lambda i, e: (0, e)),
            gate_a, gate_a, gate_b, gate_b,
            pl.BlockSpec((tb, d), lambda i, e: (i, 0)),
        ],
        out_specs=pl.BlockSpec((tb, d), lambda i, e: (i, 0)),
        out_shape=jax.ShapeDtypeStruct((t, d), F32),
        scratch_shapes=[pltpu.VMEM((d, tb), F32), pltpu.VMEM((e_t, tb), F32), pltpu.VMEM((e_t, tb), _MXU_DTYPE)],
        compiler_params=_params(("parallel", "arbitrary")),
        name="peer_dense",
    )(h, u, vt, ea, ca, eb, rb, x2d)


def _peer_layer(x2d, gain, wq, keys, u, vt, *, tb_route, tb_dense, a_t):
    h, ea, ca, eb, rb = _peer_route(x2d, gain, wq, keys, tb=tb_route)
    return _peer_dense(h, u, vt, ea, ca, eb, rb, x2d, tb=tb_dense, a_t=a_t)


def _norm_proj_kernel(x_ref, g_ref, w_ref, o_ref, hb_ref):
    @pl.when(pl.program_id(1) == 0)
    def _():
        hb_ref[...] = _rms(x_ref[...], g_ref[...]).astype(hb_ref.dtype)

    o_ref[...] = jnp.dot(hb_ref[...], w_ref[...], preferred_element_type=F32)


def _norm_proj(x2d, gain, w, *, tm, tn):
    t, d = x2d.shape
    n = w.shape[1]
    return pl.pallas_call(
        _norm_proj_kernel,
        grid=(t // tm, n // tn),
        in_specs=[
            pl.BlockSpec((tm, d), lambda i, j: (i, 0)),
            pl.BlockSpec((1, d), lambda i, j: (0, 0)),
            pl.BlockSpec((d, tn), lambda i, j: (0, j)),
        ],
        out_specs=pl.BlockSpec((tm, tn), lambda i, j: (i, j)),
        out_shape=jax.ShapeDtypeStruct((t, n), F32),
        scratch_shapes=[pltpu.VMEM((tm, d), _MXU_DTYPE)],
        compiler_params=_params(("parallel", "arbitrary")),
        name="norm_proj",
    )(x2d, gain, w)


def _gated_out_kernel(o_ref, g_ref, gain_ref, w_ref, x_ref, y_ref, a_ref, *, dv):
    vd = o_ref.shape[1]
    for hd in range(vd // dv):
        cols = slice(hd * dv, (hd + 1) * dv)
        o = o_ref[:, cols]
        g = g_ref[:, cols]
        a_ref[:, cols] = (_rms(o, gain_ref[:, cols]) * (g * jax.nn.sigmoid(g))).astype(a_ref.dtype)
    y_ref[...] = x_ref[...] + jnp.dot(a_ref[...], w_ref[...], preferred_element_type=F32)


def _gated_out(o, p, g_block, gain, w, x2d, *, dv, tm):
    t, vd = o.shape
    d = x2d.shape[1]
    return pl.pallas_call(
        functools.partial(_gated_out_kernel, dv=dv),
        grid=(t // tm,),
        in_specs=[
            pl.BlockSpec((tm, vd), lambda i: (i, 0)),
            pl.BlockSpec((tm, vd), lambda i: (i, g_block)),
            pl.BlockSpec((1, vd), lambda i: (0, 0)),
            pl.BlockSpec((vd, d), lambda i: (0, 0)),
            pl.BlockSpec((tm, d), lambda i: (i, 0)),
        ],
        out_specs=pl.BlockSpec((tm, d), lambda i: (i, 0)),
        out_shape=jax.ShapeDtypeStruct((t, d), F32),
        scratch_shapes=[pltpu.VMEM((tm, vd), _MXU_DTYPE)],
        compiler_params=_params(("parallel",)),
        name="gated_out",
    )(o, p, gain, w, x2d)


_ROPE_THETA = 10000.0


def _rope_kernel(pos_ref, cr_ref, sr_ref, ta_ref, *, rt_half, at_half):
    pos = pos_ref[...]
    lane = lax.broadcasted_iota(jnp.int32, pos.shape, 1)
    inv = jnp.exp(lane.astype(F32) * (-math.log(_ROPE_THETA) / rt_half))
    ang = pos * inv
    cr_ref[...] = jnp.cos(ang)
    sr_ref[...] = jnp.sin(ang)
    j = lane % (2 * at_half)
    inv = jnp.exp((j % at_half).astype(F32) * (-math.log(_ROPE_THETA) / at_half))
    ang = pos * inv
    sn = jnp.sin(ang)
    ta_ref[:, :_LANES] = jnp.cos(ang)
    ta_ref[:, _LANES:2 * _LANES] = jnp.where(j < at_half, -sn, 0.0)
    ta_ref[:, 2 * _LANES:] = jnp.where(j < at_half, 0.0, sn)


def _rope_tables(pos_b, *, rt_half, at_half, tm):
    t = pos_b.shape[0]
    spec = pl.BlockSpec((tm, _LANES), lambda i: (i, 0))
    return pl.pallas_call(
        functools.partial(_rope_kernel, rt_half=rt_half, at_half=at_half),
        grid=(t // tm,),
        in_specs=[spec],
        out_specs=[spec, spec, pl.BlockSpec((tm, 3 * _LANES), lambda i: (i, 0))],
        out_shape=[jax.ShapeDtypeStruct((t, _LANES), F32)] * 2 + [jax.ShapeDtypeStruct((t, 3 * _LANES), F32)],
        compiler_params=_params(("parallel",)),
        name="rope_tables",
    )(pos_b)


_HG_CHUNK = 64
_HG_SUB = 16
_TINY = 1e-30
_TN = (((0,), (0,)), ((), ()))


def _split_dot(a, b_f32):
    hi = b_f32.astype(BF16)
    lo = (b_f32 - hi.astype(F32)).astype(BF16)
    return (jnp.dot(a, hi, preferred_element_type=F32) + jnp.dot(a, lo, preferred_element_type=F32))


def _hgrn_kernel(q_ref, f_ref, i_ref, lb_ref, o_ref, st_ref, *, layer, heads):
    c, sc = _HG_CHUNK, _HG_SUB
    nchunk = q_ref.shape[0] // c
    dk = q_ref.shape[1] // heads

    @pl.when(pl.program_id(1) == 0)
    def _():
        st_ref[...] = jnp.zeros_like(st_ref)

    lbs = lb_ref[...]
    e = jnp.exp(lbs - jnp.max(lbs, axis=0, keepdims=True))
    soft = e / jnp.sum(e, axis=0, keepdims=True)
    lb_all = jnp.zeros_like(soft[0:1])
    for l in range(1, layer + 1):
        lb_all = lb_all + soft[l:l + 1]

    r_i = lax.broadcasted_iota(jnp.int32, (c, c), 0)
    c_i = lax.broadcasted_iota(jnp.int32, (c, c), 1)
    tri = (r_i >= c_i).astype(BF16)
    sub_t = lax.broadcasted_iota(jnp.int32, (sc, 1), 0)

    def head_chunk(rows, hd):
        cols = slice(hd * dk, (hd + 1) * dk)
        lb = lb_all[:, cols]
        qr = q_ref[rows, cols]
        fr = f_ref[rows, cols]
        v = i_ref[rows, cols]
        q = qr * jax.nn.sigmoid(qr)
        forget = lb + (1.0 - lb) * jax.nn.sigmoid(fr)
        logf = jnp.log(jnp.maximum(forget, _TINY))
        k = (1.0 - lb) * jax.nn.sigmoid(-fr)
        if _MXU_DTYPE == BF16:
            b = _split_dot(tri, logf)
        else:
            b = jnp.dot(tri.astype(F32), logf, preferred_element_type=F32)
        st = st_ref[hd]
        o = lax.dot_general((q * jnp.exp(b)).astype(_MXU_DTYPE), st.astype(_MXU_DTYPE), _NT,
                            preferred_element_type=F32)
        vb = v.astype(_MXU_DTYPE)
        outs = []
        for i in range(c // sc):
            lo, hi = i * sc, (i + 1) * sc
            qi, bi = q[lo:hi], b[lo:hi]
            oi = o[lo:hi]
            if i > 0:
                ref_b = b[lo - 1:lo]
                qt = (qi * jnp.exp(bi - ref_b)).astype(_MXU_DTYPE)
                kt = (k[:lo] * jnp.exp(ref_b - b[:lo])).astype(_MXU_DTYPE)
                att = lax.dot_general(qt, kt, _NT, preferred_element_type=F32)
                oi = oi + jnp.dot(att.astype(_MXU_DTYPE), vb[:lo], preferred_element_type=F32)
            for s in range(sc):
                dec = jnp.exp(bi - bi[s:s + 1])
                col = jnp.sum(qi * k[lo + s:lo + s + 1] * dec, axis=1, keepdims=True)
                col = jnp.where(sub_t >= s, col, 0.0)
                oi = oi + col * v[lo + s:lo + s + 1]
            outs.append(oi)
        o_ref[rows, cols] = jnp.concatenate(outs, axis=0)
        b_last = b[c - 1:c]
        kd = (k * jnp.exp(b_last - b)).astype(_MXU_DTYPE)
        st_ref[hd] = jnp.exp(b_last) * st + lax.dot_general(vb, kd, _TN, preferred_element_type=F32)

    def chunk(ci, carry):
        rows = pl.ds(pl.multiple_of(ci * c, c), c)
        for hd in range(heads):
            head_chunk(rows, hd)
        return carry

    lax.fori_loop(0, nchunk, chunk, 0, unroll=2)


def _hgrn_scan(p, hg_lb, *, batch, seq, heads, dk, layer, blk):
    t = p.shape[0]
    nblk = seq // blk
    width = heads * dk
    row = lambda b, s: b * nblk + s
    return pl.pallas_call(
        functools.partial(_hgrn_kernel, layer=layer, heads=heads),
        grid=(batch, nblk),
        in_specs=[
            pl.BlockSpec((blk, width), lambda b, s: (row(b, s), 0)),
            pl.BlockSpec((blk, width), lambda b, s: (row(b, s), 1)),
            pl.BlockSpec((blk, width), lambda b, s: (row(b, s), 2)),
            pl.BlockSpec((hg_lb.shape[0], width), lambda b, s: (0, 0)),
        ],
        out_specs=pl.BlockSpec((blk, width), lambda b, s: (row(b, s), 0)),
        out_shape=jax.ShapeDtypeStruct((t, width), F32),
        scratch_shapes=[pltpu.VMEM((heads, dk, dk), F32)],
        compiler_params=_params(("parallel", "arbitrary")),
        name="hgrn_scan",
    )(p, p, p, hg_lb)


_RT_CHUNK = 128


def _retention_kernel(q_ref, k_ref, v_ref, cos_ref, sin_ref, o_ref, st_ref, *, heads):
    c = _RT_CHUNK
    nchunk = q_ref.shape[0] // c
    dk = q_ref.shape[1] // heads
    dv = v_ref.shape[1] // heads
    half = dk // 2

    @pl.when(pl.program_id(1) == 0)
    def _():
        st_ref[...] = jnp.zeros_like(st_ref)

    t_i = lax.broadcasted_iota(jnp.int32, (c, c), 0)
    s_i = lax.broadcasted_iota(jnp.int32, (c, c), 1)
    rel = (t_i - s_i).astype(F32)
    idx = lax.broadcasted_iota(jnp.int32, (c, _LANES), 0).astype(F32)
    consts = []
    for hd in range(heads):
        lg = math.log1p(-(2.0 ** (-5.0 - hd)))
        decay = jnp.where(rel >= 0, jnp.exp(lg * jnp.maximum(rel, 0.0)), 0.0)
        consts.append((decay, jnp.exp(lg * (idx + 1.0)), jnp.exp(lg * (c - 1.0 - idx)), math.exp(lg * c)))

    def rope(x, cos, sin):
        x1, x2 = x[:, :half], x[:, half:]
        return x1 * cos - x2 * sin, x2 * cos + x1 * sin

    cat = lambda a, b: jnp.concatenate([a, b], axis=1).astype(_MXU_DTYPE)

    def head_chunk(rows, hd, cos, sin):
        decay, q_dec, k_dec, chunk_dec = consts[hd]
        q1, q2 = rope(q_ref[rows, hd * dk:(hd + 1) * dk], cos, sin)
        k1, k2 = rope(k_ref[rows, hd * dk:(hd + 1) * dk], cos, sin)
        scale = dk ** -0.5
        k1, k2 = k1 * scale, k2 * scale
        qb, kb = cat(q1, q2), cat(k1, k2)
        vb = v_ref[rows, hd * dv:(hd + 1) * dv].astype(_MXU_DTYPE)
        att = lax.dot_general(qb, kb, _NT, preferred_element_type=F32) * decay
        o = jnp.dot(att.astype(_MXU_DTYPE), vb, preferred_element_type=F32)
        st = st_ref[hd]
        o = o + jnp.dot(cat(q1 * q_dec, q2 * q_dec), st.astype(_MXU_DTYPE), preferred_element_type=F32)
        o_ref[rows, hd * dv:(hd + 1) * dv] = o
        st_ref[hd] = chunk_dec * st + lax.dot_general(cat(k1 * k_dec, k2 * k_dec), vb, _TN,
                                                      preferred_element_type=F32)

    def chunk(ci, carry):
        rows = pl.ds(pl.multiple_of(ci * c, c), c)
        cos, sin = cos_ref[rows, :], sin_ref[rows, :]
        for hd in range(heads):
            head_chunk(rows, hd, cos, sin)
        return carry

    lax.fori_loop(0, nchunk, chunk, 0)


def _retention_scan(p, cos, sin, *, batch, seq, heads, dk, dv, blk):
    t = p.shape[0]
    nblk = seq // blk
    row = lambda b, s: b * nblk + s
    kd, vd = heads * dk, heads * dv
    assert 2 * kd == vd and dk // 2 == _LANES and _RT_CHUNK == _LANES
    return pl.pallas_call(
        functools.partial(_retention_kernel, heads=heads),
        grid=(batch, nblk),
        in_specs=[
            pl.BlockSpec((blk, kd), lambda b, s: (row(b, s), 0)),
            pl.BlockSpec((blk, kd), lambda b, s: (row(b, s), 1)),
            pl.BlockSpec((blk, vd), lambda b, s: (row(b, s), 1)),
            pl.BlockSpec((blk, _LANES), lambda b, s: (row(b, s), 0)),
            pl.BlockSpec((blk, _LANES), lambda b, s: (row(b, s), 0)),
        ],
        out_specs=pl.BlockSpec((blk, vd), lambda b, s: (row(b, s), 0)),
        out_shape=jax.ShapeDtypeStruct((t, vd), F32),
        scratch_shapes=[pltpu.VMEM((heads, dk, dv), F32)],
        compiler_params=_params(("parallel", "arbitrary")),
        name="retention_scan",
    )(p, p, p, cos, sin)


_AT_BLOCK = 128
_MASK_VALUE = -1e30


def _split_dot_lhs(a_f32, b):
    hi = a_f32.astype(BF16)
    lo = (a_f32 - hi.astype(F32)).astype(BF16)
    return (jnp.dot(hi, b, preferred_element_type=F32) + jnp.dot(lo, b, preferred_element_type=F32))


def _attn_kernel(q_ref, kp_ref, k_ref, vp_ref, v_ref, t0_ref, t1_ref, t2_ref, p0_ref, p1_ref, p2_ref,
                 gq_ref, gk_ref, o_ref, l_ref, *, head_dim, span, dil, nsub):
    blk = _AT_BLOCK
    half = head_dim // 2
    lane = lax.broadcasted_iota(jnp.int32, (1, _LANES), 1)
    qi = lax.broadcasted_iota(jnp.int32, (blk, 2 * blk), 0) + blk
    ki = lax.broadcasted_iota(jnp.int32, (blk, 2 * blk), 1)
    band = (ki <= qi) & (ki >= qi - span)
    first_lo = jnp.where(pl.program_id(2) == 0, blk, 0)
    band_first = (ki <= qi) & (ki >= jnp.maximum(qi - span, first_lo))
    scale = head_dim ** -0.5
    r_i = lax.broadcasted_iota(jnp.int32, (_LANES, _LANES), 0) // head_dim
    c_i = lax.broadcasted_iota(jnp.int32, (_LANES, _LANES), 1) // head_dim
    seg = jnp.where(r_i == c_i, 1.0 / head_dim, 0.0)
    gq, gk = gq_ref[...], gk_ref[...]

    nh = _LANES // head_dim
    gain_qk = jnp.concatenate([jnp.broadcast_to(gq, (blk, _LANES)), jnp.broadcast_to(gk, (blk, _LANES))], axis=0)
    heads_of = lane // head_dim
    segb = seg.astype(_MXU_DTYPE)

    def prep(x, gain, tab):
        msq = jnp.dot((x * x).astype(_MXU_DTYPE), segb, preferred_element_type=F32)
        y = x * lax.rsqrt(msq + _EPS) * gain
        cos, s1, s2 = tab
        y = y * cos + pltpu.roll(y, _LANES - half, 1) * s1 + pltpu.roll(y, half, 1) * s2
        return y.astype(_MXU_DTYPE)

    def residue(r, carry):
        def take(ref, j):
            if dil == 1:
                return ref[j * blk:(j + 1) * blk, :]
            return ref[pl.ds(r + j * blk * dil, blk, stride=dil), :]

        kprev = prep(take(kp_ref, 0), gk, [take(t, 0) for t in (p0_ref, p1_ref, p2_ref)])
        vprev = take(vp_ref, 0).astype(_MXU_DTYPE)
        for j in range(nsub):
            tab = [take(t, j) for t in (t0_ref, t1_ref, t2_ref)]
            qk = prep(jnp.concatenate([take(q_ref, j), take(k_ref, j)], axis=0), gain_qk,
                      [jnp.concatenate([t, t], axis=0) for t in tab])
            q, kcur = qk[:blk], qk[blk:]
            vcur = take(v_ref, j).astype(_MXU_DTYPE)
            kk = jnp.concatenate([kprev, kcur], axis=0)
            vv = jnp.concatenate([vprev, vcur], axis=0)
            mask = band_first if j == 0 else band
            qh = jnp.concatenate([jnp.where(heads_of == hd, q, jnp.zeros_like(q)) for hd in range(nh)], axis=0)
            s = lax.dot_general(qh, kk, _NT, preferred_element_type=F32) * scale
            s = jnp.where(jnp.concatenate([mask] * nh, axis=0), s, _MASK_VALUE)
            m = jnp.max(s, axis=-1, keepdims=True)
            pr = jnp.exp(s - m)
            l = jnp.sum(pr, axis=-1, keepdims=True)
            o = jnp.dot(pr.astype(_MXU_DTYPE), vv, preferred_element_type=F32) / l
            lse = m + jnp.log(l)
            o_pair = o[:blk]
            l_pair = jnp.broadcast_to(lse[:blk], o_pair.shape)
            for hd in range(1, nh):
                o_pair = jnp.where(heads_of == hd, o[hd * blk:(hd + 1) * blk], o_pair)
                l_pair = jnp.where(heads_of == hd, lse[hd * blk:(hd + 1) * blk], l_pair)
            if dil == 1:
                o_ref[j * blk:(j + 1) * blk, :] = o_pair
                l_ref[j * blk:(j + 1) * blk, :] = l_pair
            else:
                o_ref[pl.ds(r + j * blk * dil, blk, stride=dil), :] = o_pair
                l_ref[pl.ds(r + j * blk * dil, blk, stride=dil), :] = l_pair
            kprev, vprev = kcur, vcur
        return carry

    if dil == 1:
        residue(0, 0)
    else:
        per = 2 if nsub > 1 else 4

        def several(i, carry):
            for k in range(per):
                residue(per * i + k, carry)
            return carry

        lax.fori_loop(0, dil // per, several, 0)


def _attn_group(p, tabs, gq, gk, *, batch, seq, group, ngroup, heads, head_dim, window, dilation):
    t = p.shape[0]
    d = dilation
    width = heads * head_dim
    assert seq % (d * _AT_BLOCK) == 0 and window // d <= _AT_BLOCK
    nb = seq // d // _AT_BLOCK
    nsub = max(1, min(4, nb, 2048 // (_AT_BLOCK * d)))
    nstep = nb // nsub
    rows = nsub * _AT_BLOCK * d
    prow = _AT_BLOCK * d
    pairs = width // _LANES
    col = lambda kind, hp: (kind * ngroup + group) * pairs + hp
    main = lambda b, n: b * nstep + n
    prev = lambda b, n: jnp.maximum((b * nstep + n) * nsub - 1, 0)
    out_spec = pl.BlockSpec((rows, _LANES), lambda b, hp, n: (main(b, n), hp))
    return pl.pallas_call(
        functools.partial(_attn_kernel, head_dim=head_dim, span=window // d, dil=d, nsub=nsub),
        grid=(batch, pairs, nstep),
        in_specs=[
            pl.BlockSpec((rows, _LANES), lambda b, hp, n: (main(b, n), col(0, hp))),
            pl.BlockSpec((prow, _LANES), lambda b, hp, n: (prev(b, n), col(1, hp))),
            pl.BlockSpec((rows, _LANES), lambda b, hp, n: (main(b, n), col(1, hp))),
            pl.BlockSpec((prow, _LANES), lambda b, hp, n: (prev(b, n), col(2, hp))),
            pl.BlockSpec((rows, _LANES), lambda b, hp, n: (main(b, n), col(2, hp))),
            pl.BlockSpec((rows, _LANES), lambda b, hp, n: (main(b, n), 0)),
            pl.BlockSpec((rows, _LANES), lambda b, hp, n: (main(b, n), 1)),
            pl.BlockSpec((rows, _LANES), lambda b, hp, n: (main(b, n), 2)),
            pl.BlockSpec((prow, _LANES), lambda b, hp, n: (prev(b, n), 0)),
            pl.BlockSpec((prow, _LANES), lambda b, hp, n: (prev(b, n), 1)),
            pl.BlockSpec((prow, _LANES), lambda b, hp, n: (prev(b, n), 2)),
            pl.BlockSpec((1, _LANES), lambda b, hp, n: (0, 0)),
            pl.BlockSpec((1, _LANES), lambda b, hp, n: (0, 0)),
        ],
        out_specs=[out_spec, out_spec],
        out_shape=[jax.ShapeDtypeStruct((t, width), F32)] * 2,
        compiler_params=_params(("parallel", "parallel", "arbitrary")),
        name=f"attn_group{group}",
    )(p, p, p, p, p, tabs, tabs, tabs, tabs, tabs, tabs, gq, gk)


def _attn_out_kernel(*refs, ngroup):
    o_refs, l_refs = refs[:ngroup], refs[ngroup:2 * ngroup]
    w_ref, x_ref, y_ref = refs[2 * ngroup:]
    ls = [r[...] for r in l_refs]
    m = functools.reduce(jnp.maximum, ls)
    es = [jnp.exp(l - m) for l in ls]
    den = functools.reduce(lambda a, b: a + b, es)
    o = functools.reduce(lambda a, b: a + b, [e * r[...] for e, r in zip(es, o_refs)]) / den
    y_ref[...] = x_ref[...] + jnp.dot(o.astype(_MXU_DTYPE), w_ref[...], preferred_element_type=F32)


def _attn_out(os_, ls_, w, x2d, *, tm):
    t, width = os_[0].shape
    d = x2d.shape[1]
    ng = len(os_)
    blk = pl.BlockSpec((tm, width), lambda i: (i, 0))
    return pl.pallas_call(
        functools.partial(_attn_out_kernel, ngroup=ng),
        grid=(t // tm,),
        in_specs=[blk] * (2 * ng) + [pl.BlockSpec((width, d), lambda i: (0, 0)),
                                     pl.BlockSpec((tm, d), lambda i: (i, 0))],
        out_specs=pl.BlockSpec((tm, d), lambda i: (i, 0)),
        out_shape=jax.ShapeDtypeStruct((t, d), F32),
        compiler_params=_params(("parallel",)),
        name="attn_out",
    )(*os_, *ls_, w, x2d)


_HG_HEADS, _HG_DK = 8, 128
_AT_GROUPS = ((128, 1), (512, 4), (2048, 16))
_AT_HEADS, _AT_HEAD_DIM = 8, 64
_RT_HEADS = 4
_N_MIXERS = 3


def _pick(n, candidates):
    for c in candidates:
        if n % c == 0:
            return c
    return n


def _hgrn_mixer(x2d, gain, w_in, hg_lb, o_gain, w_out, *, batch, seq, layer, blk):
    t = x2d.shape[0]
    p = _norm_proj(x2d, gain, w_in, tm=_pick(t, (1024, 512, 256, 128)), tn=_pick(w_in.shape[1], (1024, 512)))
    o = _hgrn_scan(p, hg_lb, batch=batch, seq=seq, heads=_HG_HEADS, dk=_HG_DK, layer=layer, blk=min(blk, seq))
    return _gated_out(o, p, 3, o_gain, w_out, x2d, dv=_HG_DK, tm=_pick(t, (256, 128)))


def _retention_mixer(x2d, gain, w_in, o_gain, w_out, cos, sin, *, batch, seq, blk):
    t, d = x2d.shape
    dk = d // _RT_HEADS
    dv = 2 * dk
    assert dk == 2 * _LANES
    p = _norm_proj(x2d, gain, w_in, tm=_pick(t, (1024, 512, 256, 128)), tn=_pick(w_in.shape[1], (1536, 1024, 512)))
    o = _retention_scan(p, cos, sin, batch=batch, seq=seq, heads=_RT_HEADS, dk=dk, dv=dv, blk=min(blk, seq))
    return _gated_out(o, p, 2, o_gain, w_out, x2d, dv=dv, tm=_pick(t, (256, 128)))


def _attn_mixer(x2d, gain, w_in, q_gain, k_gain, w_out, tabs, *, batch, seq):
    t = x2d.shape[0]
    ng = len(_AT_GROUPS)
    p = _norm_proj(x2d, gain, w_in, tm=_pick(t, (1024, 512, 256, 128)), tn=_pick(w_in.shape[1], (1536, 512)))
    os_, ls_ = [], []
    for g, (window, dilation) in enumerate(_AT_GROUPS):
        gq = jnp.tile(q_gain[g], _LANES // _AT_HEAD_DIM)[None, :]
        gk = jnp.tile(k_gain[g], _LANES // _AT_HEAD_DIM)[None, :]
        o, lse = _attn_group(p, tabs, gq, gk, batch=batch, seq=seq, group=g, ngroup=ng, heads=_AT_HEADS,
                             head_dim=_AT_HEAD_DIM, window=window, dilation=dilation)
        os_.append(o)
        ls_.append(lse)
    return _attn_out(os_, ls_, w_out, x2d, tm=_pick(t, (512, 256, 128)))


def kernel(x, positions, mix_norm, ffn_norm, hg_lb, hg_w_in, hg_onorm, hg_w_out, at_w_in, at_qnorm, at_knorm,
           at_w_out, rt_w_in, rt_onorm, rt_w_out, pk_w_q, pk_keys, pk_u, pk_v):
    b, s, d = x.shape
    depth = mix_norm.shape[0]
    x2d = x.reshape(b * s, d)
    pos_b = jnp.broadcast_to(positions.reshape(b * s, 1).astype(F32), (b * s, _LANES))
    rt_dk = d // _RT_HEADS
    cos_r, sin_r, tabs_a = _rope_tables(pos_b, rt_half=rt_dk // 2, at_half=_AT_HEAD_DIM // 2, tm=512)
    ia = ib = ic = 0
    for layer in range(depth):
        kind = layer % _N_MIXERS
        gain = mix_norm[layer][None, :]
        if kind == 0:
            x2d = _hgrn_mixer(x2d, gain, hg_w_in[ia].astype(_MXU_DTYPE), hg_lb, hg_onorm[ia][None, :],
                              hg_w_out[ia].astype(_MXU_DTYPE), batch=b, seq=s, layer=layer, blk=512)
            ia += 1
        elif kind == 1:
            x2d = _attn_mixer(x2d, gain, at_w_in[ib].astype(_MXU_DTYPE), at_qnorm[ib], at_knorm[ib],
                              at_w_out[ib].astype(_MXU_DTYPE), tabs_a, batch=b, seq=s)
            ib += 1
        else:
            x2d = _retention_mixer(x2d, gain, rt_w_in[ic].astype(_MXU_DTYPE), rt_onorm[ic][None, :],
                                   rt_w_out[ic].astype(_MXU_DTYPE), cos_r, sin_r, batch=b, seq=s, blk=512)
            ic += 1
        keys = pk_keys[layer].reshape(-1, pk_keys.shape[-2], pk_keys.shape[-1]).astype(_MXU_DTYPE)
        x2d = _peer_layer(x2d, ffn_norm[layer][None, :], pk_w_q[layer].astype(_MXU_DTYPE), keys,
                          pk_u[layer].astype(_MXU_DTYPE), pk_v[layer].T.astype(_MXU_DTYPE),
                          tb_route=512, tb_dense=512, a_t=16)
    return x2d.reshape(b, s, d)
```

```python
import functools
import math

import jax
import jax.numpy as jnp
from jax import lax
from jax.experimental import pallas as pl
from jax.experimental.pallas import tpu as pltpu

F32 = jnp.float32
BF16 = jnp.bfloat16
_MXU_DTYPE = BF16
_EPS = 1e-6
_LANES = 128
_SUBLANES = 8
_VMEM_LIMIT = 56 * 1024 * 1024

_NT = (((1,), (1,)), ((), ()))


def _params(sem, flags=None):
    return pltpu.CompilerParams(dimension_semantics=sem, vmem_limit_bytes=_VMEM_LIMIT, flags=flags)


def _rms(x, gain):
    return x * lax.rsqrt(jnp.mean(x * x, axis=-1, keepdims=True) + _EPS) * gain


def _merge_desc(c):
    c = list(c)
    n = len(c)
    j = n // 2
    while j >= 1:
        for i in range(n):
            l = i ^ j
            if l > i:
                hi = jnp.maximum(c[i], c[l])
                lo = jnp.minimum(c[i], c[l])
                c[i], c[l] = hi, lo
        j //= 2
    return c


def _sort_desc(xs):
    xs = list(xs)
    n = len(xs)
    k = 2
    while k <= n:
        j = k // 2
        while j >= 1:
            for i in range(n):
                l = i ^ j
                if l > i:
                    hi = jnp.maximum(xs[i], xs[l])
                    lo = jnp.minimum(xs[i], xs[l])
                    if (i & k) == 0:
                        xs[i], xs[l] = hi, lo
                    else:
                        xs[i], xs[l] = lo, hi
            j //= 2
        k *= 2
    return xs


def _top_of_two(a, b):
    n = len(a)
    return _merge_desc([jnp.maximum(a[i], b[n - 1 - i]) for i in range(n)])


_PK_TOPK = 16
_PK_PAIRS = [(i, j) for i in range(_PK_TOPK) for j in range(_PK_TOPK) if (i + 1) * (j + 1) <= _PK_TOPK]


_NEG = -3.0e38


def _top16_sums(a_top, b_top):
    cand = [a_top[i] + b_top[j] for (i, j) in _PK_PAIRS]
    pad = jnp.full_like(cand[0], _NEG)
    cand = cand + [pad] * (64 - len(cand))
    groups = [_sort_desc(cand[16 * g:16 * (g + 1)]) for g in range(4)]
    return _top_of_two(_top_of_two(groups[0], groups[1]), _top_of_two(groups[2], groups[3]))


def _gate_word(x, dtype):
    if dtype == F32:
        return x
    u = lax.bitcast_convert_type(x.astype(BF16).astype(F32), jnp.uint32)
    return u | (u >> 16)


def _gate_rows(ref, hd, a, lanes, wdt):
    row = jnp.broadcast_to(ref[hd, pl.ds(a, 1), lanes], (_SUBLANES, _LANES))
    return row if wdt == F32 else pltpu.bitcast(row, wdt)


def _peer_route_kernel(x_ref, g_ref, wq_ref, keys_ref, h_ref, ea_ref, ca_ref, eb_ref, rb_ref,
                       s_scr, top_scr, btop_scr, *, heads, nkeys):
    tb = x_ref.shape[0]
    hf = _rms(x_ref[...], g_ref[...])
    hb = hf.astype(_MXU_DTYPE)
    h_ref[...] = hf.T.astype(_MXU_DTYPE)
    q = jnp.dot(hb, wq_ref[...], preferred_element_type=F32)
    for hc in range(2 * heads):
        qs = q[:, hc * nkeys:(hc + 1) * nkeys].astype(_MXU_DTYPE)
        s_scr[hc] = lax.dot_general(keys_ref[hc], qs, _NT, preferred_element_type=F32)

    nrow = nkeys // _SUBLANES
    bcast = lambda row: jnp.broadcast_to(row, (_SUBLANES, _LANES))

    def group(gi, carry):
        lanes = pl.ds(pl.multiple_of(gi * _LANES, _LANES), _LANES)
        for hc in range(2 * heads):
            hd, c = divmod(hc, 2)
            rows = [s_scr[hc, pl.ds(_SUBLANES * m, _SUBLANES), lanes] for m in range(nrow)]
            srt = _sort_desc(rows)
            for shift in (4, 2, 1):
                srt = _top_of_two(srt, [pltpu.roll(r, shift, 0) for r in srt])
            for i in range(_PK_TOPK):
                top_scr[c, i, pl.ds(hd, 1), :] = srt[i][0:1, :]
            if c == 1:
                for i in range(_PK_TOPK):
                    btop_scr[hd, i] = srt[i]
                ebs, rbs = [], []
                for m in range(nrow):
                    ebs.append(jnp.exp(rows[m] - srt[0]))
                    rank = jnp.full_like(rows[m], float(_PK_TOPK))
                    for i in reversed(range(_PK_TOPK)):
                        rank = jnp.where(srt[i] <= rows[m], float(i), rank)
                    rbs.append(rank)
                for m in range(0, nrow, 2):
                    sl = pl.ds(_SUBLANES * m, 2 * _SUBLANES)
                    eb_ref[hd, sl, lanes] = jnp.concatenate(ebs[m:m + 2], axis=0).astype(eb_ref.dtype)
                    rb_ref[hd, sl, lanes] = jnp.concatenate(rbs[m:m + 2], axis=0).astype(rb_ref.dtype)
        a_s = [top_scr[0, i] for i in range(_PK_TOPK)]
        b_s = [top_scr[1, i] for i in range(_PK_TOPK)]
        top = _top16_sums(a_s, b_s)
        z = jnp.exp(top[0] - top[0])
        for t in top[1:]:
            z = z + jnp.exp(t - top[0])
        rz = 1.0 / z
        theta = top[_PK_TOPK - 1]
        for hd in range(heads):
            amax = bcast(a_s[0][hd:hd + 1, :])
            rz_h = bcast(rz[hd:hd + 1, :])
            th_h = bcast(theta[hd:hd + 1, :])
            for m in range(nrow):
                sl = pl.ds(_SUBLANES * m, _SUBLANES)
                sa = s_scr[2 * hd, sl, lanes]
                ea_ref[hd, sl, lanes] = _gate_word(jnp.exp(sa - amax) * rz_h, ea_ref.dtype)
                cnt = jnp.full_like(sa, float(_PK_TOPK))
                for j in reversed(range(_PK_TOPK)):
                    cnt = jnp.where(sa + btop_scr[hd, j] < th_h, float(j), cnt)
                ca_ref[hd, sl, lanes] = _gate_word(cnt, ca_ref.dtype)
        return carry

    lax.fori_loop(0, tb // _LANES, group, 0)


def _peer_route(x2d, gain, wq, keys, *, tb):
    t, d = x2d.shape
    hc, nkeys, dk = keys.shape
    heads = hc // 2
    kern = functools.partial(_peer_route_kernel, heads=heads, nkeys=nkeys)
    tab = pl.BlockSpec((heads, nkeys, tb), lambda i: (0, 0, i))
    word = F32 if _MXU_DTYPE == F32 else jnp.uint32
    return pl.pallas_call(
        kern,
        grid=(t // tb,),
        in_specs=[
            pl.BlockSpec((tb, d), lambda i: (i, 0)),
            pl.BlockSpec((1, d), lambda i: (0, 0)),
            pl.BlockSpec(wq.shape, lambda i: (0, 0)),
            pl.BlockSpec(keys.shape, lambda i: (0, 0, 0)),
        ],
        out_specs=[pl.BlockSpec((d, tb), lambda i: (0, i)), tab, tab, tab, tab],
        out_shape=[
            jax.ShapeDtypeStruct((d, t), _MXU_DTYPE),
            jax.ShapeDtypeStruct((heads, nkeys, t), word),
            jax.ShapeDtypeStruct((heads, nkeys, t), word),
            jax.ShapeDtypeStruct((heads, nkeys, t), _MXU_DTYPE),
            jax.ShapeDtypeStruct((heads, nkeys, t), _MXU_DTYPE),
        ],
        scratch_shapes=[
            pltpu.VMEM((hc, nkeys, tb), F32),
            pltpu.VMEM((2, _PK_TOPK, heads, _LANES), F32),
            pltpu.VMEM((heads, _PK_TOPK, _SUBLANES, _LANES), F32),
        ],
        compiler_params=_params(("parallel",)),
        name="peer_route",
    )(x2d, gain, wq, keys)


def _gelu(x):
    return 0.5 * x * (1.0 + lax.erf(x * (1.0 / math.sqrt(2.0))))


def _peer_dense_kernel(h_ref, u_ref, vt_ref, ea_ref, ca_ref, eb_ref, rb_ref, x_ref, o_ref, acc_ref,
                       hu_ref, w_ref, *, heads, nkeys):
    e = pl.program_id(1)
    tb = h_ref.shape[1]
    a_t = ea_ref.shape[1]
    wdt = vt_ref.dtype
    pk = _SUBLANES * (4 // jnp.dtype(wdt).itemsize)
    zero = jnp.zeros((pk, _LANES), wdt)
    mc = _pick(tb, (512, 256))

    @pl.when(e == 0)
    def _():
        acc_ref[...] = jnp.zeros_like(acc_ref)

    mr = 2 * nkeys
    for c in range(tb // mc):
        toks = slice(c * mc, (c + 1) * mc)
        for r in range(a_t * nkeys // mr):
            rows = slice(r * mr, (r + 1) * mr)
            hu_ref[rows, toks] = jnp.dot(u_ref[rows, :], h_ref[:, toks], preferred_element_type=F32)

    for c in range(tb // mc):
        toks = slice(c * mc, (c + 1) * mc)
        for a0 in range(0, a_t, 2):
            for tc in range(mc // _LANES):
                lanes = slice(c * mc + tc * _LANES, c * mc + (tc + 1) * _LANES)
                accs = [[None] * (nkeys // pk) for _ in range(2)]
                for hd in range(heads):
                    ea_b = [_gate_rows(ea_ref, hd, a0 + i, lanes, wdt) for i in range(2)]
                    ca_b = [_gate_rows(ca_ref, hd, a0 + i, lanes, wdt) for i in range(2)]
                    for bc in range(nkeys // pk):
                        rows = slice(bc * pk, (bc + 1) * pk)
                        rbv, ebv = rb_ref[hd, rows, lanes], eb_ref[hd, rows, lanes]
                        for i in range(2):
                            term = ea_b[i] * lax.clamp(zero, ca_b[i] - rbv, ebv)
                            accs[i][bc] = term if accs[i][bc] is None else accs[i][bc] + term
                for i in range(2):
                    for bc in range(nkeys // pk):
                        erows = slice((a0 + i) * nkeys + bc * pk, (a0 + i) * nkeys + (bc + 1) * pk)
                        w_ref[erows, lanes] = accs[i][bc] * _gelu(hu_ref[erows, lanes].astype(wdt))
        acc_ref[:, toks] += jnp.dot(vt_ref[...], w_ref[:, toks], preferred_element_type=F32)

    @pl.when(e == pl.num_programs(1) - 1)
    def _():
        o_ref[...] = x_ref[...] + acc_ref[...].T


def _peer_dense(h, u, vt, ea, ca, eb, rb, x2d, *, tb, a_t):
    t, d = x2d.shape
    heads, nkeys, _ = ea.shape
    e_t = a_t * nkeys
    n_e = u.shape[0] // e_t
    kern = functools.partial(_peer_dense_kernel, heads=heads, nkeys=nkeys)
    gate_a = pl.BlockSpec((heads, a_t, tb), lambda i, e: (0, e, i))
    gate_b = pl.BlockSpec((heads, nkeys, tb), lambda i, e: (0, 0, i))
    return pl.pallas_call(
        kern,
        grid=(t // tb, n_e),
        in_specs=[
            pl.BlockSpec((d, tb), lambda i, e: (0, i)),
            pl.BlockSpec((e_t, d), lambda i, e: (e, 0)),
            pl.BlockSpec((d, e_t), lambda i, e: (0, e)),
            gate_a, gate_a, gate_b, gate_b,
            pl.BlockSpec((tb, d), lambda i, e: (i, 0)),
        ],
        out_specs=pl.BlockSpec((tb, d), lambda i, e: (i, 0)),
        out_shape=jax.ShapeDtypeStruct((t, d), F32),
        scratch_shapes=[pltpu.VMEM((d, tb), F32), pltpu.VMEM((e_t, tb), F32), pltpu.VMEM((e_t, tb), _MXU_DTYPE)],
        compiler_params=_params(("parallel", "arbitrary")),
        name="peer_dense",
    )(h, u, vt, ea, ca, eb, rb, x2d)


def _peer_layer(x2d, gain, wq, keys, u, vt, *, tb_route, tb_dense, a_t):
    h, ea, ca, eb, rb = _peer_route(x2d, gain, wq, keys, tb=tb_route)
    return _peer_dense(h, u, vt, ea, ca, eb, rb, x2d, tb=tb_dense, a_t=a_t)


def _norm_proj_kernel(x_ref, g_ref, w_ref, o_ref, hb_ref):
    @pl.when(pl.program_id(1) == 0)
    def _():
        hb_ref[...] = _rms(x_ref[...], g_ref[...]).astype(hb_ref.dtype)

    o_ref[...] = jnp.dot(hb_ref[...], w_ref[...], preferred_element_type=F32)


def _norm_proj(x2d, gain, w, *, tm, tn):
    t, d = x2d.shape
    n = w.shape[1]
    return pl.pallas_call(
        _norm_proj_kernel,
        grid=(t // tm, n // tn),
        in_specs=[
            pl.BlockSpec((tm, d), lambda i, j: (i, 0)),
            pl.BlockSpec((1, d), lambda i, j: (0, 0)),
            pl.BlockSpec((d, tn), lambda i, j: (0, j)),
        ],
        out_specs=pl.BlockSpec((tm, tn), lambda i, j: (i, j)),
        out_shape=jax.ShapeDtypeStruct((t, n), F32),
        scratch_shapes=[pltpu.VMEM((tm, d), _MXU_DTYPE)],
        compiler_params=_params(("parallel", "arbitrary")),
        name="norm_proj",
    )(x2d, gain, w)


def _gated_out_kernel(o_ref, g_ref, gain_ref, w_ref, x_ref, y_ref, a_ref, *, dv):
    vd = o_ref.shape[1]
    for hd in range(vd // dv):
        cols = slice(hd * dv, (hd + 1) * dv)
        o = o_ref[:, cols]
        g = g_ref[:, cols]
        a_ref[:, cols] = (_rms(o, gain_ref[:, cols]) * (g * jax.nn.sigmoid(g))).astype(a_ref.dtype)
    y_ref[...] = x_ref[...] + jnp.dot(a_ref[...], w_ref[...], preferred_element_type=F32)


def _gated_out(o, p, g_block, gain, w, x2d, *, dv, tm):
    t, vd = o.shape
    d = x2d.shape[1]
    return pl.pallas_call(
        functools.partial(_gated_out_kernel, dv=dv),
        grid=(t // tm,),
        in_specs=[
            pl.BlockSpec((tm, vd), lambda i: (i, 0)),
            pl.BlockSpec((tm, vd), lambda i: (i, g_block)),
            pl.BlockSpec((1, vd), lambda i: (0, 0)),
            pl.BlockSpec((vd, d), lambda i: (0, 0)),
            pl.BlockSpec((tm, d), lambda i: (i, 0)),
        ],
        out_specs=pl.BlockSpec((tm, d), lambda i: (i, 0)),
        out_shape=jax.ShapeDtypeStruct((t, d), F32),
        scratch_shapes=[pltpu.VMEM((tm, vd), _MXU_DTYPE)],
        compiler_params=_params(("parallel",)),
        name="gated_out",
    )(o, p, gain, w, x2d)


_ROPE_THETA = 10000.0


def _rope_kernel(pos_ref, cr_ref, sr_ref, ta_ref, *, rt_half, at_half):
    pos = pos_ref[...]
    lane = lax.broadcasted_iota(jnp.int32, pos.shape, 1)
    inv = jnp.exp(lane.astype(F32) * (-math.log(_ROPE_THETA) / rt_half))
    ang = pos * inv
    cr_ref[...] = jnp.cos(ang)
    sr_ref[...] = jnp.sin(ang)
    j = lane % (2 * at_half)
    inv = jnp.exp((j % at_half).astype(F32) * (-math.log(_ROPE_THETA) / at_half))
    ang = pos * inv
    sn = jnp.sin(ang)
    ta_ref[:, :_LANES] = jnp.cos(ang)
    ta_ref[:, _LANES:2 * _LANES] = jnp.where(j < at_half, -sn, 0.0)
    ta_ref[:, 2 * _LANES:] = jnp.where(j < at_half, 0.0, sn)


def _rope_tables(pos_b, *, rt_half, at_half, tm):
    t = pos_b.shape[0]
    spec = pl.BlockSpec((tm, _LANES), lambda i: (i, 0))
    return pl.pallas_call(
        functools.partial(_rope_kernel, rt_half=rt_half, at_half=at_half),
        grid=(t // tm,),
        in_specs=[spec],
        out_specs=[spec, spec, pl.BlockSpec((tm, 3 * _LANES), lambda i: (i, 0))],
        out_shape=[jax.ShapeDtypeStruct((t, _LANES), F32)] * 2 + [jax.ShapeDtypeStruct((t, 3 * _LANES), F32)],
        compiler_params=_params(("parallel",)),
        name="rope_tables",
    )(pos_b)


_HG_CHUNK = 64
_HG_SUB = 16
_TINY = 1e-30
_TN = (((0,), (0,)), ((), ()))


def _split_dot(a, b_f32):
    hi = b_f32.astype(BF16)
    lo = (b_f32 - hi.astype(F32)).astype(BF16)
    return (jnp.dot(a, hi, preferred_element_type=F32) + jnp.dot(a, lo, preferred_element_type=F32))


def _hgrn_kernel(q_ref, f_ref, i_ref, lb_ref, o_ref, st_ref, *, layer, heads):
    c, sc = _HG_CHUNK, _HG_SUB
    nchunk = q_ref.shape[0] // c
    dk = q_ref.shape[1] // heads

    @pl.when(pl.program_id(1) == 0)
    def _():
        st_ref[...] = jnp.zeros_like(st_ref)

    lbs = lb_ref[...]
    e = jnp.exp(lbs - jnp.max(lbs, axis=0, keepdims=True))
    soft = e / jnp.sum(e, axis=0, keepdims=True)
    lb_all = jnp.zeros_like(soft[0:1])
    for l in range(1, layer + 1):
        lb_all = lb_all + soft[l:l + 1]

    r_i = lax.broadcasted_iota(jnp.int32, (c, c), 0)
    c_i = lax.broadcasted_iota(jnp.int32, (c, c), 1)
    tri = (r_i >= c_i).astype(BF16)
    sub_t = lax.broadcasted_iota(jnp.int32, (sc, 1), 0)

    def head_chunk(rows, hd):
        cols = slice(hd * dk, (hd + 1) * dk)
        lb = lb_all[:, cols]
        qr = q_ref[rows, cols]
        fr = f_ref[rows, cols]
        v = i_ref[rows, cols]
        q = qr * jax.nn.sigmoid(qr)
        forget = lb + (1.0 - lb) * jax.nn.sigmoid(fr)
        logf = jnp.log(jnp.maximum(forget, _TINY))
        k = (1.0 - lb) * jax.nn.sigmoid(-fr)
        if _MXU_DTYPE == BF16:
            b = _split_dot(tri, logf)
        else:
            b = jnp.dot(tri.astype(F32), logf, preferred_element_type=F32)
        st = st_ref[hd]
        o = lax.dot_general((q * jnp.exp(b)).astype(_MXU_DTYPE), st.astype(_MXU_DTYPE), _NT,
                            preferred_element_type=F32)
        vb = v.astype(_MXU_DTYPE)
        outs = []
        for i in range(c // sc):
            lo, hi = i * sc, (i + 1) * sc
            qi, bi = q[lo:hi], b[lo:hi]
            oi = o[lo:hi]
            if i > 0:
                ref_b = b[lo - 1:lo]
                qt = (qi * jnp.exp(bi - ref_b)).astype(_MXU_DTYPE)
                kt = (k[:lo] * jnp.exp(ref_b - b[:lo])).astype(_MXU_DTYPE)
                att = lax.dot_general(qt, kt, _NT, preferred_element_type=F32)
                oi = oi + jnp.dot(att.astype(_MXU_DTYPE), vb[:lo], preferred_element_type=F32)
            for s in range(sc):
                dec = jnp.exp(bi - bi[s:s + 1])
                col = jnp.sum(qi * k[lo + s:lo + s + 1] * dec, axis=1, keepdims=True)
                col = jnp.where(sub_t >= s, col, 0.0)
                oi = oi + col * v[lo + s:lo + s + 1]
            outs.append(oi)
        o_ref[rows, cols] = jnp.concatenate(outs, axis=0)
        b_last = b[c - 1:c]
        kd = (k * jnp.exp(b_last - b)).astype(_MXU_DTYPE)
        st_ref[hd] = jnp.exp(b_last) * st + lax.dot_general(vb, kd, _TN, preferred_element_type=F32)

    def chunk(ci, carry):
        rows = pl.ds(pl.multiple_of(ci * c, c), c)
        for hd in range(heads):
            head_chunk(rows, hd)
        return carry

    lax.fori_loop(0, nchunk, chunk, 0, unroll=2)


def _hgrn_scan(p, hg_lb, *, batch, seq, heads, dk, layer, blk):
    t = p.shape[0]
    nblk = seq // blk
    width = heads * dk
    row = lambda b, s: b * nblk + s
    return pl.pallas_call(
        functools.partial(_hgrn_kernel, layer=layer, heads=heads),
        grid=(batch, nblk),
        in_specs=[
            pl.BlockSpec((blk, width), lambda b, s: (row(b, s), 0)),
            pl.BlockSpec((blk, width), lambda b, s: (row(b, s), 1)),
            pl.BlockSpec((blk, width), lambda b, s: (row(b, s), 2)),
            pl.BlockSpec((hg_lb.shape[0], width), lambda b, s: (0, 0)),
        ],
        out_specs=pl.BlockSpec((blk, width), lambda b, s: (row(b, s), 0)),
        out_shape=jax.ShapeDtypeStruct((t, width), F32),
        scratch_shapes=[pltpu.VMEM((heads, dk, dk), F32)],
        compiler_params=_params(("parallel", "arbitrary")),
        name="hgrn_scan",
    )(p, p, p, hg_lb)


_RT_CHUNK = 128


def _retention_kernel(q_ref, k_ref, v_ref, cos_ref, sin_ref, o_ref, st_ref, *, heads):
    c = _RT_CHUNK
    nchunk = q_ref.shape[0] // c
    dk = q_ref.shape[1] // heads
    dv = v_ref.shape[1] // heads
    half = dk // 2

    @pl.when(pl.program_id(1) == 0)
    def _():
        st_ref[...] = jnp.zeros_like(st_ref)

    t_i = lax.broadcasted_iota(jnp.int32, (c, c), 0)
    s_i = lax.broadcasted_iota(jnp.int32, (c, c), 1)
    rel = (t_i - s_i).astype(F32)
    idx = lax.broadcasted_iota(jnp.int32, (c, _LANES), 0).astype(F32)
    consts = []
    for hd in range(heads):
        lg = math.log1p(-(2.0 ** (-5.0 - hd)))
        decay = jnp.where(rel >= 0, jnp.exp(lg * jnp.maximum(rel, 0.0)), 0.0)
        consts.append((decay, jnp.exp(lg * (idx + 1.0)), jnp.exp(lg * (c - 1.0 - idx)), math.exp(lg * c)))

    def rope(x, cos, sin):
        x1, x2 = x[:, :half], x[:, half:]
        return x1 * cos - x2 * sin, x2 * cos + x1 * sin

    cat = lambda a, b: jnp.concatenate([a, b], axis=1).astype(_MXU_DTYPE)

    def head_chunk(rows, hd, cos, sin):
        decay, q_dec, k_dec, chunk_dec = consts[hd]
        q1, q2 = rope(q_ref[rows, hd * dk:(hd + 1) * dk], cos, sin)
        k1, k2 = rope(k_ref[rows, hd * dk:(hd + 1) * dk], cos, sin)
        scale = dk ** -0.5
        k1, k2 = k1 * scale, k2 * scale
        qb, kb = cat(q1, q2), cat(k1, k2)
        vb = v_ref[rows, hd * dv:(hd + 1) * dv].astype(_MXU_DTYPE)
        att = lax.dot_general(qb, kb, _NT, preferred_element_type=F32) * decay
        o = jnp.dot(att.astype(_MXU_DTYPE), vb, preferred_element_type=F32)
        st = st_ref[hd]
        o = o + jnp.dot(cat(q1 * q_dec, q2 * q_dec), st.astype(_MXU_DTYPE), preferred_element_type=F32)
        o_ref[rows, hd * dv:(hd + 1) * dv] = o
        st_ref[hd] = chunk_dec * st + lax.dot_general(cat(k1 * k_dec, k2 * k_dec), vb, _TN,
                                                      preferred_element_type=F32)

    def chunk(ci, carry):
        rows = pl.ds(pl.multiple_of(ci * c, c), c)
        cos, sin = cos_ref[rows, :], sin_ref[rows, :]
        for hd in range(heads):
            head_chunk(rows, hd, cos, sin)
        return carry

    lax.fori_loop(0, nchunk, chunk, 0)


def _retention_scan(p, cos, sin, *, batch, seq, heads, dk, dv, blk):
    t = p.shape[0]
    nblk = seq // blk
    row = lambda b, s: b * nblk + s
    kd, vd = heads * dk, heads * dv
    assert 2 * kd == vd and dk // 2 == _LANES and _RT_CHUNK == _LANES
    return pl.pallas_call(
        functools.partial(_retention_kernel, heads=heads),
        grid=(batch, nblk),
        in_specs=[
            pl.BlockSpec((blk, kd), lambda b, s: (row(b, s), 0)),
            pl.BlockSpec((blk, kd), lambda b, s: (row(b, s), 1)),
            pl.BlockSpec((blk, vd), lambda b, s: (row(b, s), 1)),
            pl.BlockSpec((blk, _LANES), lambda b, s: (row(b, s), 0)),
            pl.BlockSpec((blk, _LANES), lambda b, s: (row(b, s), 0)),
        ],
        out_specs=pl.BlockSpec((blk, vd), lambda b, s: (row(b, s), 0)),
        out_shape=jax.ShapeDtypeStruct((t, vd), F32),
        scratch_shapes=[pltpu.VMEM((heads, dk, dv), F32)],
        compiler_params=_params(("parallel", "arbitrary")),
        name="retention_scan",
    )(p, p, p, cos, sin)


_AT_BLOCK = 128
_MASK_VALUE = -1e30


def _split_dot_lhs(a_f32, b):
    hi = a_f32.astype(BF16)
    lo = (a_f32 - hi.astype(F32)).astype(BF16)
    return (jnp.dot(hi, b, preferred_element_type=F32) + jnp.dot(lo, b, preferred_element_type=F32))


def _attn_kernel(q_ref, kp_ref, k_ref, vp_ref, v_ref, t0_ref, t1_ref, t2_ref, p0_ref, p1_ref, p2_ref,
                 gq_ref, gk_ref, o_ref, l_ref, *, head_dim, span, dil, nsub):
    blk = _AT_BLOCK
    half = head_dim // 2
    lane = lax.broadcasted_iota(jnp.int32, (1, _LANES), 1)
    qi = lax.broadcasted_iota(jnp.int32, (blk, 2 * blk), 0) + blk
    ki = lax.broadcasted_iota(jnp.int32, (blk, 2 * blk), 1)
    band = (ki <= qi) & (ki >= qi - span)
    first_lo = jnp.where(pl.program_id(2) == 0, blk, 0)
    band_first = (ki <= qi) & (ki >= jnp.maximum(qi - span, first_lo))
    scale = head_dim ** -0.5
    r_i = lax.broadcasted_iota(jnp.int32, (_LANES, _LANES), 0) // head_dim
    c_i = lax.broadcasted_iota(jnp.int32, (_LANES, _LANES), 1) // head_dim
    seg = jnp.where(r_i == c_i, 1.0 / head_dim, 0.0)
    gq, gk = gq_ref[...], gk_ref[...]

    nh = _LANES // head_dim
    gain_qk = jnp.concatenate([jnp.broadcast_to(gq, (blk, _LANES)), jnp.broadcast_to(gk, (blk, _LANES))], axis=0)
    heads_of = lane // head_dim
    segb = seg.astype(_MXU_DTYPE)

    def prep(x, gain, tab):
        msq = jnp.dot((x * x).astype(_MXU_DTYPE), segb, preferred_element_type=F32)
        y = x * lax.rsqrt(msq + _EPS) * gain
        cos, s1, s2 = tab
        y = y * cos + pltpu.roll(y, _LANES - half, 1) * s1 + pltpu.roll(y, half, 1) * s2
        return y.astype(_MXU_DTYPE)

    def residue(r, carry):
        def take(ref, j):
            if dil == 1:
                return ref[j * blk:(j + 1) * blk, :]
            return ref[pl.ds(r + j * blk * dil, blk, stride=dil), :]

        kprev = prep(take(kp_ref, 0), gk, [take(t, 0) for t in (p0_ref, p1_ref, p2_ref)])
        vprev = take(vp_ref, 0).astype(_MXU_DTYPE)
        for j in range(nsub):
            tab = [take(t, j) for t in (t0_ref, t1_ref, t2_ref)]
            qk = prep(jnp.concatenate([take(q_ref, j), take(k_ref, j)], axis=0), gain_qk,
                      [jnp.concatenate([t, t], axis=0) for t in tab])
            q, kcur = qk[:blk], qk[blk:]
            vcur = take(v_ref, j).astype(_MXU_DTYPE)
            kk = jnp.concatenate([kprev, kcur], axis=0)
            vv = jnp.concatenate([vprev, vcur], axis=0)
            mask = band_first if j == 0 else band
            qh = jnp.concatenate([jnp.where(heads_of == hd, q, jnp.zeros_like(q)) for hd in range(nh)], axis=0)
            s = lax.dot_general(qh, kk, _NT, preferred_element_type=F32) * scale
            s = jnp.where(jnp.concatenate([mask] * nh, axis=0), s, _MASK_VALUE)
            m = jnp.max(s, axis=-1, keepdims=True)
            pr = jnp.exp(s - m)
            l = jnp.sum(pr, axis=-1, keepdims=True)
            o = jnp.dot(pr.astype(_MXU_DTYPE), vv, preferred_element_type=F32) / l
            lse = m + jnp.log(l)
            o_pair = o[:blk]
            l_pair = jnp.broadcast_to(lse[:blk], o_pair.shape)
            for hd in range(1, nh):
                o_pair = jnp.where(heads_of == hd, o[hd * blk:(hd + 1) * blk], o_pair)
                l_pair = jnp.where(heads_of == hd, lse[hd * blk:(hd + 1) * blk], l_pair)
            if dil == 1:
                o_ref[j * blk:(j + 1) * blk, :] = o_pair
                l_ref[j * blk:(j + 1) * blk, :] = l_pair
            else:
                o_ref[pl.ds(r + j * blk * dil, blk, stride=dil), :] = o_pair
                l_ref[pl.ds(r + j * blk * dil, blk, stride=dil), :] = l_pair
            kprev, vprev = kcur, vcur
        return carry

    if dil == 1:
        residue(0, 0)
    else:
        per = 2 if nsub > 1 else 4

        def several(i, carry):
            for k in range(per):
                residue(per * i + k, carry)
            return carry

        lax.fori_loop(0, dil // per, several, 0)


def _attn_group(p, tabs, gq, gk, *, batch, seq, group, ngroup, heads, head_dim, window, dilation):
    t = p.shape[0]
    d = dilation
    width = heads * head_dim
    assert seq % (d * _AT_BLOCK) == 0 and window // d <= _AT_BLOCK
    nb = seq // d // _AT_BLOCK
    nsub = max(1, min(4, nb, 2048 // (_AT_BLOCK * d)))
    nstep = nb // nsub
    rows = nsub * _AT_BLOCK * d
    prow = _AT_BLOCK * d
    pairs = width // _LANES
    col = lambda kind, hp: (kind * ngroup + group) * pairs + hp
    main = lambda b, n: b * nstep + n
    prev = lambda b, n: jnp.maximum((b * nstep + n) * nsub - 1, 0)
    out_spec = pl.BlockSpec((rows, _LANES), lambda b, hp, n: (main(b, n), hp))
    return pl.pallas_call(
        functools.partial(_attn_kernel, head_dim=head_dim, span=window // d, dil=d, nsub=nsub),
        grid=(batch, pairs, nstep),
        in_specs=[
            pl.BlockSpec((rows, _LANES), lambda b, hp, n: (main(b, n), col(0, hp))),
            pl.BlockSpec((prow, _LANES), lambda b, hp, n: (prev(b, n), col(1, hp))),
            pl.BlockSpec((rows, _LANES), lambda b, hp, n: (main(b, n), col(1, hp))),
            pl.BlockSpec((prow, _LANES), lambda b, hp, n: (prev(b, n), col(2, hp))),
            pl.BlockSpec((rows, _LANES), lambda b, hp, n: (main(b, n), col(2, hp))),
            pl.BlockSpec((rows, _LANES), lambda b, hp, n: (main(b, n), 0)),
            pl.BlockSpec((rows, _LANES), lambda b, hp, n: (main(b, n), 1)),
            pl.BlockSpec((rows, _LANES), lambda b, hp, n: (main(b, n), 2)),
            pl.BlockSpec((prow, _LANES), lambda b, hp, n: (prev(b, n), 0)),
            pl.BlockSpec((prow, _LANES), lambda b, hp, n: (prev(b, n), 1)),
            pl.BlockSpec((prow, _LANES), lambda b, hp, n: (prev(b, n), 2)),
            pl.BlockSpec((1, _LANES), lambda b, hp, n: (0, 0)),
            pl.BlockSpec((1, _LANES), lambda b, hp, n: (0, 0)),
        ],
        out_specs=[out_spec, out_spec],
        out_shape=[jax.ShapeDtypeStruct((t, width), F32)] * 2,
        compiler_params=_params(("parallel", "parallel", "arbitrary")),
        name=f"attn_group{group}",
    )(p, p, p, p, p, tabs, tabs, tabs, tabs, tabs, tabs, gq, gk)


def _attn_out_kernel(*refs, ngroup):
    o_refs, l_refs = refs[:ngroup], refs[ngroup:2 * ngroup]
    w_ref, x_ref, y_ref = refs[2 * ngroup:]
    ls = [r[...] for r in l_refs]
    m = functools.reduce(jnp.maximum, ls)
    es = [jnp.exp(l - m) for l in ls]
    den = functools.reduce(lambda a, b: a + b, es)
    o = functools.reduce(lambda a, b: a + b, [e * r[...] for e, r in zip(es, o_refs)]) / den
    y_ref[...] = x_ref[...] + jnp.dot(o.astype(_MXU_DTYPE), w_ref[...], preferred_element_type=F32)


def _attn_out(os_, ls_, w, x2d, *, tm):
    t, width = os_[0].shape
    d = x2d.shape[1]
    ng = len(os_)
    blk = pl.BlockSpec((tm, width), lambda i: (i, 0))
    return pl.pallas_call(
        functools.partial(_attn_out_kernel, ngroup=ng),
        grid=(t // tm,),
        in_specs=[blk] * (2 * ng) + [pl.BlockSpec((width, d), lambda i: (0, 0)),
                                     pl.BlockSpec((tm, d), lambda i: (i, 0))],
        out_specs=pl.BlockSpec((tm, d), lambda i: (i, 0)),
        out_shape=jax.ShapeDtypeStruct((t, d), F32),
        compiler_params=_params(("parallel",)),
        name="attn_out",
    )(*os_, *ls_, w, x2d)


_HG_HEADS, _HG_DK = 8, 128
_AT_GROUPS = ((128, 1), (512, 4), (2048, 16))
_AT_HEADS, _AT_HEAD_DIM = 8, 64
_RT_HEADS = 4
_N_MIXERS = 3


def _pick(n, candidates):
    for c in candidates:
        if n % c == 0:
            return c
    return n


def _hgrn_mixer(x2d, gain, w_in, hg_lb, o_gain, w_out, *, batch, seq, layer, blk):
    t = x2d.shape[0]
    p = _norm_proj(x2d, gain, w_in, tm=_pick(t, (1024, 512, 256, 128)), tn=_pick(w_in.shape[1], (1024, 512)))
    o = _hgrn_scan(p, hg_lb, batch=batch, seq=seq, heads=_HG_HEADS, dk=_HG_DK, layer=layer, blk=min(blk, seq))
    return _gated_out(o, p, 3, o_gain, w_out, x2d, dv=_HG_DK, tm=_pick(t, (256, 128)))


def _retention_mixer(x2d, gain, w_in, o_gain, w_out, cos, sin, *, batch, seq, blk):
    t, d = x2d.shape
    dk = d // _RT_HEADS
    dv = 2 * dk
    assert dk == 2 * _LANES
    p = _norm_proj(x2d, gain, w_in, tm=_pick(t, (1024, 512, 256, 128)), tn=_pick(w_in.shape[1], (1536, 1024, 512)))
    o = _retention_scan(p, cos, sin, batch=batch, seq=seq, heads=_RT_HEADS, dk=dk, dv=dv, blk=min(blk, seq))
    return _gated_out(o, p, 2, o_gain, w_out, x2d, dv=dv, tm=_pick(t, (256, 128)))


def _attn_mixer(x2d, gain, w_in, q_gain, k_gain, w_out, tabs, *, batch, seq):
    t = x2d.shape[0]
    ng = len(_AT_GROUPS)
    p = _norm_proj(x2d, gain, w_in, tm=_pick(t, (1024, 512, 256, 128)), tn=_pick(w_in.shape[1], (1536, 512)))
    os_, ls_ = [], []
    for g, (window, dilation) in enumerate(_AT_GROUPS):
        gq = jnp.tile(q_gain[g], _LANES // _AT_HEAD_DIM)[None, :]
        gk = jnp.tile(k_gain[g], _LANES // _AT_HEAD_DIM)[None, :]
        o, lse = _attn_group(p, tabs, gq, gk, batch=batch, seq=seq, group=g, ngroup=ng, heads=_AT_HEADS,
                             head_dim=_AT_HEAD_DIM, window=window, dilation=dilation)
        os_.append(o)
        ls_.append(lse)
    return _attn_out(os_, ls_, w_out, x2d, tm=_pick(t, (512, 256, 128)))


def kernel(x, positions, mix_norm, ffn_norm, hg_lb, hg_w_in, hg_onorm, hg_w_out, at_w_in, at_qnorm, at_knorm,
           at_w_out, rt_w_in, rt_onorm, rt_w_out, pk_w_q, pk_keys, pk_u, pk_v):
    b, s, d = x.shape
    depth = mix_norm.shape[0]
    x2d = x.reshape(b * s, d)
    pos_b = jnp.broadcast_to(positions.reshape(b * s, 1).astype(F32), (b * s, _LANES))
    rt_dk = d // _RT_HEADS
    cos_r, sin_r, tabs_a = _rope_tables(pos_b, rt_half=rt_dk // 2, at_half=_AT_HEAD_DIM // 2, tm=512)
    ia = ib = ic = 0
    for layer in range(depth):
        kind = layer % _N_MIXERS
        gain = mix_norm[layer][None, :]
        if kind == 0:
            x2d = _hgrn_mixer(x2d, gain, hg_w_in[ia].astype(_MXU_DTYPE), hg_lb, hg_onorm[ia][None, :],
                              hg_w_out[ia].astype(_MXU_DTYPE), batch=b, seq=s, layer=layer, blk=512)
            ia += 1
        elif kind == 1:
            x2d = _attn_mixer(x2d, gain, at_w_in[ib].astype(_MXU_DTYPE), at_qnorm[ib], at_knorm[ib],
                              at_w_out[ib].astype(_MXU_DTYPE), tabs_a, batch=b, seq=s)
            ib += 1
        else:
            x2d = _retention_mixer(x2d, gain, rt_w_in[ic].astype(_MXU_DTYPE), rt_onorm[ic][None, :],
                                   rt_w_out[ic].astype(_MXU_DTYPE), cos_r, sin_r, batch=b, seq=s, blk=512)
            ic += 1
        keys = pk_keys[layer].reshape(-1, pk_keys.shape[-2], pk_keys.shape[-1]).astype(_MXU_DTYPE)
        x2d = _peer_layer(x2d, ffn_norm[layer][None, :], pk_w_q[layer].astype(_MXU_DTYPE), keys,
                          pk_u[layer].astype(_MXU_DTYPE), pk_v[layer].T.astype(_MXU_DTYPE),
                          tb_route=512, tb_dense=512, a_t=16)
    return x2d.reshape(b, s, d)
```

```python
import functools
import math

import jax
import jax.numpy as jnp
from jax import lax
from jax.experimental import pallas as pl
from jax.experimental.pallas import tpu as pltpu

F32 = jnp.float32
BF16 = jnp.bfloat16
_MXU_DTYPE = BF16
_EPS = 1e-6
_LANES = 128
_SUBLANES = 8
_VMEM_LIMIT = 56 * 1024 * 1024

_NT = (((1,), (1,)), ((), ()))


def _params(sem, flags=None):
    return pltpu.CompilerParams(dimension_semantics=sem, vmem_limit_bytes=_VMEM_LIMIT, flags=flags)


def _rms(x, gain):
    return x * lax.rsqrt(jnp.mean(x * x, axis=-1, keepdims=True) + _EPS) * gain


def _merge_desc(c):
    c = list(c)
    n = len(c)
    j = n // 2
    while j >= 1:
        for i in range(n):
            l = i ^ j
            if l > i:
                hi = jnp.maximum(c[i], c[l])
                lo = jnp.minimum(c[i], c[l])
                c[i], c[l] = hi, lo
        j //= 2
    return c


def _sort_desc(xs):
    xs = list(xs)
    n = len(xs)
    k = 2
    while k <= n:
        j = k // 2
        while j >= 1:
            for i in range(n):
                l = i ^ j
                if l > i:
                    hi = jnp.maximum(xs[i], xs[l])
                    lo = jnp.minimum(xs[i], xs[l])
                    if (i & k) == 0:
                        xs[i], xs[l] = hi, lo
                    else:
                        xs[i], xs[l] = lo, hi
            j //= 2
        k *= 2
    return xs


def _top_of_two(a, b):
    n = len(a)
    return _merge_desc([jnp.maximum(a[i], b[n - 1 - i]) for i in range(n)])


_PK_TOPK = 16
_PK_PAIRS = [(i, j) for i in range(_PK_TOPK) for j in range(_PK_TOPK) if (i + 1) * (j + 1) <= _PK_TOPK]


_NEG = -3.0e38


def _top16_sums(a_top, b_top):
    cand = [a_top[i] + b_top[j] for (i, j) in _PK_PAIRS]
    pad = jnp.full_like(cand[0], _NEG)
    cand = cand + [pad] * (64 - len(cand))
    groups = [_sort_desc(cand[16 * g:16 * (g + 1)]) for g in range(4)]
    return _top_of_two(_top_of_two(groups[0], groups[1]), _top_of_two(groups[2], groups[3]))


def _gate_word(x, dtype):
    if dtype == F32:
        return x
    u = lax.bitcast_convert_type(x.astype(BF16).astype(F32), jnp.uint32)
    return u | (u >> 16)


def _gate_rows(ref, hd, a, lanes, wdt):
    row = jnp.broadcast_to(ref[hd, pl.ds(a, 1), lanes], (_SUBLANES, _LANES))
    return row if wdt == F32 else pltpu.bitcast(row, wdt)


def _peer_route_kernel(x_ref, g_ref, wq_ref, keys_ref, h_ref, ea_ref, ca_ref, eb_ref, rb_ref,
                       s_scr, top_scr, btop_scr, *, heads, nkeys):
    tb = x_ref.shape[0]
    hf = _rms(x_ref[...], g_ref[...])
    hb = hf.astype(_MXU_DTYPE)
    h_ref[...] = hf.T.astype(_MXU_DTYPE)
    q = jnp.dot(hb, wq_ref[...], preferred_element_type=F32)
    for hc in range(2 * heads):
        qs = q[:, hc * nkeys:(hc + 1) * nkeys].astype(_MXU_DTYPE)
        s_scr[hc] = lax.dot_general(keys_ref[hc], qs, _NT, preferred_element_type=F32)

    nrow = nkeys // _SUBLANES
    bcast = lambda row: jnp.broadcast_to(row, (_SUBLANES, _LANES))

    def group(gi, carry):
        lanes = pl.ds(pl.multiple_of(gi * _LANES, _LANES), _LANES)
        for hc in range(2 * heads):
            hd, c = divmod(hc, 2)
            rows = [s_scr[hc, pl.ds(_SUBLANES * m, _SUBLANES), lanes] for m in range(nrow)]
            srt = _sort_desc(rows)
            for shift in (4, 2, 1):
                srt = _top_of_two(srt, [pltpu.roll(r, shift, 0) for r in srt])
            for i in range(_PK_TOPK):
                top_scr[c, i, pl.ds(hd, 1), :] = srt[i][0:1, :]
            if c == 1:
                for i in range(_PK_TOPK):
                    btop_scr[hd, i] = srt[i]
                ebs, rbs = [], []
                for m in range(nrow):
                    ebs.append(jnp.exp(rows[m] - srt[0]))
                    rank = jnp.full_like(rows[m], float(_PK_TOPK))
                    for i in reversed(range(_PK_TOPK)):
                        rank = jnp.where(srt[i] <= rows[m], float(i), rank)
                    rbs.append(rank)
                for m in range(0, nrow, 2):
                    sl = pl.ds(_SUBLANES * m, 2 * _SUBLANES)
                    eb_ref[hd, sl, lanes] = jnp.concatenate(ebs[m:m + 2], axis=0).astype(eb_ref.dtype)
                    rb_ref[hd, sl, lanes] = jnp.concatenate(rbs[m:m + 2], axis=0).astype(rb_ref.dtype)
        a_s = [top_scr[0, i] for i in range(_PK_TOPK)]
        b_s = [top_scr[1, i] for i in range(_PK_TOPK)]
        top = _top16_sums(a_s, b_s)
        z = jnp.exp(top[0] - top[0])
        for t in top[1:]:
            z = z + jnp.exp(t - top[0])
        rz = 1.0 / z
        theta = top[_PK_TOPK - 1]
        for hd in range(heads):
            amax = bcast(a_s[0][hd:hd + 1, :])
            rz_h = bcast(rz[hd:hd + 1, :])
            th_h = bcast(theta[hd:hd + 1, :])
            for m in range(nrow):
                sl = pl.ds(_SUBLANES * m, _SUBLANES)
                sa = s_scr[2 * hd, sl, lanes]
                ea_ref[hd, sl, lanes] = _gate_word(jnp.exp(sa - amax) * rz_h, ea_ref.dtype)
                cnt = jnp.full_like(sa, float(_PK_TOPK))
                for j in reversed(range(_PK_TOPK)):
                    cnt = jnp.where(sa + btop_scr[hd, j] < th_h, float(j), cnt)
                ca_ref[hd, sl, lanes] = _gate_word(cnt, ca_ref.dtype)
        return carry

    lax.fori_loop(0, tb // _LANES, group, 0)


def _peer_route(x2d, gain, wq, keys, *, tb):
    t, d = x2d.shape
    hc, nkeys, dk = keys.shape
    heads = hc // 2
    kern = functools.partial(_peer_route_kernel, heads=heads, nkeys=nkeys)
    tab = pl.BlockSpec((heads, nkeys, tb), lambda i: (0, 0, i))
    word = F32 if _MXU_DTYPE == F32 else jnp.uint32
    return pl.pallas_call(
        kern,
        grid=(t // tb,),
        in_specs=[
            pl.BlockSpec((tb, d), lambda i: (i, 0)),
            pl.BlockSpec((1, d), lambda i: (0, 0)),
            pl.BlockSpec(wq.shape, lambda i: (0, 0)),
            pl.BlockSpec(keys.shape, lambda i: (0, 0, 0)),
        ],
        out_specs=[pl.BlockSpec((d, tb), lambda i: (0, i)), tab, tab, tab, tab],
        out_shape=[
            jax.ShapeDtypeStruct((d, t), _MXU_DTYPE),
            jax.ShapeDtypeStruct((heads, nkeys, t), word),
            jax.ShapeDtypeStruct((heads, nkeys, t), word),
            jax.ShapeDtypeStruct((heads, nkeys, t), _MXU_DTYPE),
            jax.ShapeDtypeStruct((heads, nkeys, t), _MXU_DTYPE),
        ],
        scratch_shapes=[
            pltpu.VMEM((hc, nkeys, tb), F32),
            pltpu.VMEM((2, _PK_TOPK, heads, _LANES), F32),
            pltpu.VMEM((heads, _PK_TOPK, _SUBLANES, _LANES), F32),
        ],
        compiler_params=_params(("parallel",)),
        name="peer_route",
    )(x2d, gain, wq, keys)


def _gelu(x):
    return 0.5 * x * (1.0 + lax.erf(x * (1.0 / math.sqrt(2.0))))


def _peer_dense_kernel(h_ref, u_ref, vt_ref, ea_ref, ca_ref, eb_ref, rb_ref, x_ref, o_ref, acc_ref,
                       hu_ref, w_ref, *, heads, nkeys):
    e = pl.program_id(1)
    tb = h_ref.shape[1]
    a_t = ea_ref.shape[1]
    wdt = vt_ref.dtype
    pk = _SUBLANES * (4 // jnp.dtype(wdt).itemsize)
    zero = jnp.zeros((pk, _LANES), wdt)
    mc = _pick(tb, (512, 256))

    @pl.when(e == 0)
    def _():
        acc_ref[...] = jnp.zeros_like(acc_ref)

    mr = 2 * nkeys
    for c in range(tb // mc):
        toks = slice(c * mc, (c + 1) * mc)
        for r in range(a_t * nkeys // mr):
            rows = slice(r * mr, (r + 1) * mr)
            hu_ref[rows, toks] = jnp.dot(u_ref[rows, :], h_ref[:, toks], preferred_element_type=F32)

    for c in range(tb // mc):
        toks = slice(c * mc, (c + 1) * mc)
        for a0 in range(0, a_t, 2):
            for tc in range(mc // _LANES):
                lanes = slice(c * mc + tc * _LANES, c * mc + (tc + 1) * _LANES)
                accs = [[None] * (nkeys // pk) for _ in range(2)]
                for hd in range(heads):
                    ea_b = [_gate_rows(ea_ref, hd, a0 + i, lanes, wdt) for i in range(2)]
                    ca_b = [_gate_rows(ca_ref, hd, a0 + i, lanes, wdt) for i in range(2)]
                    for bc in range(nkeys // pk):
                        rows = slice(bc * pk, (bc + 1) * pk)
                        rbv, ebv = rb_ref[hd, rows, lanes], eb_ref[hd, rows, lanes]
                        for i in range(2):
                            term = ea_b[i] * lax.clamp(zero, ca_b[i] - rbv, ebv)
                            accs[i][bc] = term if accs[i][bc] is None else accs[i][bc] + term
                for i in range(2):
                    for bc in range(nkeys // pk):
                        erows = slice((a0 + i) * nkeys + bc * pk, (a0 + i) * nkeys + (bc + 1) * pk)
                        w_ref[erows, lanes] = accs[i][bc] * _gelu(hu_ref[erows, lanes].astype(wdt))
        acc_ref[:, toks] += jnp.dot(vt_ref[...], w_ref[:, toks], preferred_element_type=F32)

    @pl.when(e == pl.num_programs(1) - 1)
    def _():
        o_ref[...] = x_ref[...] + acc_ref[...].T


def _peer_dense(h, u, vt, ea, ca, eb, rb, x2d, *, tb, a_t):
    t, d = x2d.shape
    heads, nkeys, _ = ea.shape
    e_t = a_t * nkeys
    n_e = u.shape[0] // e_t
    kern = functools.partial(_peer_dense_kernel, heads=heads, nkeys=nkeys)
    gate_a = pl.BlockSpec((heads, a_t, tb), lambda i, e: (0, e, i))
    gate_b = pl.BlockSpec((heads, nkeys, tb), lambda i, e: (0, 0, i))
    return pl.pallas_call(
        kern,
        grid=(t // tb, n_e),
        in_specs=[
            pl.BlockSpec((d, tb), lambda i, e: (0, i)),
            pl.BlockSpec((e_t, d), lambda i, e: (e, 0)),
            pl.BlockSpec((d, e_t), lambda i, e: (0, e)),
            gate_a, gate_a, gate_b, gate_b,
            pl.BlockSpec((tb, d), lambda i, e: (i, 0)),
        ],
        out_specs=pl.BlockSpec((tb, d), lambda i, e: (i, 0)),
        out_shape=jax.ShapeDtypeStruct((t, d), F32),
        scratch_shapes=[pltpu.VMEM((d, tb), F32), pltpu.VMEM((e_t, tb), F32), pltpu.VMEM((e_t, tb), _MXU_DTYPE)],
        compiler_params=_params(("parallel", "arbitrary")),
        name="peer_dense",
    )(h, u, vt, ea, ca, eb, rb, x2d)


def _peer_layer(x2d, gain, wq, keys, u, vt, *, tb_route, tb_dense, a_t):
    h, ea, ca, eb, rb = _peer_route(x2d, gain, wq, keys, tb=tb_route)
    return _peer_dense(h, u, vt, ea, ca, eb, rb, x2d, tb=tb_dense, a_t=a_t)


def _norm_proj_kernel(x_ref, g_ref, w_ref, o_ref, hb_ref):
    @pl.when(pl.program_id(1) == 0)
    def _():
        hb_ref[...] = _rms(x_ref[...], g_ref[...]).astype(hb_ref.dtype)

    o_ref[...] = jnp.dot(hb_ref[...], w_ref[...], preferred_element_type=F32)


def _norm_proj(x2d, gain, w, *, tm, tn):
    t, d = x2d.shape
    n = w.shape[1]
    return pl.pallas_call(
        _norm_proj_kernel,
        grid=(t // tm, n // tn),
        in_specs=[
            pl.BlockSpec((tm, d), lambda i, j: (i, 0)),
            pl.BlockSpec((1, d), lambda i, j: (0, 0)),
            pl.BlockSpec((d, tn), lambda i, j: (0, j)),
        ],
        out_specs=pl.BlockSpec((tm, tn), lambda i, j: (i, j)),
        out_shape=jax.ShapeDtypeStruct((t, n), F32),
        scratch_shapes=[pltpu.VMEM((tm, d), _MXU_DTYPE)],
        compiler_params=_params(("parallel", "arbitrary")),
        name="norm_proj",
    )(x2d, gain, w)


def _gated_out_kernel(o_ref, g_ref, gain_ref, w_ref, x_ref, y_ref, a_ref, *, dv):
    vd = o_ref.shape[1]
    for hd in range(vd // dv):
        cols = slice(hd * dv, (hd + 1) * dv)
        o = o_ref[:, cols]
        g = g_ref[:, cols]
        a_ref[:, cols] = (_rms(o, gain_ref[:, cols]) * (g * jax.nn.sigmoid(g))).astype(a_ref.dtype)
    y_ref[...] = x_ref[...] + jnp.dot(a_ref[...], w_ref[...], preferred_element_type=F32)


def _gated_out(o, p, g_block, gain, w, x2d, *, dv, tm):
    t, vd = o.shape
    d = x2d.shape[1]
    return pl.pallas_call(
        functools.partial(_gated_out_kernel, dv=dv),
        grid=(t // tm,),
        in_specs=[
            pl.BlockSpec((tm, vd), lambda i: (i, 0)),
            pl.BlockSpec((tm, vd), lambda i: (i, g_block)),
            pl.BlockSpec((1, vd), lambda i: (0, 0)),
            pl.BlockSpec((vd, d), lambda i: (0, 0)),
            pl.BlockSpec((tm, d), lambda i: (i, 0)),
        ],
        out_specs=pl.BlockSpec((tm, d), lambda i: (i, 0)),
        out_shape=jax.ShapeDtypeStruct((t, d), F32),
        scratch_shapes=[pltpu.VMEM((tm, vd), _MXU_DTYPE)],
        compiler_params=_params(("parallel",)),
        name="gated_out",
    )(o, p, gain, w, x2d)


_ROPE_THETA = 10000.0


def _rope_kernel(pos_ref, cr_ref, sr_ref, ta_ref, *, rt_half, at_half):
    pos = pos_ref[...]
    lane = lax.broadcasted_iota(jnp.int32, pos.shape, 1)
    inv = jnp.exp(lane.astype(F32) * (-math.log(_ROPE_THETA) / rt_half))
    ang = pos * inv
    cr_ref[...] = jnp.cos(ang)
    sr_ref[...] = jnp.sin(ang)
    j = lane % (2 * at_half)
    inv = jnp.exp((j % at_half).astype(F32) * (-math.log(_ROPE_THETA) / at_half))
    ang = pos * inv
    sn = jnp.sin(ang)
    ta_ref[:, :_LANES] = jnp.cos(ang)
    ta_ref[:, _LANES:2 * _LANES] = jnp.where(j < at_half, -sn, 0.0)
    ta_ref[:, 2 * _LANES:] = jnp.where(j < at_half, 0.0, sn)


def _rope_tables(pos_b, *, rt_half, at_half, tm):
    t = pos_b.shape[0]
    spec = pl.BlockSpec((tm, _LANES), lambda i: (i, 0))
    return pl.pallas_call(
        functools.partial(_rope_kernel, rt_half=rt_half, at_half=at_half),
        grid=(t // tm,),
        in_specs=[spec],
        out_specs=[spec, spec, pl.BlockSpec((tm, 3 * _LANES), lambda i: (i, 0))],
        out_shape=[jax.ShapeDtypeStruct((t, _LANES), F32)] * 2 + [jax.ShapeDtypeStruct((t, 3 * _LANES), F32)],
        compiler_params=_params(("parallel",)),
        name="rope_tables",
    )(pos_b)


_HG_CHUNK = 64
_HG_SUB = 16
_TINY = 1e-30
_TN = (((0,), (0,)), ((), ()))


def _split_dot(a, b_f32):
    hi = b_f32.astype(BF16)
    lo = (b_f32 - hi.astype(F32)).astype(BF16)
    return (jnp.dot(a, hi, preferred_element_type=F32) + jnp.dot(a, lo, preferred_element_type=F32))


def _hgrn_kernel(q_ref, f_ref, i_ref, lb_ref, o_ref, st_ref, *, layer, heads):
    c, sc = _HG_CHUNK, _HG_SUB
    nchunk = q_ref.shape[0] // c
    dk = q_ref.shape[1] // heads

    @pl.when(pl.program_id(1) == 0)
    def _():
        st_ref[...] = jnp.zeros_like(st_ref)

    lbs = lb_ref[...]
    e = jnp.exp(lbs - jnp.max(lbs, axis=0, keepdims=True))
    soft = e / jnp.sum(e, axis=0, keepdims=True)
    lb_all = jnp.zeros_like(soft[0:1])
    for l in range(1, layer + 1):
        lb_all = lb_all + soft[l:l + 1]

    r_i = lax.broadcasted_iota(jnp.int32, (c, c), 0)
    c_i = lax.broadcasted_iota(jnp.int32, (c, c), 1)
    tri = (r_i >= c_i).astype(BF16)
    sub_t = lax.broadcasted_iota(jnp.int32, (sc, 1), 0)

    def head_chunk(rows, hd):
        cols = slice(hd * dk, (hd + 1) * dk)
        lb = lb_all[:, cols]
        qr = q_ref[rows, cols]
        fr = f_ref[rows, cols]
        v = i_ref[rows, cols]
        q = qr * jax.nn.sigmoid(qr)
        forget = lb + (1.0 - lb) * jax.nn.sigmoid(fr)
        logf = jnp.log(jnp.maximum(forget, _TINY))
        k = (1.0 - lb) * jax.nn.sigmoid(-fr)
        if _MXU_DTYPE == BF16:
            b = _split_dot(tri, logf)
        else:
            b = jnp.dot(tri.astype(F32), logf, preferred_element_type=F32)
        st = st_ref[hd]
        o = lax.dot_general((q * jnp.exp(b)).astype(_MXU_DTYPE), st.astype(_MXU_DTYPE), _NT,
                            preferred_element_type=F32)
        vb = v.astype(_MXU_DTYPE)
        outs = []
        for i in range(c // sc):
            lo, hi = i * sc, (i + 1) * sc
            qi, bi = q[lo:hi], b[lo:hi]
            oi = o[lo:hi]
            if i > 0:
                ref_b = b[lo - 1:lo]
                qt = (qi * jnp.exp(bi - ref_b)).astype(_MXU_DTYPE)
                kt = (k[:lo] * jnp.exp(ref_b - b[:lo])).astype(_MXU_DTYPE)
                att = lax.dot_general(qt, kt, _NT, preferred_element_type=F32)
                oi = oi + jnp.dot(att.astype(_MXU_DTYPE), vb[:lo], preferred_element_type=F32)
            for s in range(sc):
                dec = jnp.exp(bi - bi[s:s + 1])
                col = jnp.sum(qi * k[lo + s:lo + s + 1] * dec, axis=1, keepdims=True)
                col = jnp.where(sub_t >= s, col, 0.0)
                oi = oi + col * v[lo + s:lo + s + 1]
            outs.append(oi)
        o_ref[rows, cols] = jnp.concatenate(outs, axis=0)
        b_last = b[c - 1:c]
        kd = (k * jnp.exp(b_last - b)).astype(_MXU_DTYPE)
        st_ref[hd] = jnp.exp(b_last) * st + lax.dot_general(vb, kd, _TN, preferred_element_type=F32)

    def chunk(ci, carry):
        rows = pl.ds(pl.multiple_of(ci * c, c), c)
        for hd in range(heads):
            head_chunk(rows, hd)
        return carry

    lax.fori_loop(0, nchunk, chunk, 0, unroll=2)


def _hgrn_scan(p, hg_lb, *, batch, seq, heads, dk, layer, blk):
    t = p.shape[0]
    nblk = seq // blk
    width = heads * dk
    row = lambda b, s: b * nblk + s
    return pl.pallas_call(
        functools.partial(_hgrn_kernel, layer=layer, heads=heads),
        grid=(batch, nblk),
        in_specs=[
            pl.BlockSpec((blk, width), lambda b, s: (row(b, s), 0)),
            pl.BlockSpec((blk, width), lambda b, s: (row(b, s), 1)),
            pl.BlockSpec((blk, width), lambda b, s: (row(b, s), 2)),
            pl.BlockSpec((hg_lb.shape[0], width), lambda b, s: (0, 0)),
        ],
        out_specs=pl.BlockSpec((blk, width), lambda b, s: (row(b, s), 0)),
        out_shape=jax.ShapeDtypeStruct((t, width), F32),
        scratch_shapes=[pltpu.VMEM((heads, dk, dk), F32)],
        compiler_params=_params(("parallel", "arbitrary")),
        name="hgrn_scan",
    )(p, p, p, hg_lb)


_RT_CHUNK = 128


def _retention_kernel(q_ref, k_ref, v_ref, cos_ref, sin_ref, o_ref, st_ref, *, heads):
    c = _RT_CHUNK
    nchunk = q_ref.shape[0] // c
    dk = q_ref.shape[1] // heads
    dv = v_ref.shape[1] // heads
    half = dk // 2

    @pl.when(pl.program_id(1) == 0)
    def _():
        st_ref[...] = jnp.zeros_like(st_ref)

    t_i = lax.broadcasted_iota(jnp.int32, (c, c), 0)
    s_i = lax.broadcasted_iota(jnp.int32, (c, c), 1)
    rel = (t_i - s_i).astype(F32)
    idx = lax.broadcasted_iota(jnp.int32, (c, _LANES), 0).astype(F32)
    consts = []
    for hd in range(heads):
        lg = math.log1p(-(2.0 ** (-5.0 - hd)))
        decay = jnp.where(rel >= 0, jnp.exp(lg * jnp.maximum(rel, 0.0)), 0.0)
        consts.append((decay, jnp.exp(lg * (idx + 1.0)), jnp.exp(lg * (c - 1.0 - idx)), math.exp(lg * c)))

    def rope(x, cos, sin):
        x1, x2 = x[:, :half], x[:, half:]
        return x1 * cos - x2 * sin, x2 * cos + x1 * sin

    cat = lambda a, b: jnp.concatenate([a, b], axis=1).astype(_MXU_DTYPE)

    def head_chunk(rows, hd, cos, sin):
        decay, q_dec, k_dec, chunk_dec = consts[hd]
        q1, q2 = rope(q_ref[rows, hd * dk:(hd + 1) * dk], cos, sin)
        k1, k2 = rope(k_ref[rows, hd * dk:(hd + 1) * dk], cos, sin)
        scale = dk ** -0.5
        k1, k2 = k1 * scale, k2 * scale
        qb, kb = cat(q1, q2), cat(k1, k2)
        vb = v_ref[rows, hd * dv:(hd + 1) * dv].astype(_MXU_DTYPE)
        att = lax.dot_general(qb, kb, _NT, preferred_element_type=F32) * decay
        o = jnp.dot(att.astype(_MXU_DTYPE), vb, preferred_element_type=F32)
        st = st_ref[hd]
        o = o + jnp.dot(cat(q1 * q_dec, q2 * q_dec), st.astype(_MXU_DTYPE), preferred_element_type=F32)
        o_ref[rows, hd * dv:(hd + 1) * dv] = o
        st_ref[hd] = chunk_dec * st + lax.dot_general(cat(k1 * k_dec, k2 * k_dec), vb, _TN,
                                                      preferred_element_type=F32)

    def chunk(ci, carry):
        rows = pl.ds(pl.multiple_of(ci * c, c), c)
        cos, sin = cos_ref[rows, :], sin_ref[rows, :]
        for hd in range(heads):
            head_chunk(rows, hd, cos, sin)
        return carry

    lax.fori_loop(0, nchunk, chunk, 0)


def _retention_scan(p, cos, sin, *, batch, seq, heads, dk, dv, blk):
    t = p.shape[0]
    nblk = seq // blk
    row = lambda b, s: b * nblk + s
    kd, vd = heads * dk, heads * dv
    assert 2 * kd == vd and dk // 2 == _LANES and _RT_CHUNK == _LANES
    return pl.pallas_call(
        functools.partial(_retention_kernel, heads=heads),
        grid=(batch, nblk),
        in_specs=[
            pl.BlockSpec((blk, kd), lambda b, s: (row(b, s), 0)),
            pl.BlockSpec((blk, kd), lambda b, s: (row(b, s), 1)),
            pl.BlockSpec((blk, vd), lambda b, s: (row(b, s), 1)),
            pl.BlockSpec((blk, _LANES), lambda b, s: (row(b, s), 0)),
            pl.BlockSpec((blk, _LANES), lambda b, s: (row(b, s), 0)),
        ],
        out_specs=pl.BlockSpec((blk, vd), lambda b, s: (row(b, s), 0)),
        out_shape=jax.ShapeDtypeStruct((t, vd), F32),
        scratch_shapes=[pltpu.VMEM((heads, dk, dv), F32)],
        compiler_params=_params(("parallel", "arbitrary")),
        name="retention_scan",
    )(p, p, p, cos, sin)


_AT_BLOCK = 128
_MASK_VALUE = -1e30


def _attn_kernel(q_ref, kp_ref, k_ref, vp_ref, v_ref, t0_ref, t1_ref, t2_ref, p0_ref, p1_ref, p2_ref,
                 gq_ref, gk_ref, o_ref, l_ref, *, head_dim, span, dil, nsub):
    blk = _AT_BLOCK
    half = head_dim // 2
    lane = lax.broadcasted_iota(jnp.int32, (1, _LANES), 1)
    qi = lax.broadcasted_iota(jnp.int32, (blk, 2 * blk), 0) + blk
    ki = lax.broadcasted_iota(jnp.int32, (blk, 2 * blk), 1)
    band = (ki <= qi) & (ki >= qi - span)
    first_lo = jnp.where(pl.program_id(2) == 0, blk, 0)
    band_first = (ki <= qi) & (ki >= jnp.maximum(qi - span, first_lo))
    scale = head_dim ** -0.5
    r_i = lax.broadcasted_iota(jnp.int32, (_LANES, _LANES), 0) // head_dim
    c_i = lax.broadcasted_iota(jnp.int32, (_LANES, _LANES), 1) // head_dim
    seg = jnp.where(r_i == c_i, 1.0 / head_dim, 0.0)
    gq, gk = gq_ref[...], gk_ref[...]

    nh = _LANES // head_dim
    gain_qk = jnp.concatenate([jnp.broadcast_to(gq, (blk, _LANES)), jnp.broadcast_to(gk, (blk, _LANES))], axis=0)
    heads_of = lane // head_dim
    segb = seg.astype(_MXU_DTYPE)

    def prep(x, gain, tab):
        msq = jnp.dot((x * x).astype(_MXU_DTYPE), segb, preferred_element_type=F32)
        y = x * lax.rsqrt(msq + _EPS) * gain
        cos, s1, s2 = tab
        y = y * cos + pltpu.roll(y, _LANES - half, 1) * s1 + pltpu.roll(y, half, 1) * s2
        return y.astype(_MXU_DTYPE)

    def residue(r, carry):
        def take(ref, j):
            if dil == 1:
                return ref[j * blk:(j + 1) * blk, :]
            return ref[pl.ds(r + j * blk * dil, blk, stride=dil), :]

        kprev = prep(take(kp_ref, 0), gk, [take(t, 0) for t in (p0_ref, p1_ref, p2_ref)])
        vprev = take(vp_ref, 0).astype(_MXU_DTYPE)
        for j in range(nsub):
            tab = [take(t, j) for t in (t0_ref, t1_ref, t2_ref)]
            qk = prep(jnp.concatenate([take(q_ref, j), take(k_ref, j)], axis=0), gain_qk,
                      [jnp.concatenate([t, t], axis=0) for t in tab])
            q, kcur = qk[:blk], qk[blk:]
            vcur = take(v_ref, j).astype(_MXU_DTYPE)
            kk = jnp.concatenate([kprev, kcur], axis=0)
            vv = jnp.concatenate([vprev, vcur], axis=0)
            mask = band_first if j == 0 else band
            qh = jnp.concatenate([jnp.where(heads_of == hd, q, jnp.zeros_like(q)) for hd in range(nh)], axis=0)
            s = lax.dot_general(qh, kk, _NT, preferred_element_type=F32) * scale
            s = jnp.where(jnp.concatenate([mask] * nh, axis=0), s, _MASK_VALUE)
            m = jnp.max(s, axis=-1, keepdims=True)
            pr = jnp.exp(s - m)
            l = jnp.sum(pr, axis=-1, keepdims=True)
            o = jnp.dot(pr.astype(_MXU_DTYPE), vv, preferred_element_type=F32) / l
            lse = m + jnp.log(l)
            o_pair = o[:blk]
            l_pair = jnp.broadcast_to(lse[:blk], o_pair.shape)
            for hd in range(1, nh):
                o_pair = jnp.where(heads_of == hd, o[hd * blk:(hd + 1) * blk], o_pair)
                l_pair = jnp.where(heads_of == hd, lse[hd * blk:(hd + 1) * blk], l_pair)
            if dil == 1:
                o_ref[j * blk:(j + 1) * blk, :] = o_pair
                l_ref[j * blk:(j + 1) * blk, :] = l_pair
            else:
                o_ref[pl.ds(r + j * blk * dil, blk, stride=dil), :] = o_pair
                l_ref[pl.ds(r + j * blk * dil, blk, stride=dil), :] = l_pair
            kprev, vprev = kcur, vcur
        return carry

    if dil == 1:
        residue(0, 0)
    else:
        per = 2 if nsub > 1 else 4

        def several(i, carry):
            for k in range(per):
                residue(per * i + k, carry)
            return carry

        lax.fori_loop(0, dil // per, several, 0)


def _attn_group(p, tabs, gq, gk, *, batch, seq, group, ngroup, heads, head_dim, window, dilation):
    t = p.shape[0]
    d = dilation
    width = heads * head_dim
    assert seq % (d * _AT_BLOCK) == 0 and window // d <= _AT_BLOCK
    nb = seq // d // _AT_BLOCK
    nsub = max(1, min(4, nb, 2048 // (_AT_BLOCK * d)))
    nstep = nb // nsub
    rows = nsub * _AT_BLOCK * d
    prow = _AT_BLOCK * d
    pairs = width // _LANES
    col = lambda kind, hp: (kind * ngroup + group) * pairs + hp
    main = lambda b, n: b * nstep + n
    prev = lambda b, n: jnp.maximum((b * nstep + n) * nsub - 1, 0)
    out_spec = pl.BlockSpec((rows, _LANES), lambda b, hp, n: (main(b, n), hp))
    return pl.pallas_call(
        functools.partial(_attn_kernel, head_dim=head_dim, span=window // d, dil=d, nsub=nsub),
        grid=(batch, pairs, nstep),
        in_specs=[
            pl.BlockSpec((rows, _LANES), lambda b, hp, n: (main(b, n), col(0, hp))),
            pl.BlockSpec((prow, _LANES), lambda b, hp, n: (prev(b, n), col(1, hp))),
            pl.BlockSpec((rows, _LANES), lambda b, hp, n: (main(b, n), col(1, hp))),
            pl.BlockSpec((prow, _LANES), lambda b, hp, n: (prev(b, n), col(2, hp))),
            pl.BlockSpec((rows, _LANES), lambda b, hp, n: (main(b, n), col(2, hp))),
            pl.BlockSpec((rows, _LANES), lambda b, hp, n: (main(b, n), 0)),
            pl.BlockSpec((rows, _LANES), lambda b, hp, n: (main(b, n), 1)),
            pl.BlockSpec((rows, _LANES), lambda b, hp, n: (main(b, n), 2)),
            pl.BlockSpec((prow, _LANES), lambda b, hp, n: (prev(b, n), 0)),
            pl.BlockSpec((prow, _LANES), lambda b, hp, n: (prev(b, n), 1)),
            pl.BlockSpec((prow, _LANES), lambda b, hp, n: (prev(b, n), 2)),
            pl.BlockSpec((1, _LANES), lambda b, hp, n: (0, 0)),
            pl.BlockSpec((1, _LANES), lambda b, hp, n: (0, 0)),
        ],
        out_specs=[out_spec, out_spec],
        out_shape=[jax.ShapeDtypeStruct((t, width), F32)] * 2,
        compiler_params=_params(("parallel", "parallel", "arbitrary")),
        name=f"attn_group{group}",
    )(p, p, p, p, p, tabs, tabs, tabs, tabs, tabs, tabs, gq, gk)


def _attn_out_kernel(*refs, ngroup):
    o_refs, l_refs = refs[:ngroup], refs[ngroup:2 * ngroup]
    w_ref, x_ref, y_ref = refs[2 * ngroup:]
    ls = [r[...] for r in l_refs]
    m = functools.reduce(jnp.maximum, ls)
    es = [jnp.exp(l - m) for l in ls]
    den = functools.reduce(lambda a, b: a + b, es)
    o = functools.reduce(lambda a, b: a + b, [e * r[...] for e, r in zip(es, o_refs)]) / den
    y_ref[...] = x_ref[...] + jnp.dot(o.astype(_MXU_DTYPE), w_ref[...], preferred_element_type=F32)


def _attn_out(os_, ls_, w, x2d, *, tm):
    t, width = os_[0].shape
    d = x2d.shape[1]
    ng = len(os_)
    blk = pl.BlockSpec((tm, width), lambda i: (i, 0))
    return pl.pallas_call(
        functools.partial(_attn_out_kernel, ngroup=ng),
        grid=(t // tm,),
        in_specs=[blk] * (2 * ng) + [pl.BlockSpec((width, d), lambda i: (0, 0)),
                                     pl.BlockSpec((tm, d), lambda i: (i, 0))],
        out_specs=pl.BlockSpec((tm, d), lambda i: (i, 0)),
        out_shape=jax.ShapeDtypeStruct((t, d), F32),
        compiler_params=_params(("parallel",)),
        name="attn_out",
    )(*os_, *ls_, w, x2d)


_HG_HEADS, _HG_DK = 8, 128
_AT_GROUPS = ((128, 1), (512, 4), (2048, 16))
_AT_HEADS, _AT_HEAD_DIM = 8, 64
_RT_HEADS = 4
_N_MIXERS = 3


def _pick(n, candidates):
    for c in candidates:
        if n % c == 0:
            return c
    return n


def _tiles(t, seq):
    return dict(
        proj_rows=_pick(t, (1024, 512, 256, 128)),
        out_rows=_pick(t, (512, 256, 128)),
        scan_blk=min(512, seq),
        route_tb=_pick(t, (512, 256, 128)),
        dense_tb=_pick(t, (512, 256, 128)),
        dense_at=16,
    )


def _hgrn_mixer(x2d, gain, w_in, hg_lb, o_gain, w_out, *, batch, seq, layer, tiles):
    p = _norm_proj(x2d, gain, w_in, tm=tiles["proj_rows"], tn=_pick(w_in.shape[1], (1024, 512)))
    o = _hgrn_scan(p, hg_lb, batch=batch, seq=seq, heads=_HG_HEADS, dk=_HG_DK, layer=layer, blk=tiles["scan_blk"])
    return _gated_out(o, p, 3, o_gain, w_out, x2d, dv=_HG_DK, tm=tiles["out_rows"])


def _retention_mixer(x2d, gain, w_in, o_gain, w_out, cos, sin, *, batch, seq, tiles):
    d = x2d.shape[1]
    dk = d // _RT_HEADS
    dv = 2 * dk
    p = _norm_proj(x2d, gain, w_in, tm=tiles["proj_rows"], tn=_pick(w_in.shape[1], (1536, 1024, 512)))
    o = _retention_scan(p, cos, sin, batch=batch, seq=seq, heads=_RT_HEADS, dk=dk, dv=dv, blk=tiles["scan_blk"])
    return _gated_out(o, p, 2, o_gain, w_out, x2d, dv=dv, tm=tiles["out_rows"])


def _attn_mixer(x2d, gain, w_in, q_gain, k_gain, w_out, tabs, *, batch, seq, tiles):
    ng = len(_AT_GROUPS)
    p = _norm_proj(x2d, gain, w_in, tm=tiles["proj_rows"], tn=_pick(w_in.shape[1], (1536, 512)))
    os_, ls_ = [], []
    for g, (window, dilation) in enumerate(_AT_GROUPS):
        gq = jnp.tile(q_gain[g], _LANES // _AT_HEAD_DIM)[None, :]
        gk = jnp.tile(k_gain[g], _LANES // _AT_HEAD_DIM)[None, :]
        o, lse = _attn_group(p, tabs, gq, gk, batch=batch, seq=seq, group=g, ngroup=ng, heads=_AT_HEADS,
                             head_dim=_AT_HEAD_DIM, window=window, dilation=dilation)
        os_.append(o)
        ls_.append(lse)
    return _attn_out(os_, ls_, w_out, x2d, tm=tiles["out_rows"])


def kernel(x, positions, mix_norm, ffn_norm, hg_lb, hg_w_in, hg_onorm, hg_w_out, at_w_in, at_qnorm, at_knorm,
           at_w_out, rt_w_in, rt_onorm, rt_w_out, pk_w_q, pk_keys, pk_u, pk_v):
    b, s, d = x.shape
    depth = mix_norm.shape[0]
    tiles = _tiles(b * s, s)
    x2d = x.reshape(b * s, d)
    pos_b = jnp.broadcast_to(positions.reshape(b * s, 1).astype(F32), (b * s, _LANES))
    rt_dk = d // _RT_HEADS
    cos_r, sin_r, tabs_a = _rope_tables(pos_b, rt_half=rt_dk // 2, at_half=_AT_HEAD_DIM // 2,
                                        tm=tiles["out_rows"])
    ia = ib = ic = 0
    for layer in range(depth):
        kind = layer % _N_MIXERS
        gain = mix_norm[layer][None, :]
        if kind == 0:
            x2d = _hgrn_mixer(x2d, gain, hg_w_in[ia].astype(_MXU_DTYPE), hg_lb, hg_onorm[ia][None, :],
                              hg_w_out[ia].astype(_MXU_DTYPE), batch=b, seq=s, layer=layer, tiles=tiles)
            ia += 1
        elif kind == 1:
            x2d = _attn_mixer(x2d, gain, at_w_in[ib].astype(_MXU_DTYPE), at_qnorm[ib], at_knorm[ib],
                              at_w_out[ib].astype(_MXU_DTYPE), tabs_a, batch=b, seq=s, tiles=tiles)
            ib += 1
        else:
            x2d = _retention_mixer(x2d, gain, rt_w_in[ic].astype(_MXU_DTYPE), rt_onorm[ic][None, :],
                                   rt_w_out[ic].astype(_MXU_DTYPE), cos_r, sin_r, batch=b, seq=s, tiles=tiles)
            ic += 1
        keys = pk_keys[layer].reshape(-1, pk_keys.shape[-2], pk_keys.shape[-1]).astype(_MXU_DTYPE)
        x2d = _peer_layer(x2d, ffn_norm[layer][None, :], pk_w_q[layer].astype(_MXU_DTYPE), keys,
                          pk_u[layer].astype(_MXU_DTYPE), pk_v[layer].T.astype(_MXU_DTYPE),
                          tb_route=tiles["route_tb"], tb_dense=tiles["dense_tb"], a_t=tiles["dense_at"])
    return x2d.reshape(b, s, d)
```

```python
import functools
import math

import jax
import jax.numpy as jnp
from jax import lax
from jax.experimental import pallas as pl
from jax.experimental.pallas import tpu as pltpu

F32 = jnp.float32
BF16 = jnp.bfloat16
_MXU_DTYPE = BF16
_EPS = 1e-6
_LANES = 128
_SUBLANES = 8
_VMEM_LIMIT = 56 * 1024 * 1024

_NT = (((1,), (1,)), ((), ()))


def _params(sem, flags=None):
    return pltpu.CompilerParams(dimension_semantics=sem, vmem_limit_bytes=_VMEM_LIMIT, flags=flags)


def _rms(x, gain):
    return x * lax.rsqrt(jnp.mean(x * x, axis=-1, keepdims=True) + _EPS) * gain


def _merge_desc(c):
    c = list(c)
    n = len(c)
    j = n // 2
    while j >= 1:
        for i in range(n):
            l = i ^ j
            if l > i:
                hi = jnp.maximum(c[i], c[l])
                lo = jnp.minimum(c[i], c[l])
                c[i], c[l] = hi, lo
        j //= 2
    return c


def _sort_desc(xs):
    xs = list(xs)
    n = len(xs)
    k = 2
    while k <= n:
        j = k // 2
        while j >= 1:
            for i in range(n):
                l = i ^ j
                if l > i:
                    hi = jnp.maximum(xs[i], xs[l])
                    lo = jnp.minimum(xs[i], xs[l])
                    if (i & k) == 0:
                        xs[i], xs[l] = hi, lo
                    else:
                        xs[i], xs[l] = lo, hi
            j //= 2
        k *= 2
    return xs


def _top_of_two(a, b):
    n = len(a)
    return _merge_desc([jnp.maximum(a[i], b[n - 1 - i]) for i in range(n)])


_PK_TOPK = 16
_PK_PAIRS = [(i, j) for i in range(_PK_TOPK) for j in range(_PK_TOPK) if (i + 1) * (j + 1) <= _PK_TOPK]


_NEG = -3.0e38


def _top16_sums(a_top, b_top):
    cand = [a_top[i] + b_top[j] for (i, j) in _PK_PAIRS]
    pad = jnp.full_like(cand[0], _NEG)
    cand = cand + [pad] * (64 - len(cand))
    groups = [_sort_desc(cand[16 * g:16 * (g + 1)]) for g in range(4)]
    return _top_of_two(_top_of_two(groups[0], groups[1]), _top_of_two(groups[2], groups[3]))


def _gate_word(x, dtype):
    if dtype == F32:
        return x
    u = lax.bitcast_convert_type(x.astype(BF16).astype(F32), jnp.uint32)
    return u | (u >> 16)


def _gate_rows(ref, hd, a, lanes, wdt):
    row = jnp.broadcast_to(ref[hd, pl.ds(a, 1), lanes], (_SUBLANES, _LANES))
    return row if wdt == F32 else pltpu.bitcast(row, wdt)


def _peer_route_kernel(x_ref, g_ref, wq_ref, keys_ref, h_ref, ea_ref, ca_ref, eb_ref, rb_ref,
                       s_scr, top_scr, btop_scr, *, heads, nkeys):
    tb = x_ref.shape[0]
    hf = _rms(x_ref[...], g_ref[...])
    hb = hf.astype(_MXU_DTYPE)
    h_ref[...] = hf.T.astype(_MXU_DTYPE)
    q = jnp.dot(hb, wq_ref[...], preferred_element_type=F32)
    for hc in range(2 * heads):
        qs = q[:, hc * nkeys:(hc + 1) * nkeys].astype(_MXU_DTYPE)
        s_scr[hc] = lax.dot_general(keys_ref[hc], qs, _NT, preferred_element_type=F32)

    nrow = nkeys // _SUBLANES
    bcast = lambda row: jnp.broadcast_to(row, (_SUBLANES, _LANES))

    def group(gi, carry):
        lanes = pl.ds(pl.multiple_of(gi * _LANES, _LANES), _LANES)
        for hc in range(2 * heads):
            hd, c = divmod(hc, 2)
            rows = [s_scr[hc, pl.ds(_SUBLANES * m, _SUBLANES), lanes] for m in range(nrow)]
            srt = _sort_desc(rows)
            for shift in (4, 2, 1):
                srt = _top_of_two(srt, [pltpu.roll(r, shift, 0) for r in srt])
            for i in range(_PK_TOPK):
                top_scr[c, i, pl.ds(hd, 1), :] = srt[i][0:1, :]
            if c == 1:
                for i in range(_PK_TOPK):
                    btop_scr[hd, i] = srt[i]
                ebs, rbs = [], []
                for m in range(nrow):
                    ebs.append(jnp.exp(rows[m] - srt[0]))
                    rank = jnp.full_like(rows[m], float(_PK_TOPK))
                    for i in reversed(range(_PK_TOPK)):
                        rank = jnp.where(srt[i] <= rows[m], float(i), rank)
                    rbs.append(rank)
                for m in range(0, nrow, 2):
                    sl = pl.ds(_SUBLANES * m, 2 * _SUBLANES)
                    eb_ref[hd, sl, lanes] = jnp.concatenate(ebs[m:m + 2], axis=0).astype(eb_ref.dtype)
                    rb_ref[hd, sl, lanes] = jnp.concatenate(rbs[m:m + 2], axis=0).astype(rb_ref.dtype)
        a_s = [top_scr[0, i] for i in range(_PK_TOPK)]
        b_s = [top_scr[1, i] for i in range(_PK_TOPK)]
        top = _top16_sums(a_s, b_s)
        z = jnp.exp(top[0] - top[0])
        for t in top[1:]:
            z = z + jnp.exp(t - top[0])
        rz = 1.0 / z
        theta = top[_PK_TOPK - 1]
        for hd in range(heads):
            amax = bcast(a_s[0][hd:hd + 1, :])
            rz_h = bcast(rz[hd:hd + 1, :])
            th_h = bcast(theta[hd:hd + 1, :])
            for m in range(nrow):
                sl = pl.ds(_SUBLANES * m, _SUBLANES)
                sa = s_scr[2 * hd, sl, lanes]
                ea_ref[hd, sl, lanes] = _gate_word(jnp.exp(sa - amax) * rz_h, ea_ref.dtype)
                cnt = jnp.full_like(sa, float(_PK_TOPK))
                for j in reversed(range(_PK_TOPK)):
                    cnt = jnp.where(sa + btop_scr[hd, j] < th_h, float(j), cnt)
                ca_ref[hd, sl, lanes] = _gate_word(cnt, ca_ref.dtype)
        return carry

    lax.fori_loop(0, tb // _LANES, group, 0)


def _peer_route(x2d, gain, wq, keys, *, tb):
    t, d = x2d.shape
    hc, nkeys, dk = keys.shape
    heads = hc // 2
    kern = functools.partial(_peer_route_kernel, heads=heads, nkeys=nkeys)
    tab = pl.BlockSpec((heads, nkeys, tb), lambda i: (0, 0, i))
    word = F32 if _MXU_DTYPE == F32 else jnp.uint32
    return pl.pallas_call(
        kern,
        grid=(t // tb,),
        in_specs=[
            pl.BlockSpec((tb, d), lambda i: (i, 0)),
            pl.BlockSpec((1, d), lambda i: (0, 0)),
            pl.BlockSpec(wq.shape, lambda i: (0, 0)),
            pl.BlockSpec(keys.shape, lambda i: (0, 0, 0)),
        ],
        out_specs=[pl.BlockSpec((d, tb), lambda i: (0, i)), tab, tab, tab, tab],
        out_shape=[
            jax.ShapeDtypeStruct((d, t), _MXU_DTYPE),
            jax.ShapeDtypeStruct((heads, nkeys, t), word),
            jax.ShapeDtypeStruct((heads, nkeys, t), word),
            jax.ShapeDtypeStruct((heads, nkeys, t), _MXU_DTYPE),
            jax.ShapeDtypeStruct((heads, nkeys, t), _MXU_DTYPE),
        ],
        scratch_shapes=[
            pltpu.VMEM((hc, nkeys, tb), F32),
            pltpu.VMEM((2, _PK_TOPK, heads, _LANES), F32),
            pltpu.VMEM((heads, _PK_TOPK, _SUBLANES, _LANES), F32),
        ],
        compiler_params=_params(("parallel",)),
        name="peer_route",
    )(x2d, gain, wq, keys)


def _gelu(x):
    return 0.5 * x * (1.0 + lax.erf(x * (1.0 / math.sqrt(2.0))))


def _peer_dense_kernel(h_ref, u_ref, vt_ref, ea_ref, ca_ref, eb_ref, rb_ref, x_ref, o_ref, acc_ref,
                       hu_ref, w_ref, *, heads, nkeys):
    e = pl.program_id(1)
    tb = h_ref.shape[1]
    a_t = ea_ref.shape[1]
    wdt = vt_ref.dtype
    pk = _SUBLANES * (4 // jnp.dtype(wdt).itemsize)
    zero = jnp.zeros((pk, _LANES), wdt)
    mc = _pick(tb, (512, 256))

    @pl.when(e == 0)
    def _():
        acc_ref[...] = jnp.zeros_like(acc_ref)

    mr = 2 * nkeys
    for c in range(tb // mc):
        toks = slice(c * mc, (c + 1) * mc)
        for r in range(a_t * nkeys // mr):
            rows = slice(r * mr, (r + 1) * mr)
            hu_ref[rows, toks] = jnp.dot(u_ref[rows, :], h_ref[:, toks], preferred_element_type=F32)

    for c in range(tb // mc):
        toks = slice(c * mc, (c + 1) * mc)
        for a0 in range(0, a_t, 2):
            for tc in range(mc // _LANES):
                lanes = slice(c * mc + tc * _LANES, c * mc + (tc + 1) * _LANES)
                accs = [[None] * (nkeys // pk) for _ in range(2)]
                for hd in range(heads):
                    ea_b = [_gate_rows(ea_ref, hd, a0 + i, lanes, wdt) for i in range(2)]
                    ca_b = [_gate_rows(ca_ref, hd, a0 + i, lanes, wdt) for i in range(2)]
                    for bc in range(nkeys // pk):
                        rows = slice(bc * pk, (bc + 1) * pk)
                        rbv, ebv = rb_ref[hd, rows, lanes], eb_ref[hd, rows, lanes]
                        for i in range(2):
                            term = ea_b[i] * lax.clamp(zero, ca_b[i] - rbv, ebv)
                            accs[i][bc] = term if accs[i][bc] is None else accs[i][bc] + term
                for i in range(2):
                    for bc in range(nkeys // pk):
                        erows = slice((a0 + i) * nkeys + bc * pk, (a0 + i) * nkeys + (bc + 1) * pk)
                        w_ref[erows, lanes] = accs[i][bc] * _gelu(hu_ref[erows, lanes].astype(wdt))
        acc_ref[:, toks] += jnp.dot(vt_ref[...], w_ref[:, toks], preferred_element_type=F32)

    @pl.when(e == pl.num_programs(1) - 1)
    def _():
        o_ref[...] = x_ref[...] + acc_ref[...].T


def _peer_dense(h, u, vt, ea, ca, eb, rb, x2d, *, tb, a_t):
    t, d = x2d.shape
    heads, nkeys, _ = ea.shape
    e_t = a_t * nkeys
    n_e = u.shape[0] // e_t
    kern = functools.partial(_peer_dense_kernel, heads=heads, nkeys=nkeys)
    gate_a = pl.BlockSpec((heads, a_t, tb), lambda i, e: (0, e, i))
    gate_b = pl.BlockSpec((heads, nkeys, tb), lambda i, e: (0, 0, i))
    return pl.pallas_call(
        kern,
        grid=(t // tb, n_e),
        in_specs=[
            pl.BlockSpec((d, tb), lambda i, e: (0, i)),
            pl.BlockSpec((e_t, d), lambda i, e: (e, 0)),
            pl.BlockSpec((d, e_t), lambda i, e: (0, e)),
            gate_a, gate_a, gate_b, gate_b,
            pl.BlockSpec((tb, d), lambda i, e: (i, 0)),
        ],
        out_specs=pl.BlockSpec((tb, d), lambda i, e: (i, 0)),
        out_shape=jax.ShapeDtypeStruct((t, d), F32),
        scratch_shapes=[pltpu.VMEM((d, tb), F32), pltpu.VMEM((e_t, tb), F32), pltpu.VMEM((e_t, tb), _MXU_DTYPE)],
        compiler_params=_params(("parallel", "arbitrary")),
        name="peer_dense",
    )(h, u, vt, ea, ca, eb, rb, x2d)


def _peer_layer(x2d, gain, wq, keys, u, vt, *, tb_route, tb_dense, a_t):
    h, ea, ca, eb, rb = _peer_route(x2d, gain, wq, keys, tb=tb_route)
    return _peer_dense(h, u, vt, ea, ca, eb, rb, x2d, tb=tb_dense, a_t=a_t)


def _norm_proj_kernel(x_ref, g_ref, w_ref, o_ref, hb_ref):
    @pl.when(pl.program_id(1) == 0)
    def _():
        hb_ref[...] = _rms(x_ref[...], g_ref[...]).astype(hb_ref.dtype)

    o_ref[...] = jnp.dot(hb_ref[...], w_ref[...], preferred_element_type=F32)


def _norm_proj(x2d, gain, w, *, tm, tn):
    t, d = x2d.shape
    n = w.shape[1]
    return pl.pallas_call(
        _norm_proj_kernel,
        grid=(t // tm, n // tn),
        in_specs=[
            pl.BlockSpec((tm, d), lambda i, j: (i, 0)),
            pl.BlockSpec((1, d), lambda i, j: (0, 0)),
            pl.BlockSpec((d, tn), lambda i, j: (0, j)),
        ],
        out_specs=pl.BlockSpec((tm, tn), lambda i, j: (i, j)),
        out_shape=jax.ShapeDtypeStruct((t, n), F32),
        scratch_shapes=[pltpu.VMEM((tm, d), _MXU_DTYPE)],
        compiler_params=_params(("parallel", "arbitrary")),
        name="norm_proj",
    )(x2d, gain, w)


def _gated_out_kernel(o_ref, g_ref, gain_ref, w_ref, x_ref, y_ref, a_ref, *, dv):
    vd = o_ref.shape[1]
    for hd in range(vd // dv):
        cols = slice(hd * dv, (hd + 1) * dv)
        o = o_ref[:, cols]
        g = g_ref[:, cols]
        a_ref[:, cols] = (_rms(o, gain_ref[:, cols]) * (g * jax.nn.sigmoid(g))).astype(a_ref.dtype)
    y_ref[...] = x_ref[...] + jnp.dot(a_ref[...], w_ref[...], preferred_element_type=F32)


def _gated_out(o, p, g_block, gain, w, x2d, *, dv, tm):
    t, vd = o.shape
    d = x2d.shape[1]
    return pl.pallas_call(
        functools.partial(_gated_out_kernel, dv=dv),
        grid=(t // tm,),
        in_specs=[
            pl.BlockSpec((tm, vd), lambda i: (i, 0)),
            pl.BlockSpec((tm, vd), lambda i: (i, g_block)),
            pl.BlockSpec((1, vd), lambda i: (0, 0)),
            pl.BlockSpec((vd, d), lambda i: (0, 0)),
            pl.BlockSpec((tm, d), lambda i: (i, 0)),
        ],
        out_specs=pl.BlockSpec((tm, d), lambda i: (i, 0)),
        out_shape=jax.ShapeDtypeStruct((t, d), F32),
        scratch_shapes=[pltpu.VMEM((tm, vd), _MXU_DTYPE)],
        compiler_params=_params(("parallel",)),
        name="gated_out",
    )(o, p, gain, w, x2d)


_ROPE_THETA = 10000.0


def _rope_kernel(pos_ref, cr_ref, sr_ref, ta_ref, *, rt_half, at_half):
    pos = pos_ref[...]
    lane = lax.broadcasted_iota(jnp.int32, pos.shape, 1)
    inv = jnp.exp(lane.astype(F32) * (-math.log(_ROPE_THETA) / rt_half))
    ang = pos * inv
    cr_ref[...] = jnp.cos(ang)
    sr_ref[...] = jnp.sin(ang)
    j = lane % (2 * at_half)
    inv = jnp.exp((j % at_half).astype(F32) * (-math.log(_ROPE_THETA) / at_half))
    ang = pos * inv
    sn = jnp.sin(ang)
    ta_ref[:, :_LANES] = jnp.cos(ang)
    ta_ref[:, _LANES:2 * _LANES] = jnp.where(j < at_half, -sn, 0.0)
    ta_ref[:, 2 * _LANES:] = jnp.where(j < at_half, 0.0, sn)


def _rope_tables(pos_b, *, rt_half, at_half, tm):
    t = pos_b.shape[0]
    spec = pl.BlockSpec((tm, _LANES), lambda i: (i, 0))
    return pl.pallas_call(
        functools.partial(_rope_kernel, rt_half=rt_half, at_half=at_half),
        grid=(t // tm,),
        in_specs=[spec],
        out_specs=[spec, spec, pl.BlockSpec((tm, 3 * _LANES), lambda i: (i, 0))],
        out_shape=[jax.ShapeDtypeStruct((t, _LANES), F32)] * 2 + [jax.ShapeDtypeStruct((t, 3 * _LANES), F32)],
        compiler_params=_params(("parallel",)),
        name="rope_tables",
    )(pos_b)


_HG_CHUNK = 64
_HG_SUB = 16
_TINY = 1e-30
_TN = (((0,), (0,)), ((), ()))


def _split_dot(a, b_f32):
    hi = b_f32.astype(BF16)
    lo = (b_f32 - hi.astype(F32)).astype(BF16)
    return (jnp.dot(a, hi, preferred_element_type=F32) + jnp.dot(a, lo, preferred_element_type=F32))


def _hgrn_kernel(q_ref, f_ref, i_ref, lb_ref, o_ref, st_ref, *, layer, heads):
    c, sc = _HG_CHUNK, _HG_SUB
    nchunk = q_ref.shape[0] // c
    dk = q_ref.shape[1] // heads

    @pl.when(pl.program_id(1) == 0)
    def _():
        st_ref[...] = jnp.zeros_like(st_ref)

    lbs = lb_ref[...]
    e = jnp.exp(lbs - jnp.max(lbs, axis=0, keepdims=True))
    soft = e / jnp.sum(e, axis=0, keepdims=True)
    lb_all = jnp.zeros_like(soft[0:1])
    for l in range(1, layer + 1):
        lb_all = lb_all + soft[l:l + 1]

    r_i = lax.broadcasted_iota(jnp.int32, (c, c), 0)
    c_i = lax.broadcasted_iota(jnp.int32, (c, c), 1)
    tri = (r_i >= c_i).astype(BF16)
    sub_t = lax.broadcasted_iota(jnp.int32, (sc, 1), 0)

    def head_chunk(rows, hd):
        cols = slice(hd * dk, (hd + 1) * dk)
        lb = lb_all[:, cols]
        qr = q_ref[rows, cols]
        fr = f_ref[rows, cols]
        v = i_ref[rows, cols]
        q = qr * jax.nn.sigmoid(qr)
        forget = lb + (1.0 - lb) * jax.nn.sigmoid(fr)
        logf = jnp.log(jnp.maximum(forget, _TINY))
        k = (1.0 - lb) * jax.nn.sigmoid(-fr)
        if _MXU_DTYPE == BF16:
            b = _split_dot(tri, logf)
        else:
            b = jnp.dot(tri.astype(F32), logf, preferred_element_type=F32)
        st = st_ref[hd]
        o = lax.dot_general((q * jnp.exp(b)).astype(_MXU_DTYPE), st.astype(_MXU_DTYPE), _NT,
                            preferred_element_type=F32)
        vb = v.astype(_MXU_DTYPE)
        outs = []
        for i in range(c // sc):
            lo, hi = i * sc, (i + 1) * sc
            qi, bi = q[lo:hi], b[lo:hi]
            oi = o[lo:hi]
            if i > 0:
                ref_b = b[lo - 1:lo]
                qt = (qi * jnp.exp(bi - ref_b)).astype(_MXU_DTYPE)
                kt = (k[:lo] * jnp.exp(ref_b - b[:lo])).astype(_MXU_DTYPE)
                att = lax.dot_general(qt, kt, _NT, preferred_element_type=F32)
                oi = oi + jnp.dot(att.astype(_MXU_DTYPE), vb[:lo], preferred_element_type=F32)
            for s in range(sc):
                dec = jnp.exp(bi - bi[s:s + 1])
                col = jnp.sum(qi * k[lo + s:lo + s + 1] * dec, axis=1, keepdims=True)
                col = jnp.where(sub_t >= s, col, 0.0)
                oi = oi + col * v[lo + s:lo + s + 1]
            outs.append(oi)
        o_ref[rows, cols] = jnp.concatenate(outs, axis=0)
        b_last = b[c - 1:c]
        kd = (k * jnp.exp(b_last - b)).astype(_MXU_DTYPE)
        st_ref[hd] = jnp.exp(b_last) * st + lax.dot_general(vb, kd, _TN, preferred_element_type=F32)

    def chunk(ci, carry):
        rows = pl.ds(pl.multiple_of(ci * c, c), c)
        for hd in range(heads):
            head_chunk(rows, hd)
        return carry

    lax.fori_loop(0, nchunk, chunk, 0, unroll=2)


def _hgrn_scan(p, hg_lb, *, batch, seq, heads, dk, layer, blk):
    t = p.shape[0]
    nblk = seq // blk
    width = heads * dk
    row = lambda b, s: b * nblk + s
    return pl.pallas_call(
        functools.partial(_hgrn_kernel, layer=layer, heads=heads),
        grid=(batch, nblk),
        in_specs=[
            pl.BlockSpec((blk, width), lambda b, s: (row(b, s), 0)),
            pl.BlockSpec((blk, width), lambda b, s: (row(b, s), 1)),
            pl.BlockSpec((blk, width), lambda b, s: (row(b, s), 2)),
            pl.BlockSpec((hg_lb.shape[0], width), lambda b, s: (0, 0)),
        ],
        out_specs=pl.BlockSpec((blk, width), lambda b, s: (row(b, s), 0)),
        out_shape=jax.ShapeDtypeStruct((t, width), F32),
        scratch_shapes=[pltpu.VMEM((heads, dk, dk), F32)],
        compiler_params=_params(("parallel", "arbitrary")),
        name="hgrn_scan",
    )(p, p, p, hg_lb)


_RT_CHUNK = 128


def _retention_kernel(q_ref, k_ref, v_ref, cos_ref, sin_ref, o_ref, st_ref, *, heads):
    c = _RT_CHUNK
    nchunk = q_ref.shape[0] // c
    dk = q_ref.shape[1] // heads
    dv = v_ref.shape[1] // heads
    half = dk // 2

    @pl.when(pl.program_id(1) == 0)
    def _():
        st_ref[...] = jnp.zeros_like(st_ref)

    t_i = lax.broadcasted_iota(jnp.int32, (c, c), 0)
    s_i = lax.broadcasted_iota(jnp.int32, (c, c), 1)
    rel = (t_i - s_i).astype(F32)
    idx = lax.broadcasted_iota(jnp.int32, (c, _LANES), 0).astype(F32)
    consts = []
    for hd in range(heads):
        lg = math.log1p(-(2.0 ** (-5.0 - hd)))
        decay = jnp.where(rel >= 0, jnp.exp(lg * jnp.maximum(rel, 0.0)), 0.0)
        consts.append((decay, jnp.exp(lg * (idx + 1.0)), jnp.exp(lg * (c - 1.0 - idx)), math.exp(lg * c)))

    def rope(x, cos, sin):
        x1, x2 = x[:, :half], x[:, half:]
        return x1 * cos - x2 * sin, x2 * cos + x1 * sin

    cat = lambda a, b: jnp.concatenate([a, b], axis=1).astype(_MXU_DTYPE)

    def head_chunk(rows, hd, cos, sin):
        decay, q_dec, k_dec, chunk_dec = consts[hd]
        q1, q2 = rope(q_ref[rows, hd * dk:(hd + 1) * dk], cos, sin)
        k1, k2 = rope(k_ref[rows, hd * dk:(hd + 1) * dk], cos, sin)
        scale = dk ** -0.5
        k1, k2 = k1 * scale, k2 * scale
        qb, kb = cat(q1, q2), cat(k1, k2)
        vb = v_ref[rows, hd * dv:(hd + 1) * dv].astype(_MXU_DTYPE)
        att = lax.dot_general(qb, kb, _NT, preferred_element_type=F32) * decay
        o = jnp.dot(att.astype(_MXU_DTYPE), vb, preferred_element_type=F32)
        st = st_ref[hd]
        o = o + jnp.dot(cat(q1 * q_dec, q2 * q_dec), st.astype(_MXU_DTYPE), preferred_element_type=F32)
        o_ref[rows, hd * dv:(hd + 1) * dv] = o
        st_ref[hd] = chunk_dec * st + lax.dot_general(cat(k1 * k_dec, k2 * k_dec), vb, _TN,
                                                      preferred_element_type=F32)

    def chunk(ci, carry):
        rows = pl.ds(pl.multiple_of(ci * c, c), c)
        cos, sin = cos_ref[rows, :], sin_ref[rows, :]
        for hd in range(heads):
            head_chunk(rows, hd, cos, sin)
        return carry

    lax.fori_loop(0, nchunk, chunk, 0)


def _retention_scan(p, cos, sin, *, batch, seq, heads, dk, dv, blk):
    t = p.shape[0]
    nblk = seq // blk
    row = lambda b, s: b * nblk + s
    kd, vd = heads * dk, heads * dv
    assert 2 * kd == vd and dk // 2 == _LANES and _RT_CHUNK == _LANES
    return pl.pallas_call(
        functools.partial(_retention_kernel, heads=heads),
        grid=(batch, nblk),
        in_specs=[
            pl.BlockSpec((blk, kd), lambda b, s: (row(b, s), 0)),
            pl.BlockSpec((blk, kd), lambda b, s: (row(b, s), 1)),
            pl.BlockSpec((blk, vd), lambda b, s: (row(b, s), 1)),
            pl.BlockSpec((blk, _LANES), lambda b, s: (row(b, s), 0)),
            pl.BlockSpec((blk, _LANES), lambda b, s: (row(b, s), 0)),
        ],
        out_specs=pl.BlockSpec((blk, vd), lambda b, s: (row(b, s), 0)),
        out_shape=jax.ShapeDtypeStruct((t, vd), F32),
        scratch_shapes=[pltpu.VMEM((heads, dk, dv), F32)],
        compiler_params=_params(("parallel", "arbitrary")),
        name="retention_scan",
    )(p, p, p, cos, sin)


_AT_BLOCK = 128
_MASK_VALUE = -1e30


def _attn_kernel(q_ref, kp_ref, k_ref, vp_ref, v_ref, t0_ref, t1_ref, t2_ref, p0_ref, p1_ref, p2_ref,
                 gq_ref, gk_ref, o_ref, l_ref, *, head_dim, span, dil, nsub):
    blk = _AT_BLOCK
    half = head_dim // 2
    lane = lax.broadcasted_iota(jnp.int32, (1, _LANES), 1)
    qi = lax.broadcasted_iota(jnp.int32, (blk, 2 * blk), 0) + blk
    ki = lax.broadcasted_iota(jnp.int32, (blk, 2 * blk), 1)
    band = (ki <= qi) & (ki >= qi - span)
    first_lo = jnp.where(pl.program_id(2) == 0, blk, 0)
    band_first = (ki <= qi) & (ki >= jnp.maximum(qi - span, first_lo))
    scale = head_dim ** -0.5
    r_i = lax.broadcasted_iota(jnp.int32, (_LANES, _LANES), 0) // head_dim
    c_i = lax.broadcasted_iota(jnp.int32, (_LANES, _LANES), 1) // head_dim
    seg = jnp.where(r_i == c_i, 1.0 / head_dim, 0.0)
    gq, gk = gq_ref[...], gk_ref[...]

    nh = _LANES // head_dim
    gain_qk = jnp.concatenate([jnp.broadcast_to(gq, (blk, _LANES)), jnp.broadcast_to(gk, (blk, _LANES))], axis=0)
    heads_of = lane // head_dim
    segb = seg.astype(_MXU_DTYPE)

    def prep(x, gain, tab):
        msq = jnp.dot((x * x).astype(_MXU_DTYPE), segb, preferred_element_type=F32)
        y = x * lax.rsqrt(msq + _EPS) * gain
        cos, s1, s2 = tab
        y = y * cos + pltpu.roll(y, _LANES - half, 1) * s1 + pltpu.roll(y, half, 1) * s2
        return y.astype(_MXU_DTYPE)

    def residue(r, carry):
        def take(ref, j):
            if dil == 1:
                return ref[j * blk:(j + 1) * blk, :]
            return ref[pl.ds(r + j * blk * dil, blk, stride=dil), :]

        kprev = prep(take(kp_ref, 0), gk, [take(t, 0) for t in (p0_ref, p1_ref, p2_ref)])
        vprev = take(vp_ref, 0).astype(_MXU_DTYPE)
        for j in range(nsub):
            tab = [take(t, j) for t in (t0_ref, t1_ref, t2_ref)]
            qk = prep(jnp.concatenate([take(q_ref, j), take(k_ref, j)], axis=0), gain_qk,
                      [jnp.concatenate([t, t], axis=0) for t in tab])
            q, kcur = qk[:blk], qk[blk:]
            vcur = take(v_ref, j).astype(_MXU_DTYPE)
            kk = jnp.concatenate([kprev, kcur], axis=0)
            vv = jnp.concatenate([vprev, vcur], axis=0)
            mask = band_first if j == 0 else band
            qh = jnp.concatenate([jnp.where(heads_of == hd, q, jnp.zeros_like(q)) for hd in range(nh)], axis=0)
            s = lax.dot_general(qh, kk, _NT, preferred_element_type=F32) * scale
            s = jnp.where(jnp.concatenate([mask] * nh, axis=0), s, _MASK_VALUE)
            m = jnp.max(s, axis=-1, keepdims=True)
            pr = jnp.exp(s - m)
            l = jnp.sum(pr, axis=-1, keepdims=True)
            o = jnp.dot(pr.astype(_MXU_DTYPE), vv, preferred_element_type=F32) / l
            lse = m + jnp.log(l)
            o_pair = o[:blk]
            l_pair = jnp.broadcast_to(lse[:blk], o_pair.shape)
            for hd in range(1, nh):
                o_pair = jnp.where(heads_of == hd, o[hd * blk:(hd + 1) * blk], o_pair)
                l_pair = jnp.where(heads_of == hd, lse[hd * blk:(hd + 1) * blk], l_pair)
            if dil == 1:
                o_ref[j * blk:(j + 1) * blk, :] = o_pair
                l_ref[j * blk:(j + 1) * blk, :] = l_pair
            else:
                o_ref[pl.ds(r + j * blk * dil, blk, stride=dil), :] = o_pair
                l_ref[pl.ds(r + j * blk * dil, blk, stride=dil), :] = l_pair
            kprev, vprev = kcur, vcur
        return carry

    if dil == 1:
        residue(0, 0)
    else:
        per = 2 if nsub > 1 else 4

        def several(i, carry):
            for k in range(per):
                residue(per * i + k, carry)
            return carry

        lax.fori_loop(0, dil // per, several, 0)


def _attn_group(p, tabs, gq, gk, *, batch, seq, group, ngroup, heads, head_dim, window, dilation):
    t = p.shape[0]
    d = dilation
    width = heads * head_dim
    assert seq % (d * _AT_BLOCK) == 0 and window // d <= _AT_BLOCK
    nb = seq // d // _AT_BLOCK
    nsub = max(1, min(4, nb, 4096 // (_AT_BLOCK * d)))
    nstep = nb // nsub
    rows = nsub * _AT_BLOCK * d
    prow = _AT_BLOCK * d
    pairs = width // _LANES
    col = lambda kind, hp: (kind * ngroup + group) * pairs + hp
    main = lambda b, n: b * nstep + n
    prev = lambda b, n: jnp.maximum((b * nstep + n) * nsub - 1, 0)
    out_spec = pl.BlockSpec((rows, _LANES), lambda b, hp, n: (main(b, n), hp))
    return pl.pallas_call(
        functools.partial(_attn_kernel, head_dim=head_dim, span=window // d, dil=d, nsub=nsub),
        grid=(batch, pairs, nstep),
        in_specs=[
            pl.BlockSpec((rows, _LANES), lambda b, hp, n: (main(b, n), col(0, hp))),
            pl.BlockSpec((prow, _LANES), lambda b, hp, n: (prev(b, n), col(1, hp))),
            pl.BlockSpec((rows, _LANES), lambda b, hp, n: (main(b, n), col(1, hp))),
            pl.BlockSpec((prow, _LANES), lambda b, hp, n: (prev(b, n), col(2, hp))),
            pl.BlockSpec((rows, _LANES), lambda b, hp, n: (main(b, n), col(2, hp))),
            pl.BlockSpec((rows, _LANES), lambda b, hp, n: (main(b, n), 0)),
            pl.BlockSpec((rows, _LANES), lambda b, hp, n: (main(b, n), 1)),
            pl.BlockSpec((rows, _LANES), lambda b, hp, n: (main(b, n), 2)),
            pl.BlockSpec((prow, _LANES), lambda b, hp, n: (prev(b, n), 0)),
            pl.BlockSpec((prow, _LANES), lambda b, hp, n: (prev(b, n), 1)),
            pl.BlockSpec((prow, _LANES), lambda b, hp, n: (prev(b, n), 2)),
            pl.BlockSpec((1, _LANES), lambda b, hp, n: (0, 0)),
            pl.BlockSpec((1, _LANES), lambda b, hp, n: (0, 0)),
        ],
        out_specs=[out_spec, out_spec],
        out_shape=[jax.ShapeDtypeStruct((t, width), F32)] * 2,
        compiler_params=_params(("parallel", "parallel", "arbitrary")),
        name=f"attn_group{group}",
    )(p, p, p, p, p, tabs, tabs, tabs, tabs, tabs, tabs, gq, gk)


def _attn_out_kernel(*refs, ngroup):
    o_refs, l_refs = refs[:ngroup], refs[ngroup:2 * ngroup]
    w_ref, x_ref, y_ref = refs[2 * ngroup:]
    ls = [r[...] for r in l_refs]
    m = functools.reduce(jnp.maximum, ls)
    es = [jnp.exp(l - m) for l in ls]
    den = functools.reduce(lambda a, b: a + b, es)
    o = functools.reduce(lambda a, b: a + b, [e * r[...] for e, r in zip(es, o_refs)]) / den
    y_ref[...] = x_ref[...] + jnp.dot(o.astype(_MXU_DTYPE), w_ref[...], preferred_element_type=F32)


def _attn_out(os_, ls_, w, x2d, *, tm):
    t, width = os_[0].shape
    d = x2d.shape[1]
    ng = len(os_)
    blk = pl.BlockSpec((tm, width), lambda i: (i, 0))
    return pl.pallas_call(
        functools.partial(_attn_out_kernel, ngroup=ng),
        grid=(t // tm,),
        in_specs=[blk] * (2 * ng) + [pl.BlockSpec((width, d), lambda i: (0, 0)),
                                     pl.BlockSpec((tm, d), lambda i: (i, 0))],
        out_specs=pl.BlockSpec((tm, d), lambda i: (i, 0)),
        out_shape=jax.ShapeDtypeStruct((t, d), F32),
        compiler_params=_params(("parallel",)),
        name="attn_out",
    )(*os_, *ls_, w, x2d)


_HG_HEADS, _HG_DK = 8, 128
_AT_GROUPS = ((128, 1), (512, 4), (2048, 16))
_AT_HEADS, _AT_HEAD_DIM = 8, 64
_RT_HEADS = 4
_N_MIXERS = 3


def _pick(n, candidates):
    for c in candidates:
        if n % c == 0:
            return c
    return n


def _tiles(t, seq):
    return dict(
        proj_rows=_pick(t, (1024, 512, 256, 128)),
        out_rows=_pick(t, (512, 256, 128)),
        scan_blk=min(512, seq),
        route_tb=_pick(t, (512, 256, 128)),
        dense_tb=_pick(t, (512, 256, 128)),
        dense_at=16,
    )


def _hgrn_mixer(x2d, gain, w_in, hg_lb, o_gain, w_out, *, batch, seq, layer, tiles):
    p = _norm_proj(x2d, gain, w_in, tm=tiles["proj_rows"], tn=_pick(w_in.shape[1], (1024, 512)))
    o = _hgrn_scan(p, hg_lb, batch=batch, seq=seq, heads=_HG_HEADS, dk=_HG_DK, layer=layer, blk=tiles["scan_blk"])
    return _gated_out(o, p, 3, o_gain, w_out, x2d, dv=_HG_DK, tm=tiles["out_rows"])


def _retention_mixer(x2d, gain, w_in, o_gain, w_out, cos, sin, *, batch, seq, tiles):
    d = x2d.shape[1]
    dk = d // _RT_HEADS
    dv = 2 * dk
    p = _norm_proj(x2d, gain, w_in, tm=tiles["proj_rows"], tn=_pick(w_in.shape[1], (1536, 1024, 512)))
    o = _retention_scan(p, cos, sin, batch=batch, seq=seq, heads=_RT_HEADS, dk=dk, dv=dv, blk=tiles["scan_blk"])
    return _gated_out(o, p, 2, o_gain, w_out, x2d, dv=dv, tm=tiles["out_rows"])


def _attn_mixer(x2d, gain, w_in, q_gain, k_gain, w_out, tabs, *, batch, seq, tiles):
    ng = len(_AT_GROUPS)
    p = _norm_proj(x2d, gain, w_in, tm=tiles["proj_rows"], tn=_pick(w_in.shape[1], (1536, 512)))
    os_, ls_ = [], []
    for g, (window, dilation) in enumerate(_AT_GROUPS):
        gq = jnp.tile(q_gain[g], _LANES // _AT_HEAD_DIM)[None, :]
        gk = jnp.tile(k_gain[g], _LANES // _AT_HEAD_DIM)[None, :]
        o, lse = _attn_group(p, tabs, gq, gk, batch=batch, seq=seq, group=g, ngroup=ng, heads=_AT_HEADS,
                             head_dim=_AT_HEAD_DIM, window=window, dilation=dilation)
        os_.append(o)
        ls_.append(lse)
    return _attn_out(os_, ls_, w_out, x2d, tm=tiles["out_rows"])


def kernel(x, positions, mix_norm, ffn_norm, hg_lb, hg_w_in, hg_onorm, hg_w_out, at_w_in, at_qnorm, at_knorm,
           at_w_out, rt_w_in, rt_onorm, rt_w_out, pk_w_q, pk_keys, pk_u, pk_v):
    b, s, d = x.shape
    depth = mix_norm.shape[0]
    tiles = _tiles(b * s, s)
    x2d = x.reshape(b * s, d)
    pos_b = jnp.broadcast_to(positions.reshape(b * s, 1).astype(F32), (b * s, _LANES))
    rt_dk = d // _RT_HEADS
    cos_r, sin_r, tabs_a = _rope_tables(pos_b, rt_half=rt_dk // 2, at_half=_AT_HEAD_DIM // 2,
                                        tm=tiles["out_rows"])
    ia = ib = ic = 0
    for layer in range(depth):
        kind = layer % _N_MIXERS
        gain = mix_norm[layer][None, :]
        if kind == 0:
            x2d = _hgrn_mixer(x2d, gain, hg_w_in[ia].astype(_MXU_DTYPE), hg_lb, hg_onorm[ia][None, :],
                              hg_w_out[ia].astype(_MXU_DTYPE), batch=b, seq=s, layer=layer, tiles=tiles)
            ia += 1
        elif kind == 1:
            x2d = _attn_mixer(x2d, gain, at_w_in[ib].astype(_MXU_DTYPE), at_qnorm[ib], at_knorm[ib],
                              at_w_out[ib].astype(_MXU_DTYPE), tabs_a, batch=b, seq=s, tiles=tiles)
            ib += 1
        else:
            x2d = _retention_mixer(x2d, gain, rt_w_in[ic].astype(_MXU_DTYPE), rt_onorm[ic][None, :],
                                   rt_w_out[ic].astype(_MXU_DTYPE), cos_r, sin_r, batch=b, seq=s, tiles=tiles)
            ic += 1
        keys = pk_keys[layer].reshape(-1, pk_keys.shape[-2], pk_keys.shape[-1]).astype(_MXU_DTYPE)
        x2d = _peer_layer(x2d, ffn_norm[layer][None, :], pk_w_q[layer].astype(_MXU_DTYPE), keys,
                          pk_u[layer].astype(_MXU_DTYPE), pk_v[layer].T.astype(_MXU_DTYPE),
                          tb_route=tiles["route_tb"], tb_dense=tiles["dense_tb"], a_t=tiles["dense_at"])
    return x2d.reshape(b, s, d)
```

```python
import functools
import math

import jax
import jax.numpy as jnp
from jax import lax
from jax.experimental import pallas as pl
from jax.experimental.pallas import tpu as pltpu

F32 = jnp.float32
BF16 = jnp.bfloat16
_MXU_DTYPE = BF16
_EPS = 1e-6
_LANES = 128
_SUBLANES = 8
_VMEM_LIMIT = 56 * 1024 * 1024

_NT = (((1,), (1,)), ((), ()))


def _params(sem, flags=None):
    return pltpu.CompilerParams(dimension_semantics=sem, vmem_limit_bytes=_VMEM_LIMIT, flags=flags)


def _rms(x, gain):
    return x * lax.rsqrt(jnp.mean(x * x, axis=-1, keepdims=True) + _EPS) * gain


def _merge_desc(c):
    c = list(c)
    n = len(c)
    j = n // 2
    while j >= 1:
        for i in range(n):
            l = i ^ j
            if l > i:
                hi = jnp.maximum(c[i], c[l])
                lo = jnp.minimum(c[i], c[l])
                c[i], c[l] = hi, lo
        j //= 2
    return c


def _sort_desc(xs):
    xs = list(xs)
    n = len(xs)
    k = 2
    while k <= n:
        j = k // 2
        while j >= 1:
            for i in range(n):
                l = i ^ j
                if l > i:
                    hi = jnp.maximum(xs[i], xs[l])
                    lo = jnp.minimum(xs[i], xs[l])
                    if (i & k) == 0:
                        xs[i], xs[l] = hi, lo
                    else:
                        xs[i], xs[l] = lo, hi
            j //= 2
        k *= 2
    return xs


def _top_of_two(a, b):
    n = len(a)
    return _merge_desc([jnp.maximum(a[i], b[n - 1 - i]) for i in range(n)])


_PK_TOPK = 16
_PK_PAIRS = [(i, j) for i in range(_PK_TOPK) for j in range(_PK_TOPK) if (i + 1) * (j + 1) <= _PK_TOPK]


_NEG = -3.0e38


def _top16_sums(a_top, b_top):
    cand = [a_top[i] + b_top[j] for (i, j) in _PK_PAIRS]
    pad = jnp.full_like(cand[0], _NEG)
    cand = cand + [pad] * (64 - len(cand))
    groups = [_sort_desc(cand[16 * g:16 * (g + 1)]) for g in range(4)]
    return _top_of_two(_top_of_two(groups[0], groups[1]), _top_of_two(groups[2], groups[3]))


def _gate_word(x, dtype):
    if dtype == F32:
        return x
    u = lax.bitcast_convert_type(x.astype(BF16).astype(F32), jnp.uint32)
    return u | (u >> 16)


def _gate_rows(ref, hd, a, lanes, wdt):
    row = jnp.broadcast_to(ref[hd, pl.ds(a, 1), lanes], (_SUBLANES, _LANES))
    return row if wdt == F32 else pltpu.bitcast(row, wdt)


def _peer_route_kernel(x_ref, g_ref, wq_ref, keys_ref, h_ref, ea_ref, ca_ref, eb_ref, rb_ref,
                       s_scr, top_scr, btop_scr, *, heads, nkeys):
    tb = x_ref.shape[0]
    hf = _rms(x_ref[...], g_ref[...])
    hb = hf.astype(_MXU_DTYPE)
    h_ref[...] = hf.T.astype(_MXU_DTYPE)
    q = jnp.dot(hb, wq_ref[...], preferred_element_type=F32)
    for hc in range(2 * heads):
        qs = q[:, hc * nkeys:(hc + 1) * nkeys].astype(_MXU_DTYPE)
        s_scr[hc] = lax.dot_general(keys_ref[hc], qs, _NT, preferred_element_type=F32)

    nrow = nkeys // _SUBLANES
    bcast = lambda row: jnp.broadcast_to(row, (_SUBLANES, _LANES))

    def group(gi, carry):
        lanes = pl.ds(pl.multiple_of(gi * _LANES, _LANES), _LANES)
        for hc in range(2 * heads):
            hd, c = divmod(hc, 2)
            rows = [s_scr[hc, pl.ds(_SUBLANES * m, _SUBLANES), lanes] for m in range(nrow)]
            srt = _sort_desc(rows)
            for shift in (4, 2, 1):
                srt = _top_of_two(srt, [pltpu.roll(r, shift, 0) for r in srt])
            for i in range(_PK_TOPK):
                top_scr[c, i, pl.ds(hd, 1), :] = srt[i][0:1, :]
            if c == 1:
                for i in range(_PK_TOPK):
                    btop_scr[hd, i] = srt[i]
                ebs, rbs = [], []
                for m in range(nrow):
                    ebs.append(jnp.exp(rows[m] - srt[0]))
                    rank = jnp.full_like(rows[m], float(_PK_TOPK))
                    for i in reversed(range(_PK_TOPK)):
                        rank = jnp.where(srt[i] <= rows[m], float(i), rank)
                    rbs.append(rank)
                for m in range(0, nrow, 2):
                    sl = pl.ds(_SUBLANES * m, 2 * _SUBLANES)
                    eb_ref[hd, sl, lanes] = jnp.concatenate(ebs[m:m + 2], axis=0).astype(eb_ref.dtype)
                    rb_ref[hd, sl, lanes] = jnp.concatenate(rbs[m:m + 2], axis=0).astype(rb_ref.dtype)
        a_s = [top_scr[0, i] for i in range(_PK_TOPK)]
        b_s = [top_scr[1, i] for i in range(_PK_TOPK)]
        top = _top16_sums(a_s, b_s)
        z = jnp.exp(top[0] - top[0])
        for t in top[1:]:
            z = z + jnp.exp(t - top[0])
        rz = 1.0 / z
        theta = top[_PK_TOPK - 1]
        for hd in range(heads):
            amax = bcast(a_s[0][hd:hd + 1, :])
            rz_h = bcast(rz[hd:hd + 1, :])
            th_h = bcast(theta[hd:hd + 1, :])
            for m in range(nrow):
                sl = pl.ds(_SUBLANES * m, _SUBLANES)
                sa = s_scr[2 * hd, sl, lanes]
                ea_ref[hd, sl, lanes] = _gate_word(jnp.exp(sa - amax) * rz_h, ea_ref.dtype)
                cnt = jnp.full_like(sa, float(_PK_TOPK))
                for j in reversed(range(_PK_TOPK)):
                    cnt = jnp.where(sa + btop_scr[hd, j] < th_h, float(j), cnt)
                ca_ref[hd, sl, lanes] = _gate_word(cnt, ca_ref.dtype)
        return carry

    lax.fori_loop(0, tb // _LANES, group, 0)


def _peer_route(x2d, gain, wq, keys, *, tb):
    t, d = x2d.shape
    hc, nkeys, dk = keys.shape
    heads = hc // 2
    kern = functools.partial(_peer_route_kernel, heads=heads, nkeys=nkeys)
    tab = pl.BlockSpec((heads, nkeys, tb), lambda i: (0, 0, i))
    word = F32 if _MXU_DTYPE == F32 else jnp.uint32
    return pl.pallas_call(
        kern,
        grid=(t // tb,),
        in_specs=[
            pl.BlockSpec((tb, d), lambda i: (i, 0)),
            pl.BlockSpec((1, d), lambda i: (0, 0)),
            pl.BlockSpec(wq.shape, lambda i: (0, 0)),
            pl.BlockSpec(keys.shape, lambda i: (0, 0, 0)),
        ],
        out_specs=[pl.BlockSpec((d, tb), lambda i: (0, i)), tab, tab, tab, tab],
        out_shape=[
            jax.ShapeDtypeStruct((d, t), _MXU_DTYPE),
            jax.ShapeDtypeStruct((heads, nkeys, t), word),
            jax.ShapeDtypeStruct((heads, nkeys, t), word),
            jax.ShapeDtypeStruct((heads, nkeys, t), _MXU_DTYPE),
            jax.ShapeDtypeStruct((heads, nkeys, t), _MXU_DTYPE),
        ],
        scratch_shapes=[
            pltpu.VMEM((hc, nkeys, tb), F32),
            pltpu.VMEM((2, _PK_TOPK, heads, _LANES), F32),
            pltpu.VMEM((heads, _PK_TOPK, _SUBLANES, _LANES), F32),
        ],
        compiler_params=_params(("parallel",)),
        name="peer_route",
    )(x2d, gain, wq, keys)


def _gelu(x):
    return 0.5 * x * (1.0 + lax.erf(x * (1.0 / math.sqrt(2.0))))


def _peer_dense_kernel(h_ref, u_ref, vt_ref, ea_ref, ca_ref, eb_ref, rb_ref, x_ref, o_ref, acc_ref,
                       hu_ref, w_ref, *, heads, nkeys):
    e = pl.program_id(1)
    tb = h_ref.shape[1]
    a_t = ea_ref.shape[1]
    wdt = vt_ref.dtype
    pk = _SUBLANES * (4 // jnp.dtype(wdt).itemsize)
    zero = jnp.zeros((pk, _LANES), wdt)
    mc = _pick(tb, (512, 256))

    @pl.when(e == 0)
    def _():
        acc_ref[...] = jnp.zeros_like(acc_ref)

    mr = 2 * nkeys
    for c in range(tb // mc):
        toks = slice(c * mc, (c + 1) * mc)
        for r in range(a_t * nkeys // mr):
            rows = slice(r * mr, (r + 1) * mr)
            hu_ref[rows, toks] = jnp.dot(u_ref[rows, :], h_ref[:, toks], preferred_element_type=F32)

    for c in range(tb // mc):
        toks = slice(c * mc, (c + 1) * mc)
        for a0 in range(0, a_t, 2):
            for tc in range(mc // _LANES):
                lanes = slice(c * mc + tc * _LANES, c * mc + (tc + 1) * _LANES)
                accs = [[None] * (nkeys // pk) for _ in range(2)]
                for hd in range(heads):
                    ea_b = [_gate_rows(ea_ref, hd, a0 + i, lanes, wdt) for i in range(2)]
                    ca_b = [_gate_rows(ca_ref, hd, a0 + i, lanes, wdt) for i in range(2)]
                    for bc in range(nkeys // pk):
                        rows = slice(bc * pk, (bc + 1) * pk)
                        rbv, ebv = rb_ref[hd, rows, lanes], eb_ref[hd, rows, lanes]
                        for i in range(2):
                            term = ea_b[i] * lax.clamp(zero, ca_b[i] - rbv, ebv)
                            accs[i][bc] = term if accs[i][bc] is None else accs[i][bc] + term
                for i in range(2):
                    for bc in range(nkeys // pk):
                        erows = slice((a0 + i) * nkeys + bc * pk, (a0 + i) * nkeys + (bc + 1) * pk)
                        w_ref[erows, lanes] = accs[i][bc] * _gelu(hu_ref[erows, lanes].astype(wdt))
        acc_ref[:, toks] += jnp.dot(vt_ref[...], w_ref[:, toks], preferred_element_type=F32)

    @pl.when(e == pl.num_programs(1) - 1)
    def _():
        o_ref[...] = x_ref[...] + acc_ref[...].T


def _peer_dense(h, u, vt, ea, ca, eb, rb, x2d, *, tb, a_t):
    t, d = x2d.shape
    heads, nkeys, _ = ea.shape
    e_t = a_t * nkeys
    n_e = u.shape[0] // e_t
    kern = functools.partial(_peer_dense_kernel, heads=heads, nkeys=nkeys)
    gate_a = pl.BlockSpec((heads, a_t, tb), lambda i, e: (0, e, i))
    gate_b = pl.BlockSpec((heads, nkeys, tb), lambda i, e: (0, 0, i))
    return pl.pallas_call(
        kern,
        grid=(t // tb, n_e),
        in_specs=[
            pl.BlockSpec((d, tb), lambda i, e: (0, i)),
            pl.BlockSpec((e_t, d), lambda i, e: (e, 0)),
            pl.BlockSpec((d, e_t), lambda i, e: (0, e)),
            gate_a, gate_a, gate_b, gate_b,
            pl.BlockSpec((tb, d), lambda i, e: (i, 0)),
        ],
        out_specs=pl.BlockSpec((tb, d), lambda i, e: (i, 0)),
        out_shape=jax.ShapeDtypeStruct((t, d), F32),
        scratch_shapes=[pltpu.VMEM((d, tb), F32), pltpu.VMEM((e_t, tb), F32), pltpu.VMEM((e_t, tb), _MXU_DTYPE)],
        compiler_params=_params(("parallel", "arbitrary")),
        name="peer_dense",
    )(h, u, vt, ea, ca, eb, rb, x2d)


def _peer_layer(x2d, gain, wq, keys, u, vt, *, tb_route, tb_dense, a_t):
    h, ea, ca, eb, rb = _peer_route(x2d, gain, wq, keys, tb=tb_route)
    return _peer_dense(h, u, vt, ea, ca, eb, rb, x2d, tb=tb_dense, a_t=a_t)


def _norm_proj_kernel(x_ref, g_ref, w_ref, o_ref, hb_ref):
    @pl.when(pl.program_id(1) == 0)
    def _():
        hb_ref[...] = _rms(x_ref[...], g_ref[...]).astype(hb_ref.dtype)

    o_ref[...] = jnp.dot(hb_ref[...], w_ref[...], preferred_element_type=F32)


def _norm_proj(x2d, gain, w, *, tm, tn):
    t, d = x2d.shape
    n = w.shape[1]
    return pl.pallas_call(
        _norm_proj_kernel,
        grid=(t // tm, n // tn),
        in_specs=[
            pl.BlockSpec((tm, d), lambda i, j: (i, 0)),
            pl.BlockSpec((1, d), lambda i, j: (0, 0)),
            pl.BlockSpec((d, tn), lambda i, j: (0, j)),
        ],
        out_specs=pl.BlockSpec((tm, tn), lambda i, j: (i, j)),
        out_shape=jax.ShapeDtypeStruct((t, n), F32),
        scratch_shapes=[pltpu.VMEM((tm, d), _MXU_DTYPE)],
        compiler_params=_params(("parallel", "arbitrary")),
        name="norm_proj",
    )(x2d, gain, w)


def _gated_out_kernel(o_ref, g_ref, gain_ref, w_ref, x_ref, y_ref, a_ref, *, dv):
    vd = o_ref.shape[1]
    for hd in range(vd // dv):
        cols = slice(hd * dv, (hd + 1) * dv)
        o = o_ref[:, cols]
        g = g_ref[:, cols]
        a_ref[:, cols] = (_rms(o, gain_ref[:, cols]) * (g * jax.nn.sigmoid(g))).astype(a_ref.dtype)
    y_ref[...] = x_ref[...] + jnp.dot(a_ref[...], w_ref[...], preferred_element_type=F32)


def _gated_out(o, p, g_block, gain, w, x2d, *, dv, tm):
    t, vd = o.shape
    d = x2d.shape[1]
    return pl.pallas_call(
        functools.partial(_gated_out_kernel, dv=dv),
        grid=(t // tm,),
        in_specs=[
            pl.BlockSpec((tm, vd), lambda i: (i, 0)),
            pl.BlockSpec((tm, vd), lambda i: (i, g_block)),
            pl.BlockSpec((1, vd), lambda i: (0, 0)),
            pl.BlockSpec((vd, d), lambda i: (0, 0)),
            pl.BlockSpec((tm, d), lambda i: (i, 0)),
        ],
        out_specs=pl.BlockSpec((tm, d), lambda i: (i, 0)),
        out_shape=jax.ShapeDtypeStruct((t, d), F32),
        scratch_shapes=[pltpu.VMEM((tm, vd), _MXU_DTYPE)],
        compiler_params=_params(("parallel",)),
        name="gated_out",
    )(o, p, gain, w, x2d)


_ROPE_THETA = 10000.0


def _rope_kernel(pos_ref, cr_ref, sr_ref, ta_ref, *, rt_half, at_half):
    pos = pos_ref[...]
    lane = lax.broadcasted_iota(jnp.int32, pos.shape, 1)
    inv = jnp.exp(lane.astype(F32) * (-math.log(_ROPE_THETA) / rt_half))
    ang = pos * inv
    cr_ref[...] = jnp.cos(ang)
    sr_ref[...] = jnp.sin(ang)
    j = lane % (2 * at_half)
    inv = jnp.exp((j % at_half).astype(F32) * (-math.log(_ROPE_THETA) / at_half))
    ang = pos * inv
    sn = jnp.sin(ang)
    ta_ref[:, :_LANES] = jnp.cos(ang)
    ta_ref[:, _LANES:2 * _LANES] = jnp.where(j < at_half, -sn, 0.0)
    ta_ref[:, 2 * _LANES:] = jnp.where(j < at_half, 0.0, sn)


def _rope_tables(pos_b, *, rt_half, at_half, tm):
    t = pos_b.shape[0]
    spec = pl.BlockSpec((tm, _LANES), lambda i: (i, 0))
    return pl.pallas_call(
        functools.partial(_rope_kernel, rt_half=rt_half, at_half=at_half),
        grid=(t // tm,),
        in_specs=[spec],
        out_specs=[spec, spec, pl.BlockSpec((tm, 3 * _LANES), lambda i: (i, 0))],
        out_shape=[jax.ShapeDtypeStruct((t, _LANES), F32)] * 2 + [jax.ShapeDtypeStruct((t, 3 * _LANES), F32)],
        compiler_params=_params(("parallel",)),
        name="rope_tables",
    )(pos_b)


_HG_CHUNK = 64
_HG_SUB = 16
_TINY = 1e-30
_TN = (((0,), (0,)), ((), ()))


def _split_dot(a, b_f32):
    hi = b_f32.astype(BF16)
    lo = (b_f32 - hi.astype(F32)).astype(BF16)
    return (jnp.dot(a, hi, preferred_element_type=F32) + jnp.dot(a, lo, preferred_element_type=F32))


def _hgrn_kernel(q_ref, f_ref, i_ref, lb_ref, o_ref, st_ref, *, layer, heads):
    c, sc = _HG_CHUNK, _HG_SUB
    nchunk = q_ref.shape[0] // c
    dk = q_ref.shape[1] // heads

    @pl.when(pl.program_id(1) == 0)
    def _():
        st_ref[...] = jnp.zeros_like(st_ref)

    lbs = lb_ref[...]
    e = jnp.exp(lbs - jnp.max(lbs, axis=0, keepdims=True))
    soft = e / jnp.sum(e, axis=0, keepdims=True)
    lb_all = jnp.zeros_like(soft[0:1])
    for l in range(1, layer + 1):
        lb_all = lb_all + soft[l:l + 1]

    r_i = lax.broadcasted_iota(jnp.int32, (c, c), 0)
    c_i = lax.broadcasted_iota(jnp.int32, (c, c), 1)
    tri = (r_i >= c_i).astype(BF16)
    sub_t = lax.broadcasted_iota(jnp.int32, (sc, 1), 0)

    def head_chunk(rows, hd):
        cols = slice(hd * dk, (hd + 1) * dk)
        lb = lb_all[:, cols]
        qr = q_ref[rows, cols]
        fr = f_ref[rows, cols]
        v = i_ref[rows, cols]
        q = qr * jax.nn.sigmoid(qr)
        forget = lb + (1.0 - lb) * jax.nn.sigmoid(fr)
        logf = jnp.log(jnp.maximum(forget, _TINY))
        k = (1.0 - lb) * jax.nn.sigmoid(-fr)
        if _MXU_DTYPE == BF16:
            b = _split_dot(tri, logf)
        else:
            b = jnp.dot(tri.astype(F32), logf, preferred_element_type=F32)
        st = st_ref[hd]
        o = lax.dot_general((q * jnp.exp(b)).astype(_MXU_DTYPE), st.astype(_MXU_DTYPE), _NT,
                            preferred_element_type=F32)
        vb = v.astype(_MXU_DTYPE)
        outs = []
        for i in range(c // sc):
            lo, hi = i * sc, (i + 1) * sc
            qi, bi = q[lo:hi], b[lo:hi]
            oi = o[lo:hi]
            if i > 0:
                ref_b = b[lo - 1:lo]
                qt = (qi * jnp.exp(bi - ref_b)).astype(_MXU_DTYPE)
                kt = (k[:lo] * jnp.exp(ref_b - b[:lo])).astype(_MXU_DTYPE)
                att = lax.dot_general(qt, kt, _NT, preferred_element_type=F32)
                oi = oi + jnp.dot(att.astype(_MXU_DTYPE), vb[:lo], preferred_element_type=F32)
            for s in range(sc):
                dec = jnp.exp(bi - bi[s:s + 1])
                col = jnp.sum(qi * k[lo + s:lo + s + 1] * dec, axis=1, keepdims=True)
                col = jnp.where(sub_t >= s, col, 0.0)
                oi = oi + col * v[lo + s:lo + s + 1]
            outs.append(oi)
        o_ref[rows, cols] = jnp.concatenate(outs, axis=0)
        b_last = b[c - 1:c]
        kd = (k * jnp.exp(b_last - b)).astype(_MXU_DTYPE)
        st_ref[hd] = jnp.exp(b_last) * st + lax.dot_general(vb, kd, _TN, preferred_element_type=F32)

    def chunk(ci, carry):
        rows = pl.ds(pl.multiple_of(ci * c, c), c)
        for hd in range(heads):
            head_chunk(rows, hd)
        return carry

    lax.fori_loop(0, nchunk, chunk, 0, unroll=4)


def _hgrn_scan(p, hg_lb, *, batch, seq, heads, dk, layer, blk):
    t = p.shape[0]
    nblk = seq // blk
    width = heads * dk
    row = lambda b, s: b * nblk + s
    return pl.pallas_call(
        functools.partial(_hgrn_kernel, layer=layer, heads=heads),
        grid=(batch, nblk),
        in_specs=[
            pl.BlockSpec((blk, width), lambda b, s: (row(b, s), 0)),
            pl.BlockSpec((blk, width), lambda b, s: (row(b, s), 1)),
            pl.BlockSpec((blk, width), lambda b, s: (row(b, s), 2)),
            pl.BlockSpec((hg_lb.shape[0], width), lambda b, s: (0, 0)),
        ],
        out_specs=pl.BlockSpec((blk, width), lambda b, s: (row(b, s), 0)),
        out_shape=jax.ShapeDtypeStruct((t, width), F32),
        scratch_shapes=[pltpu.VMEM((heads, dk, dk), F32)],
        compiler_params=_params(("parallel", "arbitrary")),
        name="hgrn_scan",
    )(p, p, p, hg_lb)


_RT_CHUNK = 128


def _retention_kernel(q_ref, k_ref, v_ref, cos_ref, sin_ref, o_ref, st_ref, *, heads):
    c = _RT_CHUNK
    nchunk = q_ref.shape[0] // c
    dk = q_ref.shape[1] // heads
    dv = v_ref.shape[1] // heads
    half = dk // 2

    @pl.when(pl.program_id(1) == 0)
    def _():
        st_ref[...] = jnp.zeros_like(st_ref)

    t_i = lax.broadcasted_iota(jnp.int32, (c, c), 0)
    s_i = lax.broadcasted_iota(jnp.int32, (c, c), 1)
    rel = (t_i - s_i).astype(F32)
    idx = lax.broadcasted_iota(jnp.int32, (c, _LANES), 0).astype(F32)
    consts = []
    for hd in range(heads):
        lg = math.log1p(-(2.0 ** (-5.0 - hd)))
        decay = jnp.where(rel >= 0, jnp.exp(lg * jnp.maximum(rel, 0.0)), 0.0)
        consts.append((decay, jnp.exp(lg * (idx + 1.0)), jnp.exp(lg * (c - 1.0 - idx)), math.exp(lg * c)))

    def rope(x, cos, sin):
        x1, x2 = x[:, :half], x[:, half:]
        return x1 * cos - x2 * sin, x2 * cos + x1 * sin

    cat = lambda a, b: jnp.concatenate([a, b], axis=1).astype(_MXU_DTYPE)

    def head_chunk(rows, hd, cos, sin):
        decay, q_dec, k_dec, chunk_dec = consts[hd]
        q1, q2 = rope(q_ref[rows, hd * dk:(hd + 1) * dk], cos, sin)
        k1, k2 = rope(k_ref[rows, hd * dk:(hd + 1) * dk], cos, sin)
        scale = dk ** -0.5
        k1, k2 = k1 * scale, k2 * scale
        qb, kb = cat(q1, q2), cat(k1, k2)
        vb = v_ref[rows, hd * dv:(hd + 1) * dv].astype(_MXU_DTYPE)
        att = lax.dot_general(qb, kb, _NT, preferred_element_type=F32) * decay
        o = jnp.dot(att.astype(_MXU_DTYPE), vb, preferred_element_type=F32)
        st = st_ref[hd]
        o = o + jnp.dot(cat(q1 * q_dec, q2 * q_dec), st.astype(_MXU_DTYPE), preferred_element_type=F32)
        o_ref[rows, hd * dv:(hd + 1) * dv] = o
        st_ref[hd] = chunk_dec * st + lax.dot_general(cat(k1 * k_dec, k2 * k_dec), vb, _TN,
                                                      preferred_element_type=F32)

    def chunk(ci, carry):
        rows = pl.ds(pl.multiple_of(ci * c, c), c)
        cos, sin = cos_ref[rows, :], sin_ref[rows, :]
        for hd in range(heads):
            head_chunk(rows, hd, cos, sin)
        return carry

    lax.fori_loop(0, nchunk, chunk, 0)


def _retention_scan(p, cos, sin, *, batch, seq, heads, dk, dv, blk):
    t = p.shape[0]
    nblk = seq // blk
    row = lambda b, s: b * nblk + s
    kd, vd = heads * dk, heads * dv
    assert 2 * kd == vd and dk // 2 == _LANES and _RT_CHUNK == _LANES
    return pl.pallas_call(
        functools.partial(_retention_kernel, heads=heads),
        grid=(batch, nblk),
        in_specs=[
            pl.BlockSpec((blk, kd), lambda b, s: (row(b, s), 0)),
            pl.BlockSpec((blk, kd), lambda b, s: (row(b, s), 1)),
            pl.BlockSpec((blk, vd), lambda b, s: (row(b, s), 1)),
            pl.BlockSpec((blk, _LANES), lambda b, s: (row(b, s), 0)),
            pl.BlockSpec((blk, _LANES), lambda b, s: (row(b, s), 0)),
        ],
        out_specs=pl.BlockSpec((blk, vd), lambda b, s: (row(b, s), 0)),
        out_shape=jax.ShapeDtypeStruct((t, vd), F32),
        scratch_shapes=[pltpu.VMEM((heads, dk, dv), F32)],
        compiler_params=_params(("parallel", "arbitrary")),
        name="retention_scan",
    )(p, p, p, cos, sin)


_AT_BLOCK = 128
_MASK_VALUE = -1e30


def _attn_kernel(q_ref, kp_ref, k_ref, vp_ref, v_ref, t0_ref, t1_ref, t2_ref, p0_ref, p1_ref, p2_ref,
                 gq_ref, gk_ref, o_ref, l_ref, *, head_dim, span, dil, nsub):
    blk = _AT_BLOCK
    half = head_dim // 2
    lane = lax.broadcasted_iota(jnp.int32, (1, _LANES), 1)
    qi = lax.broadcasted_iota(jnp.int32, (blk, 2 * blk), 0) + blk
    ki = lax.broadcasted_iota(jnp.int32, (blk, 2 * blk), 1)
    band = (ki <= qi) & (ki >= qi - span)
    first_lo = jnp.where(pl.program_id(2) == 0, blk, 0)
    band_first = (ki <= qi) & (ki >= jnp.maximum(qi - span, first_lo))
    scale = head_dim ** -0.5
    r_i = lax.broadcasted_iota(jnp.int32, (_LANES, _LANES), 0) // head_dim
    c_i = lax.broadcasted_iota(jnp.int32, (_LANES, _LANES), 1) // head_dim
    seg = jnp.where(r_i == c_i, 1.0 / head_dim, 0.0)
    gq, gk = gq_ref[...], gk_ref[...]

    nh = _LANES // head_dim
    gain_qk = jnp.concatenate([jnp.broadcast_to(gq, (blk, _LANES)), jnp.broadcast_to(gk, (blk, _LANES))], axis=0)
    heads_of = lane // head_dim
    segb = seg.astype(_MXU_DTYPE)

    def prep(x, gain, tab):
        msq = jnp.dot((x * x).astype(_MXU_DTYPE), segb, preferred_element_type=F32)
        y = x * lax.rsqrt(msq + _EPS) * gain
        cos, s1, s2 = tab
        y = y * cos + pltpu.roll(y, _LANES - half, 1) * s1 + pltpu.roll(y, half, 1) * s2
        return y.astype(_MXU_DTYPE)

    def residue(r, carry):
        def take(ref, j):
            if dil == 1:
                return ref[j * blk:(j + 1) * blk, :]
            return ref[pl.ds(r + j * blk * dil, blk, stride=dil), :]

        kprev = prep(take(kp_ref, 0), gk, [take(t, 0) for t in (p0_ref, p1_ref, p2_ref)])
        vprev = take(vp_ref, 0).astype(_MXU_DTYPE)
        for j in range(nsub):
            tab = [take(t, j) for t in (t0_ref, t1_ref, t2_ref)]
            qk = prep(jnp.concatenate([take(q_ref, j), take(k_ref, j)], axis=0), gain_qk,
                      [jnp.concatenate([t, t], axis=0) for t in tab])
            q, kcur = qk[:blk], qk[blk:]
            vcur = take(v_ref, j).astype(_MXU_DTYPE)
            kk = jnp.concatenate([kprev, kcur], axis=0)
            vv = jnp.concatenate([vprev, vcur], axis=0)
            mask = band_first if j == 0 else band
            qh = jnp.concatenate([jnp.where(heads_of == hd, q, jnp.zeros_like(q)) for hd in range(nh)], axis=0)
            s = lax.dot_general(qh, kk, _NT, preferred_element_type=F32) * scale
            s = jnp.where(jnp.concatenate([mask] * nh, axis=0), s, _MASK_VALUE)
            m = jnp.max(s, axis=-1, keepdims=True)
            pr = jnp.exp(s - m)
            l = jnp.sum(pr, axis=-1, keepdims=True)
            o = jnp.dot(pr.astype(_MXU_DTYPE), vv, preferred_element_type=F32) / l
            lse = m + jnp.log(l)
            o_pair = o[:blk]
            l_pair = jnp.broadcast_to(lse[:blk], o_pair.shape)
            for hd in range(1, nh):
                o_pair = jnp.where(heads_of == hd, o[hd * blk:(hd + 1) * blk], o_pair)
                l_pair = jnp.where(heads_of == hd, lse[hd * blk:(hd + 1) * blk], l_pair)
            if dil == 1:
                o_ref[j * blk:(j + 1) * blk, :] = o_pair
                l_ref[j * blk:(j + 1) * blk, :] = l_pair
            else:
                o_ref[pl.ds(r + j * blk * dil, blk, stride=dil), :] = o_pair
                l_ref[pl.ds(r + j * blk * dil, blk, stride=dil), :] = l_pair
            kprev, vprev = kcur, vcur
        return carry

    if dil == 1:
        residue(0, 0)
    else:
        per = 2 if nsub > 1 else 4

        def several(i, carry):
            for k in range(per):
                residue(per * i + k, carry)
            return carry

        lax.fori_loop(0, dil // per, several, 0)


def _attn_group(p, tabs, gq, gk, *, batch, seq, group, ngroup, heads, head_dim, window, dilation):
    t = p.shape[0]
    d = dilation
    width = heads * head_dim
    assert seq % (d * _AT_BLOCK) == 0 and window // d <= _AT_BLOCK
    nb = seq // d // _AT_BLOCK
    nsub = max(1, min(4, nb, 4096 // (_AT_BLOCK * d)))
    nstep = nb // nsub
    rows = nsub * _AT_BLOCK * d
    prow = _AT_BLOCK * d
    pairs = width // _LANES
    col = lambda kind, hp: (kind * ngroup + group) * pairs + hp
    main = lambda b, n: b * nstep + n
    prev = lambda b, n: jnp.maximum((b * nstep + n) * nsub - 1, 0)
    out_spec = pl.BlockSpec((rows, _LANES), lambda b, hp, n: (main(b, n), hp))
    return pl.pallas_call(
        functools.partial(_attn_kernel, head_dim=head_dim, span=window // d, dil=d, nsub=nsub),
        grid=(batch, pairs, nstep),
        in_specs=[
            pl.BlockSpec((rows, _LANES), lambda b, hp, n: (main(b, n), col(0, hp))),
            pl.BlockSpec((prow, _LANES), lambda b, hp, n: (prev(b, n), col(1, hp))),
            pl.BlockSpec((rows, _LANES), lambda b, hp, n: (main(b, n), col(1, hp))),
            pl.BlockSpec((prow, _LANES), lambda b, hp, n: (prev(b, n), col(2, hp))),
            pl.BlockSpec((rows, _LANES), lambda b, hp, n: (main(b, n), col(2, hp))),
            pl.BlockSpec((rows, _LANES), lambda b, hp, n: (main(b, n), 0)),
            pl.BlockSpec((rows, _LANES), lambda b, hp, n: (main(b, n), 1)),
            pl.BlockSpec((rows, _LANES), lambda b, hp, n: (main(b, n), 2)),
            pl.BlockSpec((prow, _LANES), lambda b, hp, n: (prev(b, n), 0)),
            pl.BlockSpec((prow, _LANES), lambda b, hp, n: (prev(b, n), 1)),
            pl.BlockSpec((prow, _LANES), lambda b, hp, n: (prev(b, n), 2)),
            pl.BlockSpec((1, _LANES), lambda b, hp, n: (0, 0)),
            pl.BlockSpec((1, _LANES), lambda b, hp, n: (0, 0)),
        ],
        out_specs=[out_spec, out_spec],
        out_shape=[jax.ShapeDtypeStruct((t, width), F32)] * 2,
        compiler_params=_params(("parallel", "parallel", "arbitrary")),
        name=f"attn_group{group}",
    )(p, p, p, p, p, tabs, tabs, tabs, tabs, tabs, tabs, gq, gk)


def _attn_out_kernel(*refs, ngroup):
    o_refs, l_refs = refs[:ngroup], refs[ngroup:2 * ngroup]
    w_ref, x_ref, y_ref = refs[2 * ngroup:]
    ls = [r[...] for r in l_refs]
    m = functools.reduce(jnp.maximum, ls)
    es = [jnp.exp(l - m) for l in ls]
    den = functools.reduce(lambda a, b: a + b, es)
    o = functools.reduce(lambda a, b: a + b, [e * r[...] for e, r in zip(es, o_refs)]) / den
    y_ref[...] = x_ref[...] + jnp.dot(o.astype(_MXU_DTYPE), w_ref[...], preferred_element_type=F32)


def _attn_out(os_, ls_, w, x2d, *, tm):
    t, width = os_[0].shape
    d = x2d.shape[1]
    ng = len(os_)
    blk = pl.BlockSpec((tm, width), lambda i: (i, 0))
    return pl.pallas_call(
        functools.partial(_attn_out_kernel, ngroup=ng),
        grid=(t // tm,),
        in_specs=[blk] * (2 * ng) + [pl.BlockSpec((width, d), lambda i: (0, 0)),
                                     pl.BlockSpec((tm, d), lambda i: (i, 0))],
        out_specs=pl.BlockSpec((tm, d), lambda i: (i, 0)),
        out_shape=jax.ShapeDtypeStruct((t, d), F32),
        compiler_params=_params(("parallel",)),
        name="attn_out",
    )(*os_, *ls_, w, x2d)


_HG_HEADS, _HG_DK = 8, 128
_AT_GROUPS = ((128, 1), (512, 4), (2048, 16))
_AT_HEADS, _AT_HEAD_DIM = 8, 64
_RT_HEADS = 4
_N_MIXERS = 3


def _pick(n, candidates):
    for c in candidates:
        if n % c == 0:
            return c
    return n


def _tiles(t, seq):
    return dict(
        proj_rows=_pick(t, (1024, 512, 256, 128)),
        out_rows=_pick(t, (512, 256, 128)),
        scan_blk=min(512, seq),
        route_tb=_pick(t, (512, 256, 128)),
        dense_tb=_pick(t, (512, 256, 128)),
        dense_at=16,
    )


def _hgrn_mixer(x2d, gain, w_in, hg_lb, o_gain, w_out, *, batch, seq, layer, tiles):
    p = _norm_proj(x2d, gain, w_in, tm=tiles["proj_rows"], tn=_pick(w_in.shape[1], (1024, 512)))
    o = _hgrn_scan(p, hg_lb, batch=batch, seq=seq, heads=_HG_HEADS, dk=_HG_DK, layer=layer, blk=tiles["scan_blk"])
    return _gated_out(o, p, 3, o_gain, w_out, x2d, dv=_HG_DK, tm=tiles["out_rows"])


def _retention_mixer(x2d, gain, w_in, o_gain, w_out, cos, sin, *, batch, seq, tiles):
    d = x2d.shape[1]
    dk = d // _RT_HEADS
    dv = 2 * dk
    p = _norm_proj(x2d, gain, w_in, tm=tiles["proj_rows"], tn=_pick(w_in.shape[1], (1536, 1024, 512)))
    o = _retention_scan(p, cos, sin, batch=batch, seq=seq, heads=_RT_HEADS, dk=dk, dv=dv, blk=tiles["scan_blk"])
    return _gated_out(o, p, 2, o_gain, w_out, x2d, dv=dv, tm=tiles["out_rows"])


def _attn_mixer(x2d, gain, w_in, q_gain, k_gain, w_out, tabs, *, batch, seq, tiles):
    ng = len(_AT_GROUPS)
    p = _norm_proj(x2d, gain, w_in, tm=tiles["proj_rows"], tn=_pick(w_in.shape[1], (1536, 512)))
    os_, ls_ = [], []
    for g, (window, dilation) in enumerate(_AT_GROUPS):
        gq = jnp.tile(q_gain[g], _LANES // _AT_HEAD_DIM)[None, :]
        gk = jnp.tile(k_gain[g], _LANES // _AT_HEAD_DIM)[None, :]
        o, lse = _attn_group(p, tabs, gq, gk, batch=batch, seq=seq, group=g, ngroup=ng, heads=_AT_HEADS,
                             head_dim=_AT_HEAD_DIM, window=window, dilation=dilation)
        os_.append(o)
        ls_.append(lse)
    return _attn_out(os_, ls_, w_out, x2d, tm=tiles["out_rows"])


def kernel(x, positions, mix_norm, ffn_norm, hg_lb, hg_w_in, hg_onorm, hg_w_out, at_w_in, at_qnorm, at_knorm,
           at_w_out, rt_w_in, rt_onorm, rt_w_out, pk_w_q, pk_keys, pk_u, pk_v):
    b, s, d = x.shape
    depth = mix_norm.shape[0]
    tiles = _tiles(b * s, s)
    x2d = x.reshape(b * s, d)
    pos_b = jnp.broadcast_to(positions.reshape(b * s, 1).astype(F32), (b * s, _LANES))
    rt_dk = d // _RT_HEADS
    cos_r, sin_r, tabs_a = _rope_tables(pos_b, rt_half=rt_dk // 2, at_half=_AT_HEAD_DIM // 2,
                                        tm=tiles["out_rows"])
    ia = ib = ic = 0
    for layer in range(depth):
        kind = layer % _N_MIXERS
        gain = mix_norm[layer][None, :]
        if kind == 0:
            x2d = _hgrn_mixer(x2d, gain, hg_w_in[ia].astype(_MXU_DTYPE), hg_lb, hg_onorm[ia][None, :],
                              hg_w_out[ia].astype(_MXU_DTYPE), batch=b, seq=s, layer=layer, tiles=tiles)
            ia += 1
        elif kind == 1:
            x2d = _attn_mixer(x2d, gain, at_w_in[ib].astype(_MXU_DTYPE), at_qnorm[ib], at_knorm[ib],
                              at_w_out[ib].astype(_MXU_DTYPE), tabs_a, batch=b, seq=s, tiles=tiles)
            ib += 1
        else:
            x2d = _retention_mixer(x2d, gain, rt_w_in[ic].astype(_MXU_DTYPE), rt_onorm[ic][None, :],
                                   rt_w_out[ic].astype(_MXU_DTYPE), cos_r, sin_r, batch=b, seq=s, tiles=tiles)
            ic += 1
        keys = pk_keys[layer].reshape(-1, pk_keys.shape[-2], pk_keys.shape[-1]).astype(_MXU_DTYPE)
        x2d = _peer_layer(x2d, ffn_norm[layer][None, :], pk_w_q[layer].astype(_MXU_DTYPE), keys,
                          pk_u[layer].astype(_MXU_DTYPE), pk_v[layer].T.astype(_MXU_DTYPE),
                          tb_route=tiles["route_tb"], tb_dense=tiles["dense_tb"], a_t=tiles["dense_at"])
    return x2d.reshape(b, s, d)
```

```python
import functools
import math

import jax
import jax.numpy as jnp
from jax import lax
from jax.experimental import pallas as pl
from jax.experimental.pallas import tpu as pltpu

F32 = jnp.float32
BF16 = jnp.bfloat16
_MXU_DTYPE = BF16
_EPS = 1e-6
_LANES = 128
_SUBLANES = 8
_VMEM_LIMIT = 56 * 1024 * 1024

_NT = (((1,), (1,)), ((), ()))


def _params(sem, flags=None):
    return pltpu.CompilerParams(dimension_semantics=sem, vmem_limit_bytes=_VMEM_LIMIT, flags=flags)


def _rms(x, gain):
    return x * lax.rsqrt(jnp.mean(x * x, axis=-1, keepdims=True) + _EPS) * gain


def _merge_desc(c):
    c = list(c)
    n = len(c)
    j = n // 2
    while j >= 1:
        for i in range(n):
            l = i ^ j
            if l > i:
                hi = jnp.maximum(c[i], c[l])
                lo = jnp.minimum(c[i], c[l])
                c[i], c[l] = hi, lo
        j //= 2
    return c


def _sort_desc(xs):
    xs = list(xs)
    n = len(xs)
    k = 2
    while k <= n:
        j = k // 2
        while j >= 1:
            for i in range(n):
                l = i ^ j
                if l > i:
                    hi = jnp.maximum(xs[i], xs[l])
                    lo = jnp.minimum(xs[i], xs[l])
                    if (i & k) == 0:
                        xs[i], xs[l] = hi, lo
                    else:
                        xs[i], xs[l] = lo, hi
            j //= 2
        k *= 2
    return xs


def _top_of_two(a, b):
    n = len(a)
    return _merge_desc([jnp.maximum(a[i], b[n - 1 - i]) for i in range(n)])


_PK_TOPK = 16
_PK_PAIRS = [(i, j) for i in range(_PK_TOPK) for j in range(_PK_TOPK) if (i + 1) * (j + 1) <= _PK_TOPK]


_NEG = -3.0e38


def _top16_sums(a_top, b_top):
    cand = [a_top[i] + b_top[j] for (i, j) in _PK_PAIRS]
    pad = jnp.full_like(cand[0], _NEG)
    cand = cand + [pad] * (64 - len(cand))
    groups = [_sort_desc(cand[16 * g:16 * (g + 1)]) for g in range(4)]
    return _top_of_two(_top_of_two(groups[0], groups[1]), _top_of_two(groups[2], groups[3]))


def _gate_word(x, dtype):
    if dtype == F32:
        return x
    u = lax.bitcast_convert_type(x.astype(BF16).astype(F32), jnp.uint32)
    return u | (u >> 16)


def _gate_rows(ref, hd, a, lanes, wdt):
    row = jnp.broadcast_to(ref[hd, pl.ds(a, 1), lanes], (_SUBLANES, _LANES))
    return row if wdt == F32 else pltpu.bitcast(row, wdt)


def _peer_route_kernel(x_ref, g_ref, wq_ref, keys_ref, h_ref, ea_ref, ca_ref, eb_ref, rb_ref,
                       s_scr, top_scr, btop_scr, *, heads, nkeys):
    tb = x_ref.shape[0]
    hf = _rms(x_ref[...], g_ref[...])
    hb = hf.astype(_MXU_DTYPE)
    h_ref[...] = hf.T.astype(_MXU_DTYPE)
    q = jnp.dot(hb, wq_ref[...], preferred_element_type=F32)
    for hc in range(2 * heads):
        qs = q[:, hc * nkeys:(hc + 1) * nkeys].astype(_MXU_DTYPE)
        s_scr[hc] = lax.dot_general(keys_ref[hc], qs, _NT, preferred_element_type=F32)

    nrow = nkeys // _SUBLANES
    bcast = lambda row: jnp.broadcast_to(row, (_SUBLANES, _LANES))

    def group(gi, carry):
        lanes = pl.ds(pl.multiple_of(gi * _LANES, _LANES), _LANES)
        for hc in range(2 * heads):
            hd, c = divmod(hc, 2)
            rows = [s_scr[hc, pl.ds(_SUBLANES * m, _SUBLANES), lanes] for m in range(nrow)]
            srt = _sort_desc(rows)
            for shift in (4, 2, 1):
                srt = _top_of_two(srt, [pltpu.roll(r, shift, 0) for r in srt])
            for i in range(_PK_TOPK):
                top_scr[c, i, pl.ds(hd, 1), :] = srt[i][0:1, :]
            if c == 1:
                for i in range(_PK_TOPK):
                    btop_scr[hd, i] = srt[i]
                ebs, rbs = [], []
                for m in range(nrow):
                    ebs.append(jnp.exp(rows[m] - srt[0]))
                    rank = jnp.full_like(rows[m], float(_PK_TOPK))
                    for i in reversed(range(_PK_TOPK)):
                        rank = jnp.where(srt[i] <= rows[m], float(i), rank)
                    rbs.append(rank)
                for m in range(0, nrow, 2):
                    sl = pl.ds(_SUBLANES * m, 2 * _SUBLANES)
                    eb_ref[hd, sl, lanes] = jnp.concatenate(ebs[m:m + 2], axis=0).astype(eb_ref.dtype)
                    rb_ref[hd, sl, lanes] = jnp.concatenate(rbs[m:m + 2], axis=0).astype(rb_ref.dtype)
        a_s = [top_scr[0, i] for i in range(_PK_TOPK)]
        b_s = [top_scr[1, i] for i in range(_PK_TOPK)]
        top = _top16_sums(a_s, b_s)
        z = jnp.exp(top[0] - top[0])
        for t in top[1:]:
            z = z + jnp.exp(t - top[0])
        rz = 1.0 / z
        theta = top[_PK_TOPK - 1]
        for hd in range(heads):
            amax = bcast(a_s[0][hd:hd + 1, :])
            rz_h = bcast(rz[hd:hd + 1, :])
            th_h = bcast(theta[hd:hd + 1, :])
            for m in range(nrow):
                sl = pl.ds(_SUBLANES * m, _SUBLANES)
                sa = s_scr[2 * hd, sl, lanes]
                ea_ref[hd, sl, lanes] = _gate_word(jnp.exp(sa - amax) * rz_h, ea_ref.dtype)
                cnt = jnp.full_like(sa, float(_PK_TOPK))
                for j in reversed(range(_PK_TOPK)):
                    cnt = jnp.where(sa + btop_scr[hd, j] < th_h, float(j), cnt)
                ca_ref[hd, sl, lanes] = _gate_word(cnt, ca_ref.dtype)
        return carry

    lax.fori_loop(0, tb // _LANES, group, 0)


def _peer_route(x2d, gain, wq, keys, *, tb):
    t, d = x2d.shape
    hc, nkeys, dk = keys.shape
    heads = hc // 2
    kern = functools.partial(_peer_route_kernel, heads=heads, nkeys=nkeys)
    tab = pl.BlockSpec((heads, nkeys, tb), lambda i: (0, 0, i))
    word = F32 if _MXU_DTYPE == F32 else jnp.uint32
    return pl.pallas_call(
        kern,
        grid=(t // tb,),
        in_specs=[
            pl.BlockSpec((tb, d), lambda i: (i, 0)),
            pl.BlockSpec((1, d), lambda i: (0, 0)),
            pl.BlockSpec(wq.shape, lambda i: (0, 0)),
            pl.BlockSpec(keys.shape, lambda i: (0, 0, 0)),
        ],
        out_specs=[pl.BlockSpec((d, tb), lambda i: (0, i)), tab, tab, tab, tab],
        out_shape=[
            jax.ShapeDtypeStruct((d, t), _MXU_DTYPE),
            jax.ShapeDtypeStruct((heads, nkeys, t), word),
            jax.ShapeDtypeStruct((heads, nkeys, t), word),
            jax.ShapeDtypeStruct((heads, nkeys, t), _MXU_DTYPE),
            jax.ShapeDtypeStruct((heads, nkeys, t), _MXU_DTYPE),
        ],
        scratch_shapes=[
            pltpu.VMEM((hc, nkeys, tb), F32),
            pltpu.VMEM((2, _PK_TOPK, heads, _LANES), F32),
            pltpu.VMEM((heads, _PK_TOPK, _SUBLANES, _LANES), F32),
        ],
        compiler_params=_params(("parallel",)),
        name="peer_route",
    )(x2d, gain, wq, keys)


def _gelu(x):
    return 0.5 * x * (1.0 + lax.erf(x * (1.0 / math.sqrt(2.0))))


def _peer_dense_kernel(h_ref, u_ref, vt_ref, ea_ref, ca_ref, eb_ref, rb_ref, x_ref, o_ref, acc_ref,
                       hu_ref, w_ref, *, heads, nkeys):
    e = pl.program_id(1)
    tb = h_ref.shape[1]
    a_t = ea_ref.shape[1]
    wdt = vt_ref.dtype
    pk = _SUBLANES * (4 // jnp.dtype(wdt).itemsize)
    zero = jnp.zeros((pk, _LANES), wdt)
    mc = _pick(tb, (512, 256))

    @pl.when(e == 0)
    def _():
        acc_ref[...] = jnp.zeros_like(acc_ref)

    mr = 2 * nkeys
    for c in range(tb // mc):
        toks = slice(c * mc, (c + 1) * mc)
        for r in range(a_t * nkeys // mr):
            rows = slice(r * mr, (r + 1) * mr)
            hu_ref[rows, toks] = jnp.dot(u_ref[rows, :], h_ref[:, toks], preferred_element_type=F32)

    for c in range(tb // mc):
        toks = slice(c * mc, (c + 1) * mc)
        for a0 in range(0, a_t, 2):
            for tc in range(mc // _LANES):
                lanes = slice(c * mc + tc * _LANES, c * mc + (tc + 1) * _LANES)
                accs = [[None] * (nkeys // pk) for _ in range(2)]
                for hd in range(heads):
                    ea_b = [_gate_rows(ea_ref, hd, a0 + i, lanes, wdt) for i in range(2)]
                    ca_b = [_gate_rows(ca_ref, hd, a0 + i, lanes, wdt) for i in range(2)]
                    for bc in range(nkeys // pk):
                        rows = slice(bc * pk, (bc + 1) * pk)
                        rbv, ebv = rb_ref[hd, rows, lanes], eb_ref[hd, rows, lanes]
                        for i in range(2):
                            term = ea_b[i] * lax.clamp(zero, ca_b[i] - rbv, ebv)
                            accs[i][bc] = term if accs[i][bc] is None else accs[i][bc] + term
                for i in range(2):
                    for bc in range(nkeys // pk):
                        erows = slice((a0 + i) * nkeys + bc * pk, (a0 + i) * nkeys + (bc + 1) * pk)
                        w_ref[erows, lanes] = accs[i][bc] * _gelu(hu_ref[erows, lanes].astype(wdt))
        acc_ref[:, toks] += jnp.dot(vt_ref[...], w_ref[:, toks], preferred_element_type=F32)

    @pl.when(e == pl.num_programs(1) - 1)
    def _():
        o_ref[...] = x_ref[...] + acc_ref[...].T


def _peer_dense(h, u, vt, ea, ca, eb, rb, x2d, *, tb, a_t):
    t, d = x2d.shape
    heads, nkeys, _ = ea.shape
    e_t = a_t * nkeys
    n_e = u.shape[0] // e_t
    kern = functools.partial(_peer_dense_kernel, heads=heads, nkeys=nkeys)
    gate_a = pl.BlockSpec((heads, a_t, tb), lambda i, e: (0, e, i))
    gate_b = pl.BlockSpec((heads, nkeys, tb), lambda i, e: (0, 0, i))
    return pl.pallas_call(
        kern,
        grid=(t // tb, n_e),
        in_specs=[
            pl.BlockSpec((d, tb), lambda i, e: (0, i)),
            pl.BlockSpec((e_t, d), lambda i, e: (e, 0)),
            pl.BlockSpec((d, e_t), lambda i, e: (0, e)),
            gate_a, gate_a, gate_b, gate_b,
            pl.BlockSpec((tb, d), lambda i, e: (i, 0)),
        ],
        out_specs=pl.BlockSpec((tb, d), lambda i, e: (i, 0)),
        out_shape=jax.ShapeDtypeStruct((t, d), F32),
        scratch_shapes=[pltpu.VMEM((d, tb), F32), pltpu.VMEM((e_t, tb), F32), pltpu.VMEM((e_t, tb), _MXU_DTYPE)],
        compiler_params=_params(("parallel", "arbitrary")),
        name="peer_dense",
    )(h, u, vt, ea, ca, eb, rb, x2d)


def _peer_layer(x2d, gain, wq, keys, u, vt, *, tb_route, tb_dense, a_t):
    h, ea, ca, eb, rb = _peer_route(x2d, gain, wq, keys, tb=tb_route)
    return _peer_dense(h, u, vt, ea, ca, eb, rb, x2d, tb=tb_dense, a_t=a_t)


def _norm_proj_kernel(x_ref, g_ref, w_ref, o_ref, hb_ref):
    @pl.when(pl.program_id(1) == 0)
    def _():
        hb_ref[...] = _rms(x_ref[...], g_ref[...]).astype(hb_ref.dtype)

    o_ref[...] = jnp.dot(hb_ref[...], w_ref[...], preferred_element_type=F32)


def _norm_proj(x2d, gain, w, *, tm, tn):
    t, d = x2d.shape
    n = w.shape[1]
    return pl.pallas_call(
        _norm_proj_kernel,
        grid=(t // tm, n // tn),
        in_specs=[
            pl.BlockSpec((tm, d), lambda i, j: (i, 0)),
            pl.BlockSpec((1, d), lambda i, j: (0, 0)),
            pl.BlockSpec((d, tn), lambda i, j: (0, j)),
        ],
        out_specs=pl.BlockSpec((tm, tn), lambda i, j: (i, j)),
        out_shape=jax.ShapeDtypeStruct((t, n), F32),
        scratch_shapes=[pltpu.VMEM((tm, d), _MXU_DTYPE)],
        compiler_params=_params(("parallel", "arbitrary")),
        name="norm_proj",
    )(x2d, gain, w)


def _gated_out_kernel(o_ref, g_ref, gain_ref, w_ref, x_ref, y_ref, a_ref, *, dv):
    vd = o_ref.shape[1]
    for hd in range(vd // dv):
        cols = slice(hd * dv, (hd + 1) * dv)
        o = o_ref[:, cols]
        g = g_ref[:, cols]
        a_ref[:, cols] = (_rms(o, gain_ref[:, cols]) * (g * jax.nn.sigmoid(g))).astype(a_ref.dtype)
    y_ref[...] = x_ref[...] + jnp.dot(a_ref[...], w_ref[...], preferred_element_type=F32)


def _gated_out(o, p, g_block, gain, w, x2d, *, dv, tm):
    t, vd = o.shape
    d = x2d.shape[1]
    return pl.pallas_call(
        functools.partial(_gated_out_kernel, dv=dv),
        grid=(t // tm,),
        in_specs=[
            pl.BlockSpec((tm, vd), lambda i: (i, 0)),
            pl.BlockSpec((tm, vd), lambda i: (i, g_block)),
            pl.BlockSpec((1, vd), lambda i: (0, 0)),
            pl.BlockSpec((vd, d), lambda i: (0, 0)),
            pl.BlockSpec((tm, d), lambda i: (i, 0)),
        ],
        out_specs=pl.BlockSpec((tm, d), lambda i: (i, 0)),
        out_shape=jax.ShapeDtypeStruct((t, d), F32),
        scratch_shapes=[pltpu.VMEM((tm, vd), _MXU_DTYPE)],
        compiler_params=_params(("parallel",)),
        name="gated_out",
    )(o, p, gain, w, x2d)


_ROPE_THETA = 10000.0


def _rope_kernel(pos_ref, cr_ref, sr_ref, ta_ref, *, rt_half, at_half):
    pos = pos_ref[...]
    lane = lax.broadcasted_iota(jnp.int32, pos.shape, 1)
    inv = jnp.exp(lane.astype(F32) * (-math.log(_ROPE_THETA) / rt_half))
    ang = pos * inv
    cr_ref[...] = jnp.cos(ang)
    sr_ref[...] = jnp.sin(ang)
    j = lane % (2 * at_half)
    inv = jnp.exp((j % at_half).astype(F32) * (-math.log(_ROPE_THETA) / at_half))
    ang = pos * inv
    sn = jnp.sin(ang)
    ta_ref[:, :_LANES] = jnp.cos(ang)
    ta_ref[:, _LANES:2 * _LANES] = jnp.where(j < at_half, -sn, 0.0)
    ta_ref[:, 2 * _LANES:] = jnp.where(j < at_half, 0.0, sn)


def _rope_tables(pos_b, *, rt_half, at_half, tm):
    t = pos_b.shape[0]
    spec = pl.BlockSpec((tm, _LANES), lambda i: (i, 0))
    return pl.pallas_call(
        functools.partial(_rope_kernel, rt_half=rt_half, at_half=at_half),
        grid=(t // tm,),
        in_specs=[spec],
        out_specs=[spec, spec, pl.BlockSpec((tm, 3 * _LANES), lambda i: (i, 0))],
        out_shape=[jax.ShapeDtypeStruct((t, _LANES), F32)] * 2 + [jax.ShapeDtypeStruct((t, 3 * _LANES), F32)],
        compiler_params=_params(("parallel",)),
        name="rope_tables",
    )(pos_b)


_HG_CHUNK = 64
_HG_SUB = 16
_TINY = 1e-30
_TN = (((0,), (0,)), ((), ()))


def _split_dot(a, b_f32):
    hi = b_f32.astype(BF16)
    lo = (b_f32 - hi.astype(F32)).astype(BF16)
    return (jnp.dot(a, hi, preferred_element_type=F32) + jnp.dot(a, lo, preferred_element_type=F32))


def _hgrn_kernel(q_ref, f_ref, i_ref, lb_ref, o_ref, st_ref, *, layer, heads):
    c, sc = _HG_CHUNK, _HG_SUB
    nchunk = q_ref.shape[0] // c
    dk = q_ref.shape[1] // heads

    @pl.when(pl.program_id(1) == 0)
    def _():
        st_ref[...] = jnp.zeros_like(st_ref)

    lbs = lb_ref[...]
    e = jnp.exp(lbs - jnp.max(lbs, axis=0, keepdims=True))
    soft = e / jnp.sum(e, axis=0, keepdims=True)
    lb_all = jnp.zeros_like(soft[0:1])
    for l in range(1, layer + 1):
        lb_all = lb_all + soft[l:l + 1]

    r_i = lax.broadcasted_iota(jnp.int32, (c, c), 0)
    c_i = lax.broadcasted_iota(jnp.int32, (c, c), 1)
    tri = (r_i >= c_i).astype(BF16)
    sub_t = lax.broadcasted_iota(jnp.int32, (sc, 1), 0)

    def head_chunk(rows, hd):
        cols = slice(hd * dk, (hd + 1) * dk)
        lb = lb_all[:, cols]
        qr = q_ref[rows, cols]
        fr = f_ref[rows, cols]
        v = i_ref[rows, cols]
        q = qr * jax.nn.sigmoid(qr)
        forget = lb + (1.0 - lb) * jax.nn.sigmoid(fr)
        logf = jnp.log(jnp.maximum(forget, _TINY))
        k = (1.0 - lb) * jax.nn.sigmoid(-fr)
        if _MXU_DTYPE == BF16:
            b = _split_dot(tri, logf)
        else:
            b = jnp.dot(tri.astype(F32), logf, preferred_element_type=F32)
        st = st_ref[hd]
        o = lax.dot_general((q * jnp.exp(b)).astype(_MXU_DTYPE), st.astype(_MXU_DTYPE), _NT,
                            preferred_element_type=F32)
        vb = v.astype(_MXU_DTYPE)
        outs = []
        for i in range(c // sc):
            lo, hi = i * sc, (i + 1) * sc
            qi, bi = q[lo:hi], b[lo:hi]
            oi = o[lo:hi]
            if i > 0:
                ref_b = b[lo - 1:lo]
                qt = (qi * jnp.exp(bi - ref_b)).astype(_MXU_DTYPE)
                kt = (k[:lo] * jnp.exp(ref_b - b[:lo])).astype(_MXU_DTYPE)
                att = lax.dot_general(qt, kt, _NT, preferred_element_type=F32)
                oi = oi + jnp.dot(att.astype(_MXU_DTYPE), vb[:lo], preferred_element_type=F32)
            for s in range(sc):
                dec = jnp.exp(bi - bi[s:s + 1])
                col = jnp.sum(qi * k[lo + s:lo + s + 1] * dec, axis=1, keepdims=True)
                col = jnp.where(sub_t >= s, col, 0.0)
                oi = oi + col * v[lo + s:lo + s + 1]
            outs.append(oi)
        o_ref[rows, cols] = jnp.concatenate(outs, axis=0)
        b_last = b[c - 1:c]
        kd = (k * jnp.exp(b_last - b)).astype(_MXU_DTYPE)
        st_ref[hd] = jnp.exp(b_last) * st + lax.dot_general(vb, kd, _TN, preferred_element_type=F32)

    def chunk(ci, carry):
        rows = pl.ds(pl.multiple_of(ci * c, c), c)
        for hd in range(heads):
            head_chunk(rows, hd)
        return carry

    lax.fori_loop(0, nchunk, chunk, 0, unroll=4)


def _hgrn_scan(p, hg_lb, *, batch, seq, heads, dk, layer, blk):
    t = p.shape[0]
    nblk = seq // blk
    width = heads * dk
    row = lambda b, s: b * nblk + s
    return pl.pallas_call(
        functools.partial(_hgrn_kernel, layer=layer, heads=heads),
        grid=(batch, nblk),
        in_specs=[
            pl.BlockSpec((blk, width), lambda b, s: (row(b, s), 0)),
            pl.BlockSpec((blk, width), lambda b, s: (row(b, s), 1)),
            pl.BlockSpec((blk, width), lambda b, s: (row(b, s), 2)),
            pl.BlockSpec((hg_lb.shape[0], width), lambda b, s: (0, 0)),
        ],
        out_specs=pl.BlockSpec((blk, width), lambda b, s: (row(b, s), 0)),
        out_shape=jax.ShapeDtypeStruct((t, width), F32),
        scratch_shapes=[pltpu.VMEM((heads, dk, dk), F32)],
        compiler_params=_params(("parallel", "arbitrary")),
        name="hgrn_scan",
    )(p, p, p, hg_lb)


_RT_CHUNK = 128


def _retention_kernel(q_ref, k_ref, v_ref, cos_ref, sin_ref, o_ref, st_ref, *, heads):
    c = _RT_CHUNK
    nchunk = q_ref.shape[0] // c
    dk = q_ref.shape[1] // heads
    dv = v_ref.shape[1] // heads
    half = dk // 2

    @pl.when(pl.program_id(1) == 0)
    def _():
        st_ref[...] = jnp.zeros_like(st_ref)

    t_i = lax.broadcasted_iota(jnp.int32, (c, c), 0)
    s_i = lax.broadcasted_iota(jnp.int32, (c, c), 1)
    rel = (t_i - s_i).astype(F32)
    idx = lax.broadcasted_iota(jnp.int32, (c, _LANES), 0).astype(F32)
    consts = []
    for hd in range(heads):
        lg = math.log1p(-(2.0 ** (-5.0 - hd)))
        decay = jnp.where(rel >= 0, jnp.exp(lg * jnp.maximum(rel, 0.0)), 0.0)
        consts.append((decay, jnp.exp(lg * (idx + 1.0)), jnp.exp(lg * (c - 1.0 - idx)), math.exp(lg * c)))

    def rope(x, cos, sin):
        x1, x2 = x[:, :half], x[:, half:]
        return x1 * cos - x2 * sin, x2 * cos + x1 * sin

    cat = lambda a, b: jnp.concatenate([a, b], axis=1).astype(_MXU_DTYPE)

    def head_chunk(rows, hd, cos, sin):
        decay, q_dec, k_dec, chunk_dec = consts[hd]
        q1, q2 = rope(q_ref[rows, hd * dk:(hd + 1) * dk], cos, sin)
        k1, k2 = rope(k_ref[rows, hd * dk:(hd + 1) * dk], cos, sin)
        scale = dk ** -0.5
        k1, k2 = k1 * scale, k2 * scale
        qb, kb = cat(q1, q2), cat(k1, k2)
        vb = v_ref[rows, hd * dv:(hd + 1) * dv].astype(_MXU_DTYPE)
        att = lax.dot_general(qb, kb, _NT, preferred_element_type=F32) * decay
        o = jnp.dot(att.astype(_MXU_DTYPE), vb, preferred_element_type=F32)
        st = st_ref[hd]
        o = o + jnp.dot(cat(q1 * q_dec, q2 * q_dec), st.astype(_MXU_DTYPE), preferred_element_type=F32)
        o_ref[rows, hd * dv:(hd + 1) * dv] = o
        st_ref[hd] = chunk_dec * st + lax.dot_general(cat(k1 * k_dec, k2 * k_dec), vb, _TN,
                                                      preferred_element_type=F32)

    def chunk(ci, carry):
        rows = pl.ds(pl.multiple_of(ci * c, c), c)
        cos, sin = cos_ref[rows, :], sin_ref[rows, :]
        for hd in range(heads):
            head_chunk(rows, hd, cos, sin)
        return carry

    lax.fori_loop(0, nchunk, chunk, 0, unroll=2)


def _retention_scan(p, cos, sin, *, batch, seq, heads, dk, dv, blk):
    t = p.shape[0]
    nblk = seq // blk
    row = lambda b, s: b * nblk + s
    kd, vd = heads * dk, heads * dv
    assert 2 * kd == vd and dk // 2 == _LANES and _RT_CHUNK == _LANES
    return pl.pallas_call(
        functools.partial(_retention_kernel, heads=heads),
        grid=(batch, nblk),
        in_specs=[
            pl.BlockSpec((blk, kd), lambda b, s: (row(b, s), 0)),
            pl.BlockSpec((blk, kd), lambda b, s: (row(b, s), 1)),
            pl.BlockSpec((blk, vd), lambda b, s: (row(b, s), 1)),
            pl.BlockSpec((blk, _LANES), lambda b, s: (row(b, s), 0)),
            pl.BlockSpec((blk, _LANES), lambda b, s: (row(b, s), 0)),
        ],
        out_specs=pl.BlockSpec((blk, vd), lambda b, s: (row(b, s), 0)),
        out_shape=jax.ShapeDtypeStruct((t, vd), F32),
        scratch_shapes=[pltpu.VMEM((heads, dk, dv), F32)],
        compiler_params=_params(("parallel", "arbitrary")),
        name="retention_scan",
    )(p, p, p, cos, sin)


_AT_BLOCK = 128
_MASK_VALUE = -1e30


def _attn_kernel(q_ref, kp_ref, k_ref, vp_ref, v_ref, t0_ref, t1_ref, t2_ref, p0_ref, p1_ref, p2_ref,
                 gq_ref, gk_ref, o_ref, l_ref, *, head_dim, span, dil, nsub):
    blk = _AT_BLOCK
    half = head_dim // 2
    lane = lax.broadcasted_iota(jnp.int32, (1, _LANES), 1)
    qi = lax.broadcasted_iota(jnp.int32, (blk, 2 * blk), 0) + blk
    ki = lax.broadcasted_iota(jnp.int32, (blk, 2 * blk), 1)
    band = (ki <= qi) & (ki >= qi - span)
    first_lo = jnp.where(pl.program_id(2) == 0, blk, 0)
    band_first = (ki <= qi) & (ki >= jnp.maximum(qi - span, first_lo))
    scale = head_dim ** -0.5
    r_i = lax.broadcasted_iota(jnp.int32, (_LANES, _LANES), 0) // head_dim
    c_i = lax.broadcasted_iota(jnp.int32, (_LANES, _LANES), 1) // head_dim
    seg = jnp.where(r_i == c_i, 1.0 / head_dim, 0.0)
    gq, gk = gq_ref[...], gk_ref[...]

    nh = _LANES // head_dim
    gain_qk = jnp.concatenate([jnp.broadcast_to(gq, (blk, _LANES)), jnp.broadcast_to(gk, (blk, _LANES))], axis=0)
    heads_of = lane // head_dim
    segb = seg.astype(_MXU_DTYPE)

    def prep(x, gain, tab):
        msq = jnp.dot((x * x).astype(_MXU_DTYPE), segb, preferred_element_type=F32)
        y = x * lax.rsqrt(msq + _EPS) * gain
        cos, s1, s2 = tab
        y = y * cos + pltpu.roll(y, _LANES - half, 1) * s1 + pltpu.roll(y, half, 1) * s2
        return y.astype(_MXU_DTYPE)

    def residue(r, carry):
        def take(ref, j):
            if dil == 1:
                return ref[j * blk:(j + 1) * blk, :]
            return ref[pl.ds(r + j * blk * dil, blk, stride=dil), :]

        kprev = prep(take(kp_ref, 0), gk, [take(t, 0) for t in (p0_ref, p1_ref, p2_ref)])
        vprev = take(vp_ref, 0).astype(_MXU_DTYPE)
        for j in range(nsub):
            tab = [take(t, j) for t in (t0_ref, t1_ref, t2_ref)]
            qk = prep(jnp.concatenate([take(q_ref, j), take(k_ref, j)], axis=0), gain_qk,
                      [jnp.concatenate([t, t], axis=0) for t in tab])
            q, kcur = qk[:blk], qk[blk:]
            vcur = take(v_ref, j).astype(_MXU_DTYPE)
            kk = jnp.concatenate([kprev, kcur], axis=0)
            vv = jnp.concatenate([vprev, vcur], axis=0)
            mask = band_first if j == 0 else band
            qh = jnp.concatenate([jnp.where(heads_of == hd, q, jnp.zeros_like(q)) for hd in range(nh)], axis=0)
            s = lax.dot_general(qh, kk, _NT, preferred_element_type=F32) * scale
            s = jnp.where(jnp.concatenate([mask] * nh, axis=0), s, _MASK_VALUE)
            m = jnp.max(s, axis=-1, keepdims=True)
            pr = jnp.exp(s - m)
            l = jnp.sum(pr, axis=-1, keepdims=True)
            o = jnp.dot(pr.astype(_MXU_DTYPE), vv, preferred_element_type=F32) / l
            lse = m + jnp.log(l)
            o_pair = o[:blk]
            l_pair = jnp.broadcast_to(lse[:blk], o_pair.shape)
            for hd in range(1, nh):
                o_pair = jnp.where(heads_of == hd, o[hd * blk:(hd + 1) * blk], o_pair)
                l_pair = jnp.where(heads_of == hd, lse[hd * blk:(hd + 1) * blk], l_pair)
            if dil == 1:
                o_ref[j * blk:(j + 1) * blk, :] = o_pair
                l_ref[j * blk:(j + 1) * blk, :] = l_pair
            else:
                o_ref[pl.ds(r + j * blk * dil, blk, stride=dil), :] = o_pair
                l_ref[pl.ds(r + j * blk * dil, blk, stride=dil), :] = l_pair
            kprev, vprev = kcur, vcur
        return carry

    if dil == 1:
        residue(0, 0)
    else:
        per = 2 if nsub > 1 else 4

        def several(i, carry):
            for k in range(per):
                residue(per * i + k, carry)
            return carry

        lax.fori_loop(0, dil // per, several, 0)


def _attn_group(p, tabs, gq, gk, *, batch, seq, group, ngroup, heads, head_dim, window, dilation):
    t = p.shape[0]
    d = dilation
    width = heads * head_dim
    assert seq % (d * _AT_BLOCK) == 0 and window // d <= _AT_BLOCK
    nb = seq // d // _AT_BLOCK
    nsub = max(1, min(4, nb, 4096 // (_AT_BLOCK * d)))
    nstep = nb // nsub
    rows = nsub * _AT_BLOCK * d
    prow = _AT_BLOCK * d
    pairs = width // _LANES
    col = lambda kind, hp: (kind * ngroup + group) * pairs + hp
    main = lambda b, n: b * nstep + n
    prev = lambda b, n: jnp.maximum((b * nstep + n) * nsub - 1, 0)
    out_spec = pl.BlockSpec((rows, _LANES), lambda b, hp, n: (main(b, n), hp))
    return pl.pallas_call(
        functools.partial(_attn_kernel, head_dim=head_dim, span=window // d, dil=d, nsub=nsub),
        grid=(batch, pairs, nstep),
        in_specs=[
            pl.BlockSpec((rows, _LANES), lambda b, hp, n: (main(b, n), col(0, hp))),
            pl.BlockSpec((prow, _LANES), lambda b, hp, n: (prev(b, n), col(1, hp))),
            pl.BlockSpec((rows, _LANES), lambda b, hp, n: (main(b, n), col(1, hp))),
            pl.BlockSpec((prow, _LANES), lambda b, hp, n: (prev(b, n), col(2, hp))),
            pl.BlockSpec((rows, _LANES), lambda b, hp, n: (main(b, n), col(2, hp))),
            pl.BlockSpec((rows, _LANES), lambda b, hp, n: (main(b, n), 0)),
            pl.BlockSpec((rows, _LANES), lambda b, hp, n: (main(b, n), 1)),
            pl.BlockSpec((rows, _LANES), lambda b, hp, n: (main(b, n), 2)),
            pl.BlockSpec((prow, _LANES), lambda b, hp, n: (prev(b, n), 0)),
            pl.BlockSpec((prow, _LANES), lambda b, hp, n: (prev(b, n), 1)),
            pl.BlockSpec((prow, _LANES), lambda b, hp, n: (prev(b, n), 2)),
            pl.BlockSpec((1, _LANES), lambda b, hp, n: (0, 0)),
            pl.BlockSpec((1, _LANES), lambda b, hp, n: (0, 0)),
        ],
        out_specs=[out_spec, out_spec],
        out_shape=[jax.ShapeDtypeStruct((t, width), F32)] * 2,
        compiler_params=_params(("parallel", "parallel", "arbitrary")),
        name=f"attn_group{group}",
    )(p, p, p, p, p, tabs, tabs, tabs, tabs, tabs, tabs, gq, gk)


def _attn_out_kernel(*refs, ngroup):
    o_refs, l_refs = refs[:ngroup], refs[ngroup:2 * ngroup]
    w_ref, x_ref, y_ref = refs[2 * ngroup:]
    ls = [r[...] for r in l_refs]
    m = functools.reduce(jnp.maximum, ls)
    es = [jnp.exp(l - m) for l in ls]
    den = functools.reduce(lambda a, b: a + b, es)
    o = functools.reduce(lambda a, b: a + b, [e * r[...] for e, r in zip(es, o_refs)]) / den
    y_ref[...] = x_ref[...] + jnp.dot(o.astype(_MXU_DTYPE), w_ref[...], preferred_element_type=F32)


def _attn_out(os_, ls_, w, x2d, *, tm):
    t, width = os_[0].shape
    d = x2d.shape[1]
    ng = len(os_)
    blk = pl.BlockSpec((tm, width), lambda i: (i, 0))
    return pl.pallas_call(
        functools.partial(_attn_out_kernel, ngroup=ng),
        grid=(t // tm,),
        in_specs=[blk] * (2 * ng) + [pl.BlockSpec((width, d), lambda i: (0, 0)),
                                     pl.BlockSpec((tm, d), lambda i: (i, 0))],
        out_specs=pl.BlockSpec((tm, d), lambda i: (i, 0)),
        out_shape=jax.ShapeDtypeStruct((t, d), F32),
        compiler_params=_params(("parallel",)),
        name="attn_out",
    )(*os_, *ls_, w, x2d)


_HG_HEADS, _HG_DK = 8, 128
_AT_GROUPS = ((128, 1), (512, 4), (2048, 16))
_AT_HEADS, _AT_HEAD_DIM = 8, 64
_RT_HEADS = 4
_N_MIXERS = 3


def _pick(n, candidates):
    for c in candidates:
        if n % c == 0:
            return c
    return n


def _tiles(t, seq):
    return dict(
        proj_rows=_pick(t, (1024, 512, 256, 128)),
        out_rows=_pick(t, (512, 256, 128)),
        scan_blk=min(512, seq),
        route_tb=_pick(t, (512, 256, 128)),
        dense_tb=_pick(t, (512, 256, 128)),
        dense_at=16,
    )


def _hgrn_mixer(x2d, gain, w_in, hg_lb, o_gain, w_out, *, batch, seq, layer, tiles):
    p = _norm_proj(x2d, gain, w_in, tm=tiles["proj_rows"], tn=_pick(w_in.shape[1], (1024, 512)))
    o = _hgrn_scan(p, hg_lb, batch=batch, seq=seq, heads=_HG_HEADS, dk=_HG_DK, layer=layer, blk=tiles["scan_blk"])
    return _gated_out(o, p, 3, o_gain, w_out, x2d, dv=_HG_DK, tm=tiles["out_rows"])


def _retention_mixer(x2d, gain, w_in, o_gain, w_out, cos, sin, *, batch, seq, tiles):
    d = x2d.shape[1]
    dk = d // _RT_HEADS
    dv = 2 * dk
    p = _norm_proj(x2d, gain, w_in, tm=tiles["proj_rows"], tn=_pick(w_in.shape[1], (1536, 1024, 512)))
    o = _retention_scan(p, cos, sin, batch=batch, seq=seq, heads=_RT_HEADS, dk=dk, dv=dv, blk=tiles["scan_blk"])
    return _gated_out(o, p, 2, o_gain, w_out, x2d, dv=dv, tm=tiles["out_rows"])


def _attn_mixer(x2d, gain, w_in, q_gain, k_gain, w_out, tabs, *, batch, seq, tiles):
    ng = len(_AT_GROUPS)
    p = _norm_proj(x2d, gain, w_in, tm=tiles["proj_rows"], tn=_pick(w_in.shape[1], (1536, 512)))
    os_, ls_ = [], []
    for g, (window, dilation) in enumerate(_AT_GROUPS):
        gq = jnp.tile(q_gain[g], _LANES // _AT_HEAD_DIM)[None, :]
        gk = jnp.tile(k_gain[g], _LANES // _AT_HEAD_DIM)[None, :]
        o, lse = _attn_group(p, tabs, gq, gk, batch=batch, seq=seq, group=g, ngroup=ng, heads=_AT_HEADS,
                             head_dim=_AT_HEAD_DIM, window=window, dilation=dilation)
        os_.append(o)
        ls_.append(lse)
    return _attn_out(os_, ls_, w_out, x2d, tm=tiles["out_rows"])


def kernel(x, positions, mix_norm, ffn_norm, hg_lb, hg_w_in, hg_onorm, hg_w_out, at_w_in, at_qnorm, at_knorm,
           at_w_out, rt_w_in, rt_onorm, rt_w_out, pk_w_q, pk_keys, pk_u, pk_v):
    b, s, d = x.shape
    depth = mix_norm.shape[0]
    tiles = _tiles(b * s, s)
    x2d = x.reshape(b * s, d)
    pos_b = jnp.broadcast_to(positions.reshape(b * s, 1).astype(F32), (b * s, _LANES))
    rt_dk = d // _RT_HEADS
    cos_r, sin_r, tabs_a = _rope_tables(pos_b, rt_half=rt_dk // 2, at_half=_AT_HEAD_DIM // 2,
                                        tm=tiles["out_rows"])
    ia = ib = ic = 0
    for layer in range(depth):
        kind = layer % _N_MIXERS
        gain = mix_norm[layer][None, :]
        if kind == 0:
            x2d = _hgrn_mixer(x2d, gain, hg_w_in[ia].astype(_MXU_DTYPE), hg_lb, hg_onorm[ia][None, :],
                              hg_w_out[ia].astype(_MXU_DTYPE), batch=b, seq=s, layer=layer, tiles=tiles)
            ia += 1
        elif kind == 1:
            x2d = _attn_mixer(x2d, gain, at_w_in[ib].astype(_MXU_DTYPE), at_qnorm[ib], at_knorm[ib],
                              at_w_out[ib].astype(_MXU_DTYPE), tabs_a, batch=b, seq=s, tiles=tiles)
            ib += 1
        else:
            x2d = _retention_mixer(x2d, gain, rt_w_in[ic].astype(_MXU_DTYPE), rt_onorm[ic][None, :],
                                   rt_w_out[ic].astype(_MXU_DTYPE), cos_r, sin_r, batch=b, seq=s, tiles=tiles)
            ic += 1
        keys = pk_keys[layer].reshape(-1, pk_keys.shape[-2], pk_keys.shape[-1]).astype(_MXU_DTYPE)
        x2d = _peer_layer(x2d, ffn_norm[layer][None, :], pk_w_q[layer].astype(_MXU_DTYPE), keys,
                          pk_u[layer].astype(_MXU_DTYPE), pk_v[layer].T.astype(_MXU_DTYPE),
                          tb_route=tiles["route_tb"], tb_dense=tiles["dense_tb"], a_t=tiles["dense_at"])
    return x2d.reshape(b, s, d)
```
